```python
import jax
import jax.numpy as jnp
from jax import lax
import numpy as np

D_MODEL = 1024
BATCH = 16
SEQ = 4096
DEPTH = 4

CTX_LEN = 256
GRID_W = 64
N_MOD = 6
EPS = 1e-6
MASK_VALUE = -1e30
CONV_W = 512
CONV_K = 3
HG_HEADS = 4
HG_DK = 128
HG_DV = 128
HG_WK = HG_HEADS * HG_DK
HG_WV = HG_HEADS * HG_DV
HG_CHUNK = 64
NA_HEADS = 8
NA_HD = 64
NA_W = NA_HEADS * NA_HD
WIN_R = 8
WIN_C = 16
QB_C = 16
KB_C = WIN_C + QB_C
N_EXPERTS = 16
CAP_FACTOR = 2
F_EXPERT = 2048
IN_WIDTHS = (CONV_W, CONV_W, CONV_W, HG_WK, HG_WV, HG_WK, HG_WK, HG_WV, NA_W, NA_W, NA_W, D_MODEL, D_MODEL, D_MODEL)
N_IN = sum(IN_WIDTHS)
IN_SPLITS = tuple(sum(IN_WIDTHS[:i + 1]) for i in range(len(IN_WIDTHS) - 1))

kernel_name = 'hybrid_conv_hgrn2_natten_ecmoe_dit'


def rmsnorm(x, w):
    xf = x.astype(jnp.float32)
    y = xf * lax.rsqrt(jnp.mean(xf * xf, axis=-1, keepdims=True) + EPS)
    return (y * w.astype(jnp.float32)).astype(x.dtype)


def heads(a, n):
    return a.reshape(a.shape[:-1] + (n, a.shape[-1] // n))


def modulated_norm(x, w, shift, scale):
    return rmsnorm(x, w) * (1 + scale) + shift


def mixer_inputs(x, mod, norm_w, w_in):
    h = modulated_norm(x, norm_w, mod[:, :, 0], mod[:, :, 1])
    return jnp.split(h @ w_in, IN_SPLITS, axis=-1)


def short_conv_mixer(gate_b, gate_c, u, conv_w):
    v = gate_c * u
    y = lax.conv_general_dilated(v, conv_w[:, None, :].astype(v.dtype), window_strides=(1,),
                                 padding=((CONV_K // 2, CONV_K // 2),),
                                 dimension_numbers=('NWC', 'WIO', 'NWC'),
                                 feature_group_count=v.shape[-1])
    return gate_b * y


def _chunks(a):
    b, t, h, d = a.shape
    return a.reshape(b, t // HG_CHUNK, HG_CHUNK, h, d).transpose(1, 0, 3, 2, 4)


def gla_chunk_scan(q, k, v, log_g, s0):
    b_, t_, h_, _ = q.shape
    tri = jnp.tril(jnp.ones((HG_CHUNK, HG_CHUNK), dtype=bool))[:, :, None]

    def step(s, xs):
        qc, kc, vc, gc = xs
        bcum = jnp.cumsum(gc, axis=2)
        o_inter = jnp.einsum('bhtk,bhkv->bhtv', qc * jnp.exp(bcum), s)
        diff = bcum[:, :, :, None, :] - bcum[:, :, None, :, :]
        decay = jnp.where(tri, jnp.exp(jnp.where(tri, diff, 0.0)), 0.0)
        attn = jnp.einsum('bhtk,bhsk,bhtsk->bhts', qc, kc, decay)
        o_intra = jnp.einsum('bhts,bhsv->bhtv', attn, vc)
        b_end = bcum[:, :, -1:, :]
        s_new = jnp.exp(b_end[:, :, 0, :])[..., None] * s + jnp.einsum('bhsk,bhsv->bhkv', kc * jnp.exp(b_end - bcum), vc)
        return s_new, o_inter + o_intra

    s_fin, o = lax.scan(step, s0, (_chunks(q), _chunks(k), _chunks(v), _chunks(log_g)))
    o = o.transpose(1, 0, 3, 2, 4).reshape(b_, t_, h_, -1)
    return o, s_fin


def hgrn2_gates(f, lb):
    f = f.astype(jnp.float32)
    sig = jax.nn.sigmoid(f)
    log_g = jnp.log(lb + (1.0 - lb) * sig)
    k = (1.0 - lb) * (1.0 - sig)
    return heads(k, HG_HEADS), heads(log_g, HG_HEADS)


def hgrn2_direction(q, v, f, qc, vc, fc, lb):
    k, log_g = hgrn2_gates(f, lb)
    kc, log_gc = hgrn2_gates(fc, lb)
    s0 = jnp.zeros((q.shape[0], HG_HEADS, HG_DK, HG_DV), jnp.float32)
    oc, s_ctx = gla_chunk_scan(qc, kc, vc, log_gc, s0)
    o, _ = gla_chunk_scan(q, k, v, log_g, s_ctx)
    return o, oc


def hgrn2_bidirectional(q, i, f_fw, f_bw, qc, ic, fc_fw, fc_bw, lb):
    q, qc = [heads(jax.nn.silu(a.astype(jnp.float32)) * HG_DK ** -0.5, HG_HEADS) for a in (q, qc)]
    v, vc = [heads(a.astype(jnp.float32), HG_HEADS) for a in (i, ic)]
    o_fw, oc_fw = hgrn2_direction(q, v, f_fw, qc, vc, fc_fw, lb[0])
    flip = lambda a: jnp.flip(a, axis=1)
    o_bw, oc_bw = hgrn2_direction(flip(q), flip(v), flip(f_bw), flip(qc), flip(vc), flip(fc_bw), lb[1])
    return o_fw + flip(o_bw), oc_fw + flip(oc_bw)


def hgrn2_readout(o, g, norm_w):
    y = rmsnorm(o, norm_w) * jax.nn.silu(heads(g, HG_HEADS).astype(jnp.float32))
    return y.reshape(y.shape[:2] + (HG_WV,)).astype(g.dtype)


def neighbourhood_attention(q, k, v, kc, vc, rpb):
    b_, s_, h_, hd = q.shape
    rows = s_ // GRID_W
    wr = min(WIN_R, rows)
    ncb = GRID_W // QB_C
    scale = hd ** -0.5
    q = q.reshape(b_, rows, ncb, QB_C, h_, hd).transpose(1, 0, 2, 3, 4, 5)
    k = k.reshape(b_, rows, GRID_W, h_, hd)
    v = v.reshape(b_, rows, GRID_W, h_, hd)
    qcol = jnp.arange(GRID_W).reshape(ncb, QB_C)
    wstart = jnp.clip(qcol - WIN_C // 2, 0, GRID_W - WIN_C)
    kcol = jnp.clip(jnp.arange(ncb) * QB_C - WIN_C // 2, 0, GRID_W - KB_C)[:, None] + jnp.arange(KB_C)
    in_win = (kcol[:, None, :] >= wstart[..., None]) & (kcol[:, None, :] < wstart[..., None] + WIN_C)
    dc = jnp.clip(kcol[:, None, :] - qcol[..., None] + WIN_C - 1, 0, 2 * WIN_C - 2)
    rpb_c = rpb[:, :, dc]
    rstart = jnp.clip(jnp.arange(rows) - WIN_R // 2, 0, rows - wr)

    def row_block(args):
        r, qr = args
        r0 = rstart[r]
        kr = lax.dynamic_slice_in_dim(k, r0, wr, axis=1)[:, :, kcol]
        vr = lax.dynamic_slice_in_dim(v, r0, wr, axis=1)[:, :, kcol]
        dr = r0 + jnp.arange(wr) - r + WIN_R - 1
        bias = rpb_c[:, dr].transpose(0, 2, 3, 1, 4)
        s_loc = jnp.einsum('bnqhd,bwnkhd->bhnqwk', qr, kr).astype(jnp.float32) * scale + bias[None].astype(jnp.float32)
        s_loc = jnp.where(in_win[:, :, None, :], s_loc, MASK_VALUE).reshape(b_, h_, ncb, QB_C, wr * KB_C)
        s_ctx = jnp.einsum('bnqhd,blhd->bhnql', qr, kc).astype(jnp.float32) * scale
        p = jax.nn.softmax(jnp.concatenate([s_loc, s_ctx], axis=-1), axis=-1).astype(v.dtype)
        p_loc = p[..., :wr * KB_C].reshape(b_, h_, ncb, QB_C, wr, KB_C)
        p_ctx = p[..., wr * KB_C:]
        o = jnp.einsum('bhnqwk,bwnkhd->bnqhd', p_loc, vr) + jnp.einsum('bhnql,blhd->bnqhd', p_ctx, vc)
        return o.reshape(b_, GRID_W, h_, hd)

    out = lax.map(row_block, (jnp.arange(rows), q))
    return out.transpose(1, 0, 2, 3, 4).reshape(b_, s_, h_ * hd)


def context_attention(qc, kc, vc):
    s = jnp.einsum('blhd,bmhd->bhlm', qc, kc).astype(jnp.float32) * qc.shape[-1] ** -0.5
    p = jax.nn.softmax(s, axis=-1).astype(vc.dtype)
    o = jnp.einsum('bhlm,bmhd->blhd', p, vc)
    return o.reshape(o.shape[:2] + (-1,))


def merge_branches(y_a, y_b, y_c, g_a, g_b, g_c, w_br_a, w_br_b, w_br_c, w_out):
    m = (jax.nn.sigmoid(g_a) * (y_a @ w_br_a) + jax.nn.sigmoid(g_b) * (y_b @ w_br_b)
         + jax.nn.sigmoid(g_c) * (y_c @ w_br_c))
    return m @ w_out


def expert_choice_ffn(h, w_router, w_e_gate, w_e_up, w_e_down):
    cap = CAP_FACTOR * h.shape[1] // N_EXPERTS

    def route_group(hg):
        aff = jax.nn.softmax((hg @ w_router).astype(jnp.float32), axis=-1)
        wgt, idx = lax.top_k(aff.T, cap)
        xe = hg[idx]
        a = jnp.einsum('ecd,edf->ecf', xe, w_e_gate)
        u = jnp.einsum('ecd,edf->ecf', xe, w_e_up)
        ye = jnp.einsum('ecf,efd->ecd', jax.nn.silu(a) * u, w_e_down) * wgt[..., None].astype(hg.dtype)
        return jnp.zeros_like(hg).at[idx.reshape(-1)].add(ye.reshape(-1, hg.shape[-1]).astype(hg.dtype))

    return lax.map(route_group, h)


def trunk_layer(x, xc, mod, mod_c, w_in, conv_w, lb, hg_norm, q_norm, k_norm, rpb, w_br_a, w_br_b, w_br_c,
                w_out, norm1, norm2, w_router, w_e_gate, w_e_up, w_e_down, last):
    (a_b, a_c, a_u, h_q, h_i, h_ffw, h_fbw, h_g, n_q, n_k, n_v, g_a, g_b, g_c) = mixer_inputs(x, mod, norm1, w_in)
    (ca_b, ca_c, ca_u, ch_q, ch_i, ch_ffw, ch_fbw, ch_g, cn_q, cn_k, cn_v, cg_a, cg_b, cg_c) = mixer_inputs(xc, mod_c, norm1, w_in)
    o_hg, oc_hg = hgrn2_bidirectional(h_q, h_i, h_ffw, h_fbw, ch_q, ch_i, ch_ffw, ch_fbw, lb)
    kc = rmsnorm(heads(cn_k, NA_HEADS), k_norm)
    vc = heads(cn_v, NA_HEADS)
    y_na = neighbourhood_attention(rmsnorm(heads(n_q, NA_HEADS), q_norm), rmsnorm(heads(n_k, NA_HEADS), k_norm),
                                   heads(n_v, NA_HEADS), kc, vc, rpb)
    y_cv = short_conv_mixer(a_b, a_c, a_u, conv_w)
    y_hg = hgrn2_readout(o_hg, h_g, hg_norm)
    x = x + mod[:, :, 2] * merge_branches(y_cv, y_hg, y_na, g_a, g_b, g_c, w_br_a, w_br_b, w_br_c, w_out)
    if not last:
        yc_na = context_attention(rmsnorm(heads(cn_q, NA_HEADS), q_norm), kc, vc)
        yc_cv = short_conv_mixer(ca_b, ca_c, ca_u, conv_w)
        yc_hg = hgrn2_readout(oc_hg, ch_g, hg_norm)
        xc = xc + mod_c[:, :, 2] * merge_branches(yc_cv, yc_hg, yc_na, cg_a, cg_b, cg_c, w_br_a, w_br_b, w_br_c, w_out)
    x = x + mod[:, :, 5] * expert_choice_ffn(modulated_norm(x, norm2, mod[:, :, 3], mod[:, :, 4]),
                                             w_router, w_e_gate, w_e_up, w_e_down)
    if not last:
        xc = xc + mod_c[:, :, 5] * expert_choice_ffn(modulated_norm(xc, norm2, mod_c[:, :, 3], mod_c[:, :, 4]),
                                                     w_router, w_e_gate, w_e_up, w_e_down)
    return x, xc


def setup_inputs(seed: int = 0) -> dict:
    key = jax.random.key(seed)
    ks = jax.random.split(key, 24)
    D = D_MODEL

    def nrm(k, shape, s):
        return jax.random.normal(k, shape, jnp.float32) * s

    return {
        'x': nrm(ks[0], (BATCH, SEQ, D), 1.0),
        'c': nrm(ks[1], (BATCH, D), 1.0),
        'ctx': nrm(ks[2], (BATCH, CTX_LEN, D), 1.0),
        'c_ctx': nrm(ks[3], (D,), 1.0),
        'w_mod': nrm(ks[4], (DEPTH, D, N_MOD * D), 0.5 * D ** -0.5),
        'b_mod': nrm(ks[5], (DEPTH, N_MOD * D), 0.02),
        'norm1': 1.0 + nrm(ks[6], (DEPTH, D), 0.02),
        'w_in': nrm(ks[7], (DEPTH, D, N_IN), D ** -0.5),
        'conv_w': nrm(ks[8], (DEPTH, CONV_K, CONV_W), CONV_K ** -0.5),
        'hg_lb_logits': nrm(ks[9], (DEPTH, 2, HG_WK), 1.0),
        'hg_norm': 1.0 + nrm(ks[10], (DEPTH, HG_DV), 0.02),
        'na_q_norm': 1.0 + nrm(ks[11], (DEPTH, NA_HD), 0.02),
        'na_k_norm': 1.0 + nrm(ks[12], (DEPTH, NA_HD), 0.02),
        'na_rpb': nrm(ks[13], (DEPTH, NA_HEADS, 2 * WIN_R - 1, 2 * WIN_C - 1), 0.02),
        'w_br_a': nrm(ks[14], (DEPTH, CONV_W, D), CONV_W ** -0.5),
        'w_br_b': nrm(ks[15], (DEPTH, HG_WV, D), HG_WV ** -0.5),
        'w_br_c': nrm(ks[16], (DEPTH, NA_W, D), NA_W ** -0.5),
        'w_out': nrm(ks[17], (DEPTH, D, D), D ** -0.5),
        'norm2': 1.0 + nrm(ks[18], (DEPTH, D), 0.02),
        'w_router': nrm(ks[19], (DEPTH, D, N_EXPERTS), D ** -0.5),
        'w_e_gate': nrm(ks[20], (DEPTH, N_EXPERTS, D, F_EXPERT), D ** -0.5),
        'w_e_up': nrm(ks[21], (DEPTH, N_EXPERTS, D, F_EXPERT), D ** -0.5),
        'w_e_down': nrm(ks[22], (DEPTH, N_EXPERTS, F_EXPERT, D), F_EXPERT ** -0.5),
    }


def reference(x, c, ctx, c_ctx, w_mod, b_mod, norm1, w_in, conv_w, hg_lb_logits, hg_norm, na_q_norm, na_k_norm,
              na_rpb, w_br_a, w_br_b, w_br_c, w_out, norm2, w_router, w_e_gate, w_e_up, w_e_down):
    lb_sm = jax.nn.softmax(hg_lb_logits.astype(jnp.float32), axis=0)
    lb_all = jnp.cumsum(lb_sm, axis=0) - lb_sm[0]
    cond = jax.nn.silu(c)
    cond_c = jax.nn.silu(c_ctx)[None]
    xc = ctx
    for l in range(DEPTH):
        mod = (cond @ w_mod[l] + b_mod[l]).reshape(-1, 1, N_MOD, D_MODEL)
        mod_c = (cond_c @ w_mod[l] + b_mod[l]).reshape(1, 1, N_MOD, D_MODEL)
        x, xc = trunk_layer(x, xc, mod, mod_c, w_in[l], conv_w[l], lb_all[l], hg_norm[l], na_q_norm[l],
                            na_k_norm[l], na_rpb[l], w_br_a[l], w_br_b[l], w_br_c[l], w_out[l], norm1[l],
                            norm2[l], w_router[l], w_e_gate[l], w_e_up[l], w_e_down[l], l == DEPTH - 1)
    return x
```

```python
import functools

import jax
import jax.numpy as jnp
from jax import lax
from jax.experimental import pallas as pl
from jax.experimental.pallas import tpu as pltpu

N_MOD = 6
EPS = 1e-6
MASK_VALUE = -1e30
GRID_W = 64
GROUP_W = 512
HG_HEADS = 4
HG_D = 128
HG_CHUNK = 64
HG_SUB = 16
NA_HEADS = 8
NA_HD = 64
WIN_R = 8
WIN_C = 16
CAP_FACTOR = 2
T_CONV_B, T_CONV_C, T_CONV_U, T_HG_Q, T_HG_I, T_HG_FF, T_HG_FB, T_HG_G, T_NA_Q, T_NA_K, T_NA_V = range(11)
N_GATES = 3


def _gate_tiles(d):
    assert (N_GATES * d) % GROUP_W == 0
    return N_GATES * d // GROUP_W

VMEM_LIMIT_BYTES = 48 * 1024 * 1024


def _params(*semantics):
    return pltpu.CompilerParams(dimension_semantics=semantics, vmem_limit_bytes=VMEM_LIMIT_BYTES)


def _silu(v):
    return v * jax.nn.sigmoid(v)


def _bf16(v):
    return v.astype(jnp.bfloat16)


def _dot(a, b):
    return jnp.dot(a, b, preferred_element_type=jnp.float32)


def _dot_nt(a, b):
    return lax.dot_general(a, b, (((1,), (1,)), ((), ())), preferred_element_type=jnp.float32)


def _mod_kernel(c_ref, w_ref, b_ref, o_ref):
    cond = _bf16(_silu(c_ref[...]))
    o_ref[0] = _dot(cond, _bf16(w_ref[0])) + b_ref[0]


def modulation(cc, w_mod, b_mod):
    depth, d, nm = w_mod.shape
    r = cc.shape[0]
    tn = d
    return pl.pallas_call(
        _mod_kernel,
        grid=(depth, nm // tn),
        in_specs=[pl.BlockSpec((r, d), lambda l, j: (0, 0)),
                  pl.BlockSpec((1, d, tn), lambda l, j: (l, 0, j)),
                  pl.BlockSpec((1, 1, tn), lambda l, j: (l, 0, j))],
        out_specs=pl.BlockSpec((1, r, tn), lambda l, j: (l, 0, j)),
        out_shape=jax.ShapeDtypeStruct((depth, r, nm), jnp.float32),
        compiler_params=_params("parallel", "parallel"),
        name="modulation",
    )(cc, w_mod, b_mod.reshape(depth, 1, nm))


def _modnorm(x, w, shift, scale):
    ms = jnp.mean(x * x, axis=-1, keepdims=True)
    return (x * lax.rsqrt(ms + EPS) * w) * (1.0 + scale) + shift


def _group_rms(acc, gmat, w_tiled):
    ms = _dot(_bf16(acc * acc), gmat)
    return acc * lax.rsqrt(ms + EPS) * w_tiled


def _inproj_kernel(x_ref, shift_ref, scale_ref, nw_ref, w_ref, g64_ref, qn_ref, kn_ref, o_ref, h_ref, *, g0):
    j = pl.program_id(2) - g0

    @pl.when(pl.program_id(2) == 0)
    def _():
        h_ref[...] = _bf16(_modnorm(x_ref[0], nw_ref[...], shift_ref[0, 0], scale_ref[0, 0]))

    acc = _dot(h_ref[...], w_ref[...])
    plain = (j != T_HG_Q) & (j != T_NA_Q) & (j != T_NA_K)

    @pl.when(plain)
    def _():
        o_ref[0] = acc.astype(o_ref.dtype)

    @pl.when(j == T_HG_Q)
    def _():
        o_ref[0] = (_silu(acc) * (HG_D ** -0.5)).astype(o_ref.dtype)

    @pl.when(j == T_NA_Q)
    def _():
        o_ref[0] = (_group_rms(acc, g64_ref[...], qn_ref[...]) * (NA_HD ** -0.5)).astype(o_ref.dtype)

    @pl.when(j == T_NA_K)
    def _():
        o_ref[0] = _group_rms(acc, g64_ref[...], kn_ref[...]).astype(o_ref.dtype)


def _block_diag_mean(n, group):
    idx = jnp.arange(n) // group
    return _bf16(jnp.where(idx[:, None] == idx[None, :], 1.0 / group, 0.0))


def input_projection(x, mod, norm_w, w_in, qn, kn, tm):
    bx, s, d = x.shape
    n = w_in.shape[1]
    tn = GROUP_W
    g64 = _block_diag_mean(tn, NA_HD)
    tile = lambda v: jnp.tile(v.reshape(1, -1), (1, tn // v.shape[-1]))
    return pl.pallas_call(
        functools.partial(_inproj_kernel, g0=_gate_tiles(d)),
        grid=(bx, s // tm, n // tn),
        in_specs=[pl.BlockSpec((1, tm, d), lambda b, i, j: (b, i, 0)),
                  pl.BlockSpec((1, 1, 1, d), lambda b, i, j: (b, 0, 0, 0)),
                  pl.BlockSpec((1, 1, 1, d), lambda b, i, j: (b, 1, 0, 0)),
                  pl.BlockSpec((1, d), lambda b, i, j: (0, 0)),
                  pl.BlockSpec((d, tn), lambda b, i, j: (0, j)),
                  pl.BlockSpec((tn, tn), lambda b, i, j: (0, 0)),
                  pl.BlockSpec((1, tn), lambda b, i, j: (0, 0)),
                  pl.BlockSpec((1, tn), lambda b, i, j: (0, 0))],
        out_specs=pl.BlockSpec((1, tm, tn), lambda b, i, j: (b, i, j)),
        out_shape=jax.ShapeDtypeStruct((bx, s, n), jnp.bfloat16),
        scratch_shapes=[pltpu.VMEM((tm, d), jnp.bfloat16)],
        compiler_params=_params("parallel", "parallel", "arbitrary"),
        name="input_projection",
    )(x, mod, mod, norm_w.reshape(1, d), w_in, g64, tile(qn), tile(kn))


def _split3(v):
    h1 = _bf16(v)
    r1 = v - h1.astype(jnp.float32)
    h2 = _bf16(r1)
    h3 = _bf16(r1 - h2.astype(jnp.float32))
    return h1, h2, h3


def _hgrn_chunk(q, v, f, lb, st, tri, ones, rev):
    c = HG_CHUNK
    sig = jax.nn.sigmoid(f)
    lg = jnp.log(lb + (1.0 - lb) * sig)
    k = (1.0 - lb) * (1.0 - sig)
    h1, h2, h3 = _split3(lg)
    bc = _dot(tri, h1) + _dot(tri, h2) + _dot(tri, h3)
    o_inter = _dot_nt(_bf16(q * jnp.exp(bc)), _bf16(st))
    b_end = bc[0:1] if rev else bc[c - 1:c]
    kd = k * jnp.exp(b_end - bc)
    v_t = _bf16(v.astype(jnp.float32).T)
    st_new = st * jnp.exp(b_end) + _dot(v_t, _bf16(kd))

    row = lax.broadcasted_iota(jnp.int32, (HG_SUB, HG_D), 0)
    lane = lax.broadcasted_iota(jnp.int32, (HG_SUB, HG_SUB), 1)
    outs = []
    for i in range(c // HG_SUB):
        lo, hi = i * HG_SUB, (i + 1) * HG_SUB
        qi, bi, ki, vi = q[lo:hi], bc[lo:hi], k[lo:hi], v[lo:hi]
        oi = o_inter[lo:hi]
        prev = None
        if not rev and i > 0:
            ref, prev = bc[lo - 1:lo], slice(0, lo)
        if rev and hi < c:
            ref, prev = bc[hi:hi + 1], slice(hi, c)
        if prev is not None:
            qt = qi * jnp.exp(bi - ref)
            kt = k[prev] * jnp.exp(ref - bc[prev])
            oi = oi + _dot(_bf16(_dot_nt(_bf16(qt), _bf16(kt))), v[prev])
        ps = []
        for s in range(HG_SUB):
            valid = (row <= s) if rev else (row >= s)
            diff = jnp.where(valid, bi - bi[s:s + 1], MASK_VALUE)
            ps.append(qi * ki[s:s + 1] * jnp.exp(diff))
        rsum = _dot(_bf16(jnp.concatenate(ps, axis=0)), ones)
        a_diag = jnp.zeros((HG_SUB, HG_SUB), jnp.float32)
        for s in range(HG_SUB):
            a_diag = jnp.where(lane == s, rsum[s * HG_SUB:(s + 1) * HG_SUB, :HG_SUB], a_diag)
        oi = oi + _dot(_bf16(a_diag), vi)
        outs.append(oi)
    return jnp.concatenate(outs, axis=0), st_new


def _hgrn_kernel(q_ref, v_ref, f_ref, lb_ref, s0_ref, tri_ref, ones_ref, o_ref, sfin_ref, st_ref, *, rev, n_chunks):
    cb = pl.program_id(2)

    @pl.when(cb == 0)
    def _():
        st_ref[...] = s0_ref[0, 0]

    lb = lb_ref[...]
    tri = tri_ref[...]
    ones = ones_ref[...]

    def chunk(it, carry):
        ci = (n_chunks - 1 - it) if rev else it
        r0 = pl.multiple_of(ci * HG_CHUNK, HG_CHUNK)
        q = q_ref[0, pl.ds(r0, HG_CHUNK), :].astype(jnp.float32)
        v = v_ref[0, pl.ds(r0, HG_CHUNK), :]
        f = f_ref[0, pl.ds(r0, HG_CHUNK), :].astype(jnp.float32)
        o, st_new = _hgrn_chunk(q, v, f, lb, st_ref[...], tri, ones, rev)
        st_ref[...] = st_new
        o_ref[0, pl.ds(r0, HG_CHUNK), :] = o
        return carry

    lax.fori_loop(0, n_chunks, chunk, 0)

    @pl.when(cb == pl.num_programs(2) - 1)
    def _():
        sfin_ref[0, 0] = st_ref[...]


def hgrn_scan(z, g0, lb_d, s0, rev, tc):
    b, l, _ = z.shape
    nb = l // tc
    lanes = GROUP_W // HG_D
    blk = (lambda c: nb - 1 - c) if rev else (lambda c: c)
    t_f = T_HG_FB if rev else T_HG_FF
    ci = jnp.arange(HG_CHUNK)
    tri = _bf16((ci[None, :] >= ci[:, None]) if rev else (ci[None, :] <= ci[:, None]))
    ones = jnp.ones((HG_D, HG_D), jnp.bfloat16)
    zspec = lambda t: pl.BlockSpec((1, tc, HG_D), lambda bi, h, c: (bi, blk(c), (g0 + t) * lanes + h))
    st_spec = pl.BlockSpec((1, 1, HG_D, HG_D), lambda bi, h, c: (bi, h, 0, 0))
    return pl.pallas_call(
        functools.partial(_hgrn_kernel, rev=rev, n_chunks=tc // HG_CHUNK),
        grid=(b, HG_HEADS, nb),
        in_specs=[zspec(T_HG_Q), zspec(T_HG_I), zspec(t_f),
                  pl.BlockSpec((1, HG_D), lambda bi, h, c: (0, h)),
                  st_spec,
                  pl.BlockSpec((HG_CHUNK, HG_CHUNK), lambda bi, h, c: (0, 0)),
                  pl.BlockSpec((HG_D, HG_D), lambda bi, h, c: (0, 0))],
        out_specs=[pl.BlockSpec((1, tc, HG_D), lambda bi, h, c: (bi, blk(c), h)), st_spec],
        out_shape=[jax.ShapeDtypeStruct((b, l, GROUP_W), jnp.float32),
                   jax.ShapeDtypeStruct(s0.shape, jnp.float32)],
        scratch_shapes=[pltpu.VMEM((HG_D, HG_D), jnp.float32)],
        compiler_params=_params("parallel", "parallel", "arbitrary"),
        name="hgrn_scan_bw" if rev else "hgrn_scan_fw",
    )(z, z, z, lb_d, s0, tri, ones)


def _softmax_pv(scores, values):
    m = functools.reduce(jnp.maximum, [jnp.max(s, axis=-1, keepdims=True) for s in scores])
    ps = [jnp.exp(s - m) for s in scores]
    den = functools.reduce(jnp.add, [jnp.sum(p, axis=-1, keepdims=True) for p in ps])
    num = functools.reduce(jnp.add, [_dot(_bf16(p), v) for p, v in zip(ps, values)])
    return num / den


def _natten_kernel(q_ref, k_ref, v_ref, qc_ref, kc_ref, vc_ref, bias_ref, o_ref, oc_ref, *, rows):
    lane = lax.broadcasted_iota(jnp.int32, (1, 2 * NA_HD), 1)
    halves = (lane < NA_HD, lane >= NA_HD)
    kc = kc_ref[0]
    vc = vc_ref[0]
    qc = qc_ref[0]
    zero = jnp.zeros((), q_ref.dtype)

    oc = jnp.zeros(qc.shape, jnp.float32)
    for half in halves:
        o_h = _softmax_pv([_dot_nt(jnp.where(half, qc, zero), kc)], [vc])
        oc = jnp.where(half, o_h, oc)
    oc_ref[0] = oc.astype(oc_ref.dtype)

    def row(r, carry):
        r0 = jnp.clip(r - WIN_R // 2, 0, rows - WIN_R)
        delta = r0 - r + WIN_R - 1
        qs = pl.multiple_of(r * GRID_W, GRID_W)
        ks = pl.multiple_of(r0 * GRID_W, GRID_W)
        qr = q_ref[0, pl.ds(qs, GRID_W), :]
        kl = k_ref[0, pl.ds(ks, WIN_R * GRID_W), :]
        vl = v_ref[0, pl.ds(ks, WIN_R * GRID_W), :]
        out = jnp.zeros((GRID_W, 2 * NA_HD), jnp.float32)
        for hh, half in enumerate(halves):
            qm = jnp.where(half, qr, zero)
            s_loc = _dot_nt(qm, kl) + bias_ref[0, hh, delta]
            s_ctx = _dot_nt(qm, kc)
            out = jnp.where(half, _softmax_pv([s_loc, s_ctx], [vl, vc]), out)
        o_ref[0, pl.ds(qs, GRID_W), :] = out.astype(o_ref.dtype)
        return carry

    lax.fori_loop(0, rows, row, 0)


def _natten_bias(rpb):
    qcol = jnp.arange(GRID_W)[:, None]
    kcol = jnp.arange(GRID_W)[None, :]
    wstart = jnp.clip(qcol - WIN_C // 2, 0, GRID_W - WIN_C)
    in_win = (kcol >= wstart) & (kcol < wstart + WIN_C)
    dc = jnp.clip(kcol - qcol + WIN_C - 1, 0, 2 * WIN_C - 2)
    dr = jnp.arange(WIN_R)[:, None] + jnp.arange(WIN_R)[None, :]
    t = rpb[:, :, dc][:, dr]
    t = jnp.where(in_win[None, None, None], t.astype(jnp.float32), MASK_VALUE)
    t = t.transpose(0, 1, 3, 2, 4).reshape(NA_HEADS // 2, 2, WIN_R, GRID_W, WIN_R * GRID_W)
    return t


def natten(z, zc, g0, bias):
    b, s, _ = z.shape
    l = zc.shape[1]
    rows = s // GRID_W
    assert rows >= WIN_R
    pairs = NA_HEADS // 2
    per_tile = GROUP_W // (2 * NA_HD)
    spec = lambda n, t: pl.BlockSpec((1, n, 2 * NA_HD), lambda bi, p: (bi, 0, (g0 + t) * per_tile + p))
    return pl.pallas_call(
        functools.partial(_natten_kernel, rows=rows),
        grid=(b, pairs),
        in_specs=[spec(s, T_NA_Q), spec(s, T_NA_K), spec(s, T_NA_V),
                  spec(l, T_NA_Q), spec(l, T_NA_K), spec(l, T_NA_V),
                  pl.BlockSpec((1, 2, WIN_R, GRID_W, WIN_R * GRID_W), lambda bi, p: (p, 0, 0, 0, 0))],
        out_specs=[pl.BlockSpec((1, s, 2 * NA_HD), lambda bi, p: (bi, 0, p)),
                   pl.BlockSpec((1, l, 2 * NA_HD), lambda bi, p: (bi, 0, p))],
        out_shape=[jax.ShapeDtypeStruct((b, s, GROUP_W), jnp.bfloat16),
                   jax.ShapeDtypeStruct((b, l, GROUP_W), jnp.bfloat16)],
        compiler_params=_params("parallel", "parallel"),
        name="natten",
    )(z, z, z, zc, zc, zc, bias)


HALO = 16


def _pack_halves(h):
    half = h.shape[1] // 2
    lo = pltpu.bitcast(_bf16(h[:, :half]).astype(jnp.float32), jnp.uint32)
    hi = pltpu.bitcast(_bf16(h[:, half:]).astype(jnp.float32), jnp.uint32)
    return (lo >> 16) | (hi & jnp.uint32(0xFFFF0000))


def _unpack_halves(w):
    lo = pltpu.bitcast(w << 16, jnp.float32)
    hi = pltpu.bitcast(w & jnp.uint32(0xFFFF0000), jnp.float32)
    return _bf16(jnp.concatenate([lo, hi], axis=1))


def _merge_kernel(zb_ref, zc_ref, zu_ref, pc_ref, pu_ref, nc_ref, nu_ref, cw_ref,
                  ofw_ref, obw_ref, zg_ref, hgn_ref, g128_ref, yna_ref, ga_ref, gb_ref, gc_ref,
                  x_ref, gate_ref, wa_ref, wb_ref, wc_ref, wo_ref,
                  n2_ref, shift_ref, scale_ref, wr_ref,
                  xo_ref, hp_ref, lg_ref, vs_ref):
    i = pl.program_id(1)
    tm = zb_ref.shape[1]
    f32 = jnp.float32

    v = zc_ref[0].astype(f32) * zu_ref[0].astype(f32)
    vp = pc_ref[0, HALO - 1:HALO].astype(f32) * pu_ref[0, HALO - 1:HALO].astype(f32)
    vn = nc_ref[0, 0:1].astype(f32) * nu_ref[0, 0:1].astype(f32)
    vs_ref[7:8, :] = jnp.where(i == 0, 0.0, vp)
    vs_ref[8:8 + tm, :] = v
    vs_ref[8 + tm:9 + tm, :] = jnp.where(i == pl.num_programs(1) - 1, 0.0, vn)
    cw = cw_ref[...]
    y_cv = zb_ref[0].astype(f32) * (cw[0:1] * vs_ref[7:7 + tm, :] + cw[1:2] * v + cw[2:3] * vs_ref[9:9 + tm, :])

    o = ofw_ref[0] + obw_ref[0]
    y_hg = _group_rms(o, g128_ref[...], hgn_ref[...]) * _silu(zg_ref[0].astype(f32))

    m = (jax.nn.sigmoid(ga_ref[0].astype(f32)) * _dot(_bf16(y_cv), wa_ref[...])
         + jax.nn.sigmoid(gb_ref[0].astype(f32)) * _dot(_bf16(y_hg), wb_ref[...])
         + jax.nn.sigmoid(gc_ref[0].astype(f32)) * _dot(yna_ref[0], wc_ref[...]))
    x_new = x_ref[0] + gate_ref[0, 0] * _dot(_bf16(m), wo_ref[...])
    xo_ref[0] = x_new

    h2 = _modnorm(x_new, n2_ref[...], shift_ref[0, 0], scale_ref[0, 0])
    hp_ref[0] = _pack_halves(h2)
    lg_ref[0] = _dot_nt(wr_ref[...], _bf16(h2))


def merge(z, o_fw, o_bw, y_na, x, mod, conv_w, hg_norm, w_a, w_b, w_c, w_o, norm2, w_rt, tm):
    b, l, d = x.shape
    e = w_rt.shape[0]
    nt = l // tm
    per = tm // HALO
    n_halo = l // HALO
    g0 = _gate_tiles(d)
    bm = (lambda bi: bi) if mod.shape[0] == b else (lambda bi: 0)
    zt = lambda t: pl.BlockSpec((1, tm, GROUP_W), lambda bi, i: (bi, i, g0 + t))
    zprev = lambda t: pl.BlockSpec((1, HALO, GROUP_W), lambda bi, i: (bi, jnp.maximum(i * per - 1, 0), g0 + t))
    znext = lambda t: pl.BlockSpec((1, HALO, GROUP_W),
                                   lambda bi, i: (bi, jnp.minimum((i + 1) * per, n_halo - 1), g0 + t))
    act = lambda w: pl.BlockSpec((1, tm, w), lambda bi, i: (bi, i, 0))
    zgate = lambda k: pl.BlockSpec((1, tm, d), lambda bi, i: (bi, i, k))
    modrow = lambda k: pl.BlockSpec((1, 1, 1, d), lambda bi, i: (bm(bi), k, 0, 0))
    full = lambda a: pl.BlockSpec(a.shape, lambda bi, i: (0,) * a.ndim)
    g128 = _block_diag_mean(GROUP_W, HG_D)
    hgn = jnp.tile(hg_norm.reshape(1, -1), (1, GROUP_W // HG_D))
    n2 = norm2.reshape(1, d)
    return pl.pallas_call(
        _merge_kernel,
        grid=(b, nt),
        in_specs=[zt(T_CONV_B), zt(T_CONV_C), zt(T_CONV_U),
                  zprev(T_CONV_C), zprev(T_CONV_U), znext(T_CONV_C), znext(T_CONV_U), full(conv_w),
                  act(GROUP_W), act(GROUP_W), zt(T_HG_G), full(hgn), full(g128), act(GROUP_W),
                  zgate(0), zgate(1), zgate(2),
                  act(d), modrow(2), full(w_a), full(w_b), full(w_c), full(w_o),
                  full(n2), modrow(3), modrow(4), full(w_rt)],
        out_specs=[act(d), act(d // 2), pl.BlockSpec((1, e, tm), lambda bi, i: (bi, 0, i))],
        out_shape=[jax.ShapeDtypeStruct((b, l, d), jnp.float32),
                   jax.ShapeDtypeStruct((b, l, d // 2), jnp.uint32),
                   jax.ShapeDtypeStruct((b, e, l), jnp.float32)],
        scratch_shapes=[pltpu.VMEM((tm + 16, GROUP_W), jnp.float32)],
        compiler_params=_params("parallel", "arbitrary"),
        name="merge",
    )(z, z, z, z, z, z, z, conv_w, o_fw, o_bw, z, hgn, g128, y_na, z, z, z,
      x, mod, w_a, w_b, w_c, w_o, n2, mod, mod, w_rt)


LANES = 128
ROUTE_K_CHUNK = 1024


def _count(mask):
    return jnp.sum(jnp.where(mask, 1.0, 0.0), axis=1, keepdims=True)


def _route_kernel(lg_ref, tmat_ref, excl_ref, slot_ref, aff_ref, idx_ref, *, cap):
    f32 = jnp.float32
    lg = lg_ref[0]
    e, s = lg.shape
    ex = jnp.exp(lg - jnp.max(lg, axis=0, keepdims=True))
    aff = ex / jnp.sum(ex, axis=0, keepdims=True)
    aff_ref[0] = aff
    bits = pltpu.bitcast(aff, jnp.int32)

    def thr_bit(it, thr):
        cand = thr | (jnp.int32(1) << (30 - it))
        return jnp.where(_count(bits >= cand) >= cap, cand, thr)

    thr = lax.fori_loop(0, 31, thr_bit, jnp.zeros((e, 1), jnp.int32))
    gt = bits > thr
    eq = bits == thr
    need = cap - _count(gt)
    tok = lax.broadcasted_iota(jnp.int32, (e, s), 1)
    nbits = s.bit_length()

    def end_bit(it, end):
        cand = end + (jnp.int32(1) << (nbits - 1 - it))
        ok = (cand <= s) & (_count(eq & (tok < cand)) <= need)
        return jnp.where(ok, cand, end)

    end = lax.fori_loop(0, nbits, end_bit, jnp.zeros((e, 1), jnp.int32))
    sel = gt | (eq & (tok < end))

    self = jnp.where(sel, 1.0, 0.0)
    offs = jnp.zeros((e, 1), f32)
    pieces = []
    for c in range(s // LANES):
        blk = self[:, c * LANES:(c + 1) * LANES]
        pieces.append(_dot(_bf16(blk), excl_ref[...]) + offs)
        offs = offs + jnp.sum(blk, axis=1, keepdims=True)
    slot_ref[0] = jnp.where(sel, jnp.concatenate(pieces, axis=1), -1.0)

    piota = lax.broadcasted_iota(jnp.int32, (cap, 1), 0).astype(f32)
    lane = lax.broadcasted_iota(jnp.int32, (cap, LANES), 1)
    kc = min(ROUTE_K_CHUNK, s)
    idx_ref[0] = jnp.zeros((cap, LANES), jnp.int32)

    def expert(ee, carry):
        res = jnp.zeros((cap, LANES), f32)
        for c in range(s // kc):
            row = slot_ref[0, pl.ds(ee, 1), c * kc:(c + 1) * kc]
            onehot = _bf16(jnp.where(row == piota, 1.0, 0.0))
            res = res + _dot(onehot, tmat_ref[c * kc:(c + 1) * kc, :])
        tok_idx = (res[:, 0:1] * 64.0 + res[:, 1:2]).astype(jnp.int32)
        idx_ref[0] = jnp.where(lane == ee, tok_idx, idx_ref[0])
        return carry

    lax.fori_loop(0, e, expert, 0)


def route(lg, cap):
    b, e, s = lg.shape
    t = jnp.arange(s)
    tmat = jnp.zeros((s, LANES), jnp.bfloat16).at[:, 0].set(_bf16(t // 64)).at[:, 1].set(_bf16(t % 64))
    a = jnp.arange(LANES)
    excl = _bf16(a[:, None] < a[None, :])
    spec = pl.BlockSpec((1, e, s), lambda bi: (bi, 0, 0))
    slot, aff, idx = pl.pallas_call(
        functools.partial(_route_kernel, cap=cap),
        grid=(b,),
        in_specs=[spec, pl.BlockSpec((s, LANES), lambda bi: (0, 0)), pl.BlockSpec((LANES, LANES), lambda bi: (0, 0))],
        out_specs=[spec, spec, pl.BlockSpec((1, cap, LANES), lambda bi: (bi, 0, 0))],
        out_shape=[jax.ShapeDtypeStruct((b, e, s), jnp.float32), jax.ShapeDtypeStruct((b, e, s), jnp.float32),
                   jax.ShapeDtypeStruct((b, cap, LANES), jnp.int32)],
        compiler_params=_params("parallel"),
        name="route",
    )(lg, tmat, excl)
    return slot, aff, idx[:, :, :e].transpose(0, 2, 1)


def _gather_kernel(idx_ref, h_ref, o_ref, *, cap):
    def body(j, carry):
        t = idx_ref[0, 0, j]
        o_ref[0, 0, pl.ds(j, 1), :] = h_ref[0, pl.ds(t, 1), :]
        return carry

    lax.fori_loop(0, cap, body, 0, unroll=8)


def gather(idx, hp, out_shape, out_index):
    b, e, cap = idx.shape
    _, s, w = hp.shape
    return pl.pallas_call(
        functools.partial(_gather_kernel, cap=cap),
        grid=(b, e),
        in_specs=[pl.BlockSpec((1, 1, cap), lambda bi, ei: (bi * e + ei, 0, 0), memory_space=pltpu.SMEM),
                  pl.BlockSpec((1, s, w), lambda bi, ei: (bi, 0, 0))],
        out_specs=pl.BlockSpec((1, 1, cap, w), out_index),
        out_shape=jax.ShapeDtypeStruct(out_shape, jnp.uint32),
        compiler_params=_params("parallel", "arbitrary"),
        name="gather",
    )(idx.reshape(b * e, 1, cap), hp)


FFN_CHUNK = 512


def _ffn_body(xw, wg_ref, wu_ref, wd_ref):
    x = _unpack_halves(xw)
    f = wg_ref.shape[2]
    fc = min(FFN_CHUNK, f)
    acc = jnp.zeros((x.shape[0], wd_ref.shape[2]), jnp.float32)
    for c in range(f // fc):
        a = _dot(x, wg_ref[0, :, c * fc:(c + 1) * fc])
        u = _dot(x, wu_ref[0, :, c * fc:(c + 1) * fc])
        acc = acc + _dot(_bf16(_silu(a) * u), wd_ref[0, c * fc:(c + 1) * fc, :])
    return _bf16(acc)


def _ffn_kernel(xl_ref, wg_ref, wu_ref, wd_ref, yl_ref):
    yl_ref[0, 0] = _ffn_body(xl_ref[0, 0], wg_ref, wu_ref, wd_ref)


def _ffn_ctx_kernel(xl_ref, xc_ref, wg_ref, wu_ref, wd_ref, yl_ref, yc_ref, *, nb):
    is_ctx = pl.program_id(1) == nb
    y = _ffn_body(jnp.where(is_ctx, xc_ref[0, 0], xl_ref[0, 0]), wg_ref, wu_ref, wd_ref)

    @pl.when(jnp.logical_not(is_ctx))
    def _():
        yl_ref[0, 0] = y

    @pl.when(is_ctx)
    def _():
        yc_ref[0, 0] = y


def expert_ffn(xe, xe_c, w_g, w_u, w_d):
    nb, e, cap, w = xe.shape
    d, f = w_g.shape[1:]
    wspec = lambda shape: pl.BlockSpec((1,) + shape, lambda ei, bi: (ei, 0, 0))
    weights = [wspec((d, f)), wspec((d, f)), wspec((f, d))]
    lat = lambda width: pl.BlockSpec((1, 1, cap, width), lambda ei, bi: (jnp.minimum(bi, nb - 1), ei, 0, 0))
    cspec = lambda width: pl.BlockSpec((1, 1, cap, width), lambda ei, bi: (0, ei, 0, 0))
    if xe_c is None:
        return pl.pallas_call(
            _ffn_kernel, grid=(e, nb),
            in_specs=[lat(w)] + weights, out_specs=lat(d),
            out_shape=jax.ShapeDtypeStruct((nb, e, cap, d), jnp.bfloat16),
            compiler_params=_params("parallel", "arbitrary"), name="expert_ffn",
        )(xe, w_g, w_u, w_d), None
    assert xe_c.shape == (1, e, cap, w)
    return pl.pallas_call(
        functools.partial(_ffn_ctx_kernel, nb=nb), grid=(e, nb + 1),
        in_specs=[lat(w), cspec(w)] + weights, out_specs=[lat(d), cspec(d)],
        out_shape=[jax.ShapeDtypeStruct((nb, e, cap, d), jnp.bfloat16),
                   jax.ShapeDtypeStruct((1, e, cap, d), jnp.bfloat16)],
        compiler_params=_params("parallel", "arbitrary"), name="expert_ffn_ctx",
    )(xe, xe_c, w_g, w_u, w_d)


def _combine_kernel(slot_ref, aff_ref, ye_ref, x_ref, gate_ref, o_ref, acc_ref):
    ei = pl.program_id(2)

    @pl.when(ei == 0)
    def _():
        acc_ref[...] = jnp.zeros_like(acc_ref)

    slot_t = slot_ref[0]
    pick = lax.broadcasted_iota(jnp.int32, slot_t.shape, 1) == ei
    slot_col = jnp.sum(jnp.where(pick, slot_t, 0.0), axis=1, keepdims=True)
    aff_col = jnp.sum(jnp.where(pick, aff_ref[0], 0.0), axis=1, keepdims=True)
    cap = ye_ref.shape[2]
    piota = lax.broadcasted_iota(jnp.int32, (1, cap), 1).astype(jnp.float32)
    onehot = _bf16(jnp.where(slot_col == piota, 1.0, 0.0))
    acc_ref[...] += aff_col * _dot(onehot, ye_ref[0, 0])

    @pl.when(ei == pl.num_programs(2) - 1)
    def _():
        o_ref[0] = x_ref[0] + gate_ref[0, 0] * acc_ref[...]


def combine(slot_t, aff_t, ye, ye_index, cap, x, mod, tm):
    b, l, d = x.shape
    e = slot_t.shape[2]
    bm = (lambda bi: bi) if mod.shape[0] == b else (lambda bi: 0)
    tok = lambda w: pl.BlockSpec((1, tm, w), lambda bi, i, ei: (bi, i, 0))
    return pl.pallas_call(
        _combine_kernel,
        grid=(b, l // tm, e),
        in_specs=[tok(e), tok(e),
                  pl.BlockSpec((1, 1, cap, d), lambda bi, i, ei: ye_index(bi, ei)),
                  tok(d),
                  pl.BlockSpec((1, 1, 1, d), lambda bi, i, ei: (bm(bi), N_MOD - 1, 0, 0))],
        out_specs=tok(d),
        out_shape=jax.ShapeDtypeStruct((b, l, d), jnp.float32),
        scratch_shapes=[pltpu.VMEM((tm, d), jnp.float32)],
        compiler_params=_params("parallel", "parallel", "arbitrary"),
        name="combine",
    )(slot_t, aff_t, ye, x, mod)


TM_IN = 1024
TM_MERGE = 256
TM_COMBINE = 512
HG_ROWS = 512


def _moe(x_mid, hp, lg, mod, xe_shape, xe_index, cap):
    slot, aff, idx = route(lg, cap)
    xe = gather(idx, hp, xe_shape, xe_index)
    return slot.transpose(0, 2, 1), aff.transpose(0, 2, 1), xe


def kernel(x, c, ctx, c_ctx, w_mod, b_mod, norm1, w_in, conv_w, hg_lb_logits, hg_norm, na_q_norm, na_k_norm, na_rpb,
           w_br_a, w_br_b, w_br_c, w_out, norm2, w_router, w_e_gate, w_e_up, w_e_down):
    b, s, d = x.shape
    l = ctx.shape[1]
    depth = w_mod.shape[0]
    e = w_router.shape[-1]
    cap = CAP_FACTOR * s // e
    cap_c = CAP_FACTOR * l // e
    assert b * cap_c == cap, "context rows of all samples fill one expert tile"
    g0 = _gate_tiles(d)
    n_mix = 11 * GROUP_W

    lb_sm = jax.nn.softmax(hg_lb_logits.astype(jnp.float32), axis=0)
    lb_all = jnp.cumsum(lb_sm, axis=0) - lb_sm[0]
    rows = -(-(b + 1) // 8) * 8
    cc = jnp.zeros((rows, d), jnp.float32).at[:b].set(c).at[b].set(c_ctx)
    mod_all = modulation(cc, w_mod, b_mod).reshape(depth, rows, N_MOD, 1, d)

    w_in_r = _bf16(jnp.concatenate([w_in[..., n_mix:], w_in[..., :n_mix]], axis=-1))
    w_a, w_b, w_c, w_o = _bf16(w_br_a), _bf16(w_br_b), _bf16(w_br_c), _bf16(w_out)
    w_rt = _bf16(jnp.swapaxes(w_router, 1, 2))
    w_g, w_u, w_d = _bf16(w_e_gate), _bf16(w_e_up), _bf16(w_e_down)
    bias = jax.vmap(_natten_bias)(na_rpb)
    s0 = jnp.zeros((b, HG_HEADS, HG_D, HG_D), jnp.float32)
    hg_rows = min(HG_ROWS, s)
    tm_in = min(TM_IN, s)
    tm_cmb = min(TM_COMBINE, s)

    xc = ctx
    for li in range(depth):
        last = li == depth - 1
        mod = mod_all[li, :b]
        mod_c = mod_all[li, b:b + 1]
        z = input_projection(x, mod, norm1[li], w_in_r[li], na_q_norm[li], na_k_norm[li], tm_in)
        zc = input_projection(xc.reshape(1, b * l, d), mod_c, norm1[li], w_in_r[li], na_q_norm[li], na_k_norm[li],
                              min(TM_IN, b * l)).reshape(b, l, -1)
        lb_f, lb_b = lb_all[li, 0:1], lb_all[li, 1:2]
        oc_fw, s_fw = hgrn_scan(zc, g0, lb_f, s0, False, l)
        o_fw, _ = hgrn_scan(z, g0, lb_f, s_fw, False, hg_rows)
        oc_bw, s_bw = hgrn_scan(zc, g0, lb_b, s0, True, l)
        o_bw, _ = hgrn_scan(z, g0, lb_b, s_bw, True, hg_rows)
        y_na, yc_na = natten(z, zc, g0, bias[li])

        mw = (conv_w[li], hg_norm[li], w_a[li], w_b[li], w_c[li], w_o[li], norm2[li], w_rt[li])
        x, hp, lg = merge(z, o_fw, o_bw, y_na, x, mod, *mw, TM_MERGE)
        slot_t, aff_t, xe = _moe(x, hp, lg, mod, (b, e, cap, d // 2), lambda bi, ei: (bi, ei, 0, 0), cap)
        if last:
            ye, _ = expert_ffn(xe, None, w_g[li], w_u[li], w_d[li])
        else:
            xc, hpc, lgc = merge(zc, oc_fw, oc_bw, yc_na, xc, mod_c, *mw, l)
            slot_c, aff_c, xe_c = _moe(xc, hpc, lgc, mod_c, (1, e, cap, d // 2), lambda bi, ei: (0, ei, bi, 0), cap_c)
            ye, ye_c = expert_ffn(xe, xe_c, w_g[li], w_u[li], w_d[li])
            xc = combine(slot_c, aff_c, ye_c, lambda bi, ei: (0, ei, bi, 0), cap_c, xc, mod_c, l)
        x = combine(slot_t, aff_t, ye, lambda bi, ei: (bi, ei, 0, 0), cap, x, mod, tm_cmb)
    return x
```

```python
import functools

import jax
import jax.numpy as jnp
from jax import lax
from jax.experimental import pallas as pl
from jax.experimental.pallas import tpu as pltpu

N_MOD = 6
EPS = 1e-6
MASK_VALUE = -1e30
GRID_W = 64
GROUP_W = 512
HG_HEADS = 4
HG_D = 128
HG_CHUNK = 64
HG_SUB = 16
NA_HEADS = 8
NA_HD = 64
WIN_R = 8
WIN_C = 16
CAP_FACTOR = 2
T_CONV_B, T_CONV_C, T_CONV_U, T_HG_Q, T_HG_I, T_HG_FF, T_HG_FB, T_HG_G, T_NA_Q, T_NA_K, T_NA_V = range(11)
N_GATES = 3


def _gate_tiles(d):
    assert (N_GATES * d) % GROUP_W == 0
    return N_GATES * d // GROUP_W

VMEM_LIMIT_BYTES = 48 * 1024 * 1024


def _params(*semantics):
    return pltpu.CompilerParams(dimension_semantics=semantics, vmem_limit_bytes=VMEM_LIMIT_BYTES)


def _silu(v):
    return v * jax.nn.sigmoid(v)


def _bf16(v):
    return v.astype(jnp.bfloat16)


def _dot(a, b):
    return jnp.dot(a, b, preferred_element_type=jnp.float32)


def _dot_nt(a, b):
    return lax.dot_general(a, b, (((1,), (1,)), ((), ())), preferred_element_type=jnp.float32)


def _mod_kernel(c_ref, w_ref, b_ref, o_ref):
    cond = _bf16(_silu(c_ref[...]))
    o_ref[0] = _dot(cond, _bf16(w_ref[0])) + b_ref[0]


def modulation(cc, w_mod, b_mod):
    depth, d, nm = w_mod.shape
    r = cc.shape[0]
    tn = d
    return pl.pallas_call(
        _mod_kernel,
        grid=(depth, nm // tn),
        in_specs=[pl.BlockSpec((r, d), lambda l, j: (0, 0)),
                  pl.BlockSpec((1, d, tn), lambda l, j: (l, 0, j)),
                  pl.BlockSpec((1, 1, tn), lambda l, j: (l, 0, j))],
        out_specs=pl.BlockSpec((1, r, tn), lambda l, j: (l, 0, j)),
        out_shape=jax.ShapeDtypeStruct((depth, r, nm), jnp.float32),
        compiler_params=_params("parallel", "parallel"),
        name="modulation",
    )(cc, w_mod, b_mod.reshape(depth, 1, nm))


def _modnorm(x, w, shift, scale):
    ms = jnp.mean(x * x, axis=-1, keepdims=True)
    return (x * lax.rsqrt(ms + EPS) * w) * (1.0 + scale) + shift


def _group_rms(acc, gmat, w_tiled):
    ms = _dot(_bf16(acc * acc), gmat)
    return acc * lax.rsqrt(ms + EPS) * w_tiled


def _inproj_kernel(x_ref, shift_ref, scale_ref, nw_ref, w_ref, g64_ref, qn_ref, kn_ref, o_ref, h_ref, *, g0):
    j = pl.program_id(2) - g0

    @pl.when(pl.program_id(2) == 0)
    def _():
        h_ref[...] = _bf16(_modnorm(x_ref[0], nw_ref[...], shift_ref[0, 0], scale_ref[0, 0]))

    acc = lambda: _dot(h_ref[...], w_ref[...])
    plain = (j != T_HG_Q) & (j != T_NA_Q) & (j != T_NA_K)

    @pl.when(plain)
    def _():
        o_ref[0] = acc().astype(o_ref.dtype)

    @pl.when(j == T_HG_Q)
    def _():
        o_ref[0] = (_silu(acc()) * (HG_D ** -0.5)).astype(o_ref.dtype)

    @pl.when(j == T_NA_Q)
    def _():
        o_ref[0] = (_group_rms(acc(), g64_ref[...], qn_ref[...]) * (NA_HD ** -0.5)).astype(o_ref.dtype)

    @pl.when(j == T_NA_K)
    def _():
        o_ref[0] = _group_rms(acc(), g64_ref[...], kn_ref[...]).astype(o_ref.dtype)


def _block_diag_mean(n, group):
    idx = jnp.arange(n) // group
    return _bf16(jnp.where(idx[:, None] == idx[None, :], 1.0 / group, 0.0))


def input_projection(x, mod, norm_w, w_in, qn, kn, tm):
    bx, s, d = x.shape
    n = w_in.shape[1]
    tn = GROUP_W
    g64 = _block_diag_mean(tn, NA_HD)
    tile = lambda v: jnp.tile(v.reshape(1, -1), (1, tn // v.shape[-1]))
    return pl.pallas_call(
        functools.partial(_inproj_kernel, g0=_gate_tiles(d)),
        grid=(bx, s // tm, n // tn),
        in_specs=[pl.BlockSpec((1, tm, d), lambda b, i, j: (b, i, 0)),
                  pl.BlockSpec((1, 1, 1, d), lambda b, i, j: (b, 0, 0, 0)),
                  pl.BlockSpec((1, 1, 1, d), lambda b, i, j: (b, 1, 0, 0)),
                  pl.BlockSpec((1, d), lambda b, i, j: (0, 0)),
                  pl.BlockSpec((d, tn), lambda b, i, j: (0, j)),
                  pl.BlockSpec((tn, tn), lambda b, i, j: (0, 0)),
                  pl.BlockSpec((1, tn), lambda b, i, j: (0, 0)),
                  pl.BlockSpec((1, tn), lambda b, i, j: (0, 0))],
        out_specs=pl.BlockSpec((1, tm, tn), lambda b, i, j: (b, i, j)),
        out_shape=jax.ShapeDtypeStruct((bx, s, n), jnp.bfloat16),
        scratch_shapes=[pltpu.VMEM((tm, d), jnp.bfloat16)],
        compiler_params=_params("parallel", "parallel", "arbitrary"),
        name="input_projection",
    )(x, mod, mod, norm_w.reshape(1, d), w_in, g64, tile(qn), tile(kn))


def _split3(v):
    h1 = _bf16(v)
    r1 = v - h1.astype(jnp.float32)
    h2 = _bf16(r1)
    h3 = _bf16(r1 - h2.astype(jnp.float32))
    return h1, h2, h3


def _hgrn_gates(f, lb, tri):
    sig = jax.nn.sigmoid(f)
    lg = jnp.log(lb + (1.0 - lb) * sig)
    k = (1.0 - lb) * (1.0 - sig)
    h1, h2, h3 = _split3(lg)
    return k, _dot(tri, h1) + _dot(tri, h2) + _dot(tri, h3)


HG_GRP = 8


def _hgrn_head_products(q, v, k, bc, st, ones, rev):
    c = HG_CHUNK
    o_inter = _dot_nt(_bf16(q * jnp.exp(bc)), _bf16(st))
    b_end = bc[0:1] if rev else bc[c - 1:c]
    kd = k * jnp.exp(b_end - bc)
    v_t = _bf16(v.astype(jnp.float32).T)
    st_new = st * jnp.exp(b_end) + _dot(v_t, _bf16(kd))

    row = lax.broadcasted_iota(jnp.int32, (HG_GRP, HG_D), 0)
    a_off, ps, where = [], [], []
    for i in range(c // HG_SUB):
        lo, hi = i * HG_SUB, (i + 1) * HG_SUB
        qi, bi = q[lo:hi], bc[lo:hi]
        prev = None
        if not rev and i > 0:
            ref, prev = bc[lo - 1:lo], slice(0, lo)
        if rev and hi < c:
            ref, prev = bc[hi:hi + 1], slice(hi, c)
        if prev is None:
            a_off.append(None)
        else:
            qt = _bf16(qi * jnp.exp(bi - ref))
            kt = _bf16(k[prev] * jnp.exp(ref - bc[prev]))
            pad = jnp.zeros((c - kt.shape[0], HG_D), kt.dtype)
            kt = jnp.concatenate([pad, kt] if rev else [kt, pad], axis=0)
            a_off.append(_dot_nt(qt, kt))
        for s in range(lo, hi):
            for g in range(lo // HG_GRP, hi // HG_GRP):
                g_lo = g * HG_GRP
                sees = (g_lo <= s) if rev else (g_lo + HG_GRP > s)
                if not sees:
                    continue
                diff = bc[g_lo:g_lo + HG_GRP] - bc[s:s + 1]
                if g_lo <= s < g_lo + HG_GRP:
                    valid = (row <= s - g_lo) if rev else (row >= s - g_lo)
                    diff = jnp.where(valid, diff, MASK_VALUE)
                where.append((g, s))
                ps.append(q[g_lo:g_lo + HG_GRP] * k[s:s + 1] * jnp.exp(diff))
    rsum = _dot(_bf16(jnp.concatenate(ps, axis=0)), ones)
    return o_inter, st_new, a_off, rsum, where


def _hgrn_head_output(v, o_inter, a_off, rsum, where):
    c = HG_CHUNK
    lane = lax.broadcasted_iota(jnp.int32, (HG_GRP, c), 1)
    groups = [None] * (c // HG_GRP)
    for i, a in enumerate(a_off):
        for g in range(i * HG_SUB // HG_GRP, (i + 1) * HG_SUB // HG_GRP):
            r = (g - i * HG_SUB // HG_GRP) * HG_GRP
            groups[g] = jnp.zeros((HG_GRP, c), jnp.float32) if a is None else a[r:r + HG_GRP]
    for n, (g, s) in enumerate(where):
        groups[g] = jnp.where(lane == s, rsum[n * HG_GRP:(n + 1) * HG_GRP, :c], groups[g])
    return o_inter + _dot(_bf16(jnp.concatenate(groups, axis=0)), v)


def _hgrn_kernel(q_ref, v_ref, f_ref, lb_ref, s0_ref, tri_ref, ones_ref, o_ref, sfin_ref, st_ref, *, rev, n_chunks):
    cb = pl.program_id(1)

    @pl.when(cb == 0)
    def _():
        st_ref[...] = s0_ref[0]

    lb = lb_ref[...]
    tri = tri_ref[...]
    ones = ones_ref[...]

    def chunk(it, carry):
        ci = (n_chunks - 1 - it) if rev else it
        r0 = pl.multiple_of(ci * HG_CHUNK, HG_CHUNK)
        q = q_ref[0, pl.ds(r0, HG_CHUNK), :].astype(jnp.float32)
        v = v_ref[0, pl.ds(r0, HG_CHUNK), :]
        k, bc = _hgrn_gates(f_ref[0, pl.ds(r0, HG_CHUNK), :].astype(jnp.float32), lb, tri)
        heads = [slice(h * HG_D, (h + 1) * HG_D) for h in range(HG_HEADS)]
        stage = [_hgrn_head_products(q[:, hs], v[:, hs], k[:, hs], bc[:, hs], st_ref[h], ones, rev)
                 for h, hs in enumerate(heads)]
        outs = []
        for h, (hs, (o_inter, st_new, a_off, rsum, where)) in enumerate(zip(heads, stage)):
            st_ref[h] = st_new
            outs.append(_hgrn_head_output(v[:, hs], o_inter, a_off, rsum, where))
        o_ref[0, pl.ds(r0, HG_CHUNK), :] = jnp.concatenate(outs, axis=1)
        return carry

    lax.fori_loop(0, n_chunks, chunk, 0)

    @pl.when(cb == pl.num_programs(1) - 1)
    def _():
        sfin_ref[0] = st_ref[...]


def hgrn_scan(z, g0, lb_d, s0, rev, tc):
    b, l, _ = z.shape
    nb = l // tc
    blk = (lambda c: nb - 1 - c) if rev else (lambda c: c)
    t_f = T_HG_FB if rev else T_HG_FF
    ci = jnp.arange(HG_CHUNK)
    tri = _bf16((ci[None, :] >= ci[:, None]) if rev else (ci[None, :] <= ci[:, None]))
    ones = jnp.ones((HG_D, HG_D), jnp.bfloat16)
    zspec = lambda t: pl.BlockSpec((1, tc, GROUP_W), lambda bi, c: (bi, blk(c), g0 + t))
    st_spec = pl.BlockSpec((1, HG_HEADS, HG_D, HG_D), lambda bi, c: (bi, 0, 0, 0))
    return pl.pallas_call(
        functools.partial(_hgrn_kernel, rev=rev, n_chunks=tc // HG_CHUNK),
        grid=(b, nb),
        in_specs=[zspec(T_HG_Q), zspec(T_HG_I), zspec(t_f),
                  pl.BlockSpec((1, GROUP_W), lambda bi, c: (0, 0)),
                  st_spec,
                  pl.BlockSpec((HG_CHUNK, HG_CHUNK), lambda bi, c: (0, 0)),
                  pl.BlockSpec((HG_D, HG_D), lambda bi, c: (0, 0))],
        out_specs=[pl.BlockSpec((1, tc, GROUP_W), lambda bi, c: (bi, blk(c), 0)), st_spec],
        out_shape=[jax.ShapeDtypeStruct((b, l, GROUP_W), jnp.float32),
                   jax.ShapeDtypeStruct(s0.shape, jnp.float32)],
        scratch_shapes=[pltpu.VMEM((HG_HEADS, HG_D, HG_D), jnp.float32)],
        compiler_params=_params("parallel", "arbitrary"),
        name="hgrn_scan_bw" if rev else "hgrn_scan_fw",
    )(z, z, z, lb_d, s0, tri, ones)


NA_ROWS = 4


def _softmax_pv(scores, values):
    m = functools.reduce(jnp.maximum, [jnp.max(s, axis=-1, keepdims=True) for s in scores])
    ps = [jnp.exp(s - m) for s in scores]
    den = functools.reduce(jnp.add, [jnp.sum(p, axis=-1, keepdims=True) for p in ps])
    num = functools.reduce(jnp.add, [_dot(_bf16(p), v) for p, v in zip(ps, values)])
    return num / den


def _natten_kernel(q_ref, k_ref, v_ref, qc_ref, kc_ref, vc_ref, bias_ref, o_ref, oc_ref, *, rows):
    lane = lax.broadcasted_iota(jnp.int32, (1, 2 * NA_HD), 1)
    halves = (lane < NA_HD, lane >= NA_HD)
    kc = kc_ref[0]
    vc = vc_ref[0]
    qc = qc_ref[0]
    zero = jnp.zeros((), q_ref.dtype)

    oc = jnp.zeros(qc.shape, jnp.float32)
    for half in halves:
        o_h = _softmax_pv([_dot_nt(jnp.where(half, qc, zero), kc)], [vc])
        oc = jnp.where(half, o_h, oc)
    oc_ref[0] = oc.astype(oc_ref.dtype)

    def row_group(it, carry):
        work = []
        for j in range(NA_ROWS):
            r = it * NA_ROWS + j
            r0 = jnp.clip(r - WIN_R // 2, 0, rows - WIN_R)
            delta = r0 - r + WIN_R - 1
            qs = pl.multiple_of(r * GRID_W, GRID_W)
            ks = pl.multiple_of(r0 * GRID_W, GRID_W)
            qr = q_ref[0, pl.ds(qs, GRID_W), :]
            kl = k_ref[0, pl.ds(ks, WIN_R * GRID_W), :]
            vl = v_ref[0, pl.ds(ks, WIN_R * GRID_W), :]
            for hh, half in enumerate(halves):
                qm = jnp.where(half, qr, zero)
                work.append((qs, half, vl, _dot_nt(qm, kl) + bias_ref[0, hh, delta], _dot_nt(qm, kc)))
        outs = {}
        for j, (qs, half, vl, s_loc, s_ctx) in enumerate(work):
            o_h = _softmax_pv([s_loc, s_ctx], [vl, vc])
            outs[j // 2] = o_h if j % 2 == 0 else jnp.where(half, o_h, outs[j // 2])
        for j in range(NA_ROWS):
            o_ref[0, pl.ds(work[2 * j][0], GRID_W), :] = outs[j].astype(o_ref.dtype)
        return carry

    lax.fori_loop(0, rows // NA_ROWS, row_group, 0)


def _natten_bias(rpb):
    qcol = jnp.arange(GRID_W)[:, None]
    kcol = jnp.arange(GRID_W)[None, :]
    wstart = jnp.clip(qcol - WIN_C // 2, 0, GRID_W - WIN_C)
    in_win = (kcol >= wstart) & (kcol < wstart + WIN_C)
    dc = jnp.clip(kcol - qcol + WIN_C - 1, 0, 2 * WIN_C - 2)
    dr = jnp.arange(WIN_R)[:, None] + jnp.arange(WIN_R)[None, :]
    t = rpb[:, :, dc][:, dr]
    t = jnp.where(in_win[None, None, None], t.astype(jnp.float32), MASK_VALUE)
    t = t.transpose(0, 1, 3, 2, 4).reshape(NA_HEADS // 2, 2, WIN_R, GRID_W, WIN_R * GRID_W)
    return t


def natten(z, zc, g0, bias):
    b, s, _ = z.shape
    l = zc.shape[1]
    rows = s // GRID_W
    assert rows >= WIN_R
    pairs = NA_HEADS // 2
    per_tile = GROUP_W // (2 * NA_HD)
    spec = lambda n, t: pl.BlockSpec((1, n, 2 * NA_HD), lambda bi, p: (bi, 0, (g0 + t) * per_tile + p))
    return pl.pallas_call(
        functools.partial(_natten_kernel, rows=rows),
        grid=(b, pairs),
        in_specs=[spec(s, T_NA_Q), spec(s, T_NA_K), spec(s, T_NA_V),
                  spec(l, T_NA_Q), spec(l, T_NA_K), spec(l, T_NA_V),
                  pl.BlockSpec((1, 2, WIN_R, GRID_W, WIN_R * GRID_W), lambda bi, p: (p, 0, 0, 0, 0))],
        out_specs=[pl.BlockSpec((1, s, 2 * NA_HD), lambda bi, p: (bi, 0, p)),
                   pl.BlockSpec((1, l, 2 * NA_HD), lambda bi, p: (bi, 0, p))],
        out_shape=[jax.ShapeDtypeStruct((b, s, GROUP_W), jnp.bfloat16),
                   jax.ShapeDtypeStruct((b, l, GROUP_W), jnp.bfloat16)],
        compiler_params=_params("parallel", "parallel"),
        name="natten",
    )(z, z, z, zc, zc, zc, bias)


HALO = 16


def _pack_halves(h):
    half = h.shape[1] // 2
    lo = pltpu.bitcast(_bf16(h[:, :half]).astype(jnp.float32), jnp.uint32)
    hi = pltpu.bitcast(_bf16(h[:, half:]).astype(jnp.float32), jnp.uint32)
    return (lo >> 16) | (hi & jnp.uint32(0xFFFF0000))


def _unpack_halves(w):
    lo = pltpu.bitcast(w << 16, jnp.float32)
    hi = pltpu.bitcast(w & jnp.uint32(0xFFFF0000), jnp.float32)
    return _bf16(jnp.concatenate([lo, hi], axis=1))


def _merge_kernel(zb_ref, zc_ref, zu_ref, pc_ref, pu_ref, nc_ref, nu_ref, cw_ref,
                  ofw_ref, obw_ref, zg_ref, hgn_ref, g128_ref, yna_ref, ga_ref, gb_ref, gc_ref,
                  x_ref, gate_ref, wa_ref, wb_ref, wc_ref, wo_ref,
                  n2_ref, shift_ref, scale_ref, wr_ref,
                  xo_ref, hp_ref, lg_ref, vs_ref):
    i = pl.program_id(1)
    tm = zb_ref.shape[1]
    f32 = jnp.float32

    v = zc_ref[0].astype(f32) * zu_ref[0].astype(f32)
    vp = pc_ref[0, HALO - 1:HALO].astype(f32) * pu_ref[0, HALO - 1:HALO].astype(f32)
    vn = nc_ref[0, 0:1].astype(f32) * nu_ref[0, 0:1].astype(f32)
    vs_ref[7:8, :] = jnp.where(i == 0, 0.0, vp)
    vs_ref[8:8 + tm, :] = v
    vs_ref[8 + tm:9 + tm, :] = jnp.where(i == pl.num_programs(1) - 1, 0.0, vn)
    cw = cw_ref[...]
    y_cv = zb_ref[0].astype(f32) * (cw[0:1] * vs_ref[7:7 + tm, :] + cw[1:2] * v + cw[2:3] * vs_ref[9:9 + tm, :])

    o = ofw_ref[0] + obw_ref[0]
    y_hg = _group_rms(o, g128_ref[...], hgn_ref[...]) * _silu(zg_ref[0].astype(f32))

    m = (jax.nn.sigmoid(ga_ref[0].astype(f32)) * _dot(_bf16(y_cv), wa_ref[...])
         + jax.nn.sigmoid(gb_ref[0].astype(f32)) * _dot(_bf16(y_hg), wb_ref[...])
         + jax.nn.sigmoid(gc_ref[0].astype(f32)) * _dot(yna_ref[0], wc_ref[...]))
    x_new = x_ref[0] + gate_ref[0, 0] * _dot(_bf16(m), wo_ref[...])
    xo_ref[0] = x_new

    h2 = _modnorm(x_new, n2_ref[...], shift_ref[0, 0], scale_ref[0, 0])
    hp_ref[0] = _pack_halves(h2)
    lg_ref[0] = _dot_nt(wr_ref[...], _bf16(h2))


def merge(z, o_fw, o_bw, y_na, x, mod, conv_w, hg_norm, w_a, w_b, w_c, w_o, norm2, w_rt, tm):
    b, l, d = x.shape
    e = w_rt.shape[0]
    nt = l // tm
    per = tm // HALO
    n_halo = l // HALO
    g0 = _gate_tiles(d)
    bm = (lambda bi: bi) if mod.shape[0] == b else (lambda bi: 0)
    zt = lambda t: pl.BlockSpec((1, tm, GROUP_W), lambda bi, i: (bi, i, g0 + t))
    zprev = lambda t: pl.BlockSpec((1, HALO, GROUP_W), lambda bi, i: (bi, jnp.maximum(i * per - 1, 0), g0 + t))
    znext = lambda t: pl.BlockSpec((1, HALO, GROUP_W),
                                   lambda bi, i: (bi, jnp.minimum((i + 1) * per, n_halo - 1), g0 + t))
    act = lambda w: pl.BlockSpec((1, tm, w), lambda bi, i: (bi, i, 0))
    zgate = lambda k: pl.BlockSpec((1, tm, d), lambda bi, i: (bi, i, k))
    modrow = lambda k: pl.BlockSpec((1, 1, 1, d), lambda bi, i: (bm(bi), k, 0, 0))
    full = lambda a: pl.BlockSpec(a.shape, lambda bi, i: (0,) * a.ndim)
    g128 = _block_diag_mean(GROUP_W, HG_D)
    hgn = jnp.tile(hg_norm.reshape(1, -1), (1, GROUP_W // HG_D))
    n2 = norm2.reshape(1, d)
    return pl.pallas_call(
        _merge_kernel,
        grid=(b, nt),
        in_specs=[zt(T_CONV_B), zt(T_CONV_C), zt(T_CONV_U),
                  zprev(T_CONV_C), zprev(T_CONV_U), znext(T_CONV_C), znext(T_CONV_U), full(conv_w),
                  act(GROUP_W), act(GROUP_W), zt(T_HG_G), full(hgn), full(g128), act(GROUP_W),
                  zgate(0), zgate(1), zgate(2),
                  act(d), modrow(2), full(w_a), full(w_b), full(w_c), full(w_o),
                  full(n2), modrow(3), modrow(4), full(w_rt)],
        out_specs=[act(d), act(d // 2), pl.BlockSpec((1, e, tm), lambda bi, i: (bi, 0, i))],
        out_shape=[jax.ShapeDtypeStruct((b, l, d), jnp.float32),
                   jax.ShapeDtypeStruct((b, l, d // 2), jnp.uint32),
                   jax.ShapeDtypeStruct((b, e, l), jnp.float32)],
        scratch_shapes=[pltpu.VMEM((tm + 16, GROUP_W), jnp.float32)],
        compiler_params=_params("parallel", "arbitrary"),
        name="merge",
    )(z, z, z, z, z, z, z, conv_w, o_fw, o_bw, z, hgn, g128, y_na, z, z, z,
      x, mod, w_a, w_b, w_c, w_o, n2, mod, mod, w_rt)


LANES = 128
ROUTE_K_CHUNK = 1024


def _count(mask):
    return jnp.sum(jnp.where(mask, 1.0, 0.0), axis=1, keepdims=True)


def _route_kernel(lg_ref, tmat_ref, excl_ref, slot_ref, aff_ref, idx_ref, starts_ref, *, cap, tile):
    f32 = jnp.float32
    lg = lg_ref[0]
    e, s = lg.shape
    ex = jnp.exp(lg - jnp.max(lg, axis=0, keepdims=True))
    aff = ex / jnp.sum(ex, axis=0, keepdims=True)
    aff_ref[0] = aff
    bits = pltpu.bitcast(aff, jnp.int32)

    def thr_bit(it, thr):
        cand = thr | (jnp.int32(1) << (30 - it))
        return jnp.where(_count(bits >= cand) >= cap, cand, thr)

    thr = lax.fori_loop(0, 31, thr_bit, jnp.zeros((e, 1), jnp.int32))
    gt = bits > thr
    eq = bits == thr
    need = cap - _count(gt)
    tok = lax.broadcasted_iota(jnp.int32, (e, s), 1)
    nbits = s.bit_length()

    def end_bit(it, end):
        cand = end + (jnp.int32(1) << (nbits - 1 - it))
        ok = (cand <= s) & (_count(eq & (tok < cand)) <= need)
        return jnp.where(ok, cand, end)

    end = lax.fori_loop(0, nbits, end_bit, jnp.zeros((e, 1), jnp.int32))
    sel = gt | (eq & (tok < end))

    self = jnp.where(sel, 1.0, 0.0)
    offs = jnp.zeros((e, 1), f32)
    pieces = []
    tile_lane = lax.broadcasted_iota(jnp.int32, (e, LANES), 1)
    starts = jnp.zeros((e, LANES), f32)
    for c in range(s // LANES):
        if (c * LANES) % tile == 0:
            starts = jnp.where(tile_lane == c * LANES // tile, offs, starts)
        blk = self[:, c * LANES:(c + 1) * LANES]
        pieces.append(_dot(_bf16(blk), excl_ref[...]) + offs)
        offs = offs + jnp.sum(blk, axis=1, keepdims=True)
    starts_ref[0] = jnp.where(tile_lane == s // tile, offs, starts).astype(jnp.int32)
    slot_ref[0] = jnp.where(sel, jnp.concatenate(pieces, axis=1), -1.0)

    piota = lax.broadcasted_iota(jnp.int32, (cap, 1), 0).astype(f32)
    lane = lax.broadcasted_iota(jnp.int32, (cap, LANES), 1)
    kc = min(ROUTE_K_CHUNK, s)
    idx_ref[0] = jnp.zeros((cap, LANES), jnp.int32)

    def expert(ee, carry):
        res = jnp.zeros((cap, LANES), f32)
        for c in range(s // kc):
            row = slot_ref[0, pl.ds(ee, 1), c * kc:(c + 1) * kc]
            onehot = _bf16(jnp.where(row == piota, 1.0, 0.0))
            res = res + _dot(onehot, tmat_ref[c * kc:(c + 1) * kc, :])
        tok_idx = (res[:, 0:1] * 64.0 + res[:, 1:2]).astype(jnp.int32)
        idx_ref[0] = jnp.where(lane == ee, tok_idx, idx_ref[0])
        return carry

    lax.fori_loop(0, e, expert, 0)


def route(lg, cap, tile):
    b, e, s = lg.shape
    assert tile % LANES == 0 and s % tile == 0 and s // tile < LANES
    t = jnp.arange(s)
    tmat = jnp.zeros((s, LANES), jnp.bfloat16).at[:, 0].set(_bf16(t // 64)).at[:, 1].set(_bf16(t % 64))
    a = jnp.arange(LANES)
    excl = _bf16(a[:, None] < a[None, :])
    spec = pl.BlockSpec((1, e, s), lambda bi: (bi, 0, 0))
    slot, aff, idx, starts = pl.pallas_call(
        functools.partial(_route_kernel, cap=cap, tile=tile),
        grid=(b,),
        in_specs=[spec, pl.BlockSpec((s, LANES), lambda bi: (0, 0)), pl.BlockSpec((LANES, LANES), lambda bi: (0, 0))],
        out_specs=[spec, spec, pl.BlockSpec((1, cap, LANES), lambda bi: (bi, 0, 0)),
                   pl.BlockSpec((1, e, LANES), lambda bi: (bi, 0, 0))],
        out_shape=[jax.ShapeDtypeStruct((b, e, s), jnp.float32), jax.ShapeDtypeStruct((b, e, s), jnp.float32),
                   jax.ShapeDtypeStruct((b, cap, LANES), jnp.int32), jax.ShapeDtypeStruct((b, e, LANES), jnp.int32)],
        compiler_params=_params("parallel"),
        name="route",
    )(lg, tmat, excl)
    return slot, aff, idx[:, :, :e].transpose(0, 2, 1), starts[:, :, :s // tile + 1]


def _gather_kernel(idx_ref, h_ref, o_ref, *, cap):
    def body(j, carry):
        t = idx_ref[0, 0, j]
        o_ref[0, 0, pl.ds(j, 1), :] = h_ref[0, pl.ds(t, 1), :]
        return carry

    lax.fori_loop(0, cap, body, 0, unroll=8)


def gather(idx, hp, out_shape, out_index):
    b, e, cap = idx.shape
    _, s, w = hp.shape
    return pl.pallas_call(
        functools.partial(_gather_kernel, cap=cap),
        grid=(b, e),
        in_specs=[pl.BlockSpec((1, 1, cap), lambda bi, ei: (bi * e + ei, 0, 0), memory_space=pltpu.SMEM),
                  pl.BlockSpec((1, s, w), lambda bi, ei: (bi, 0, 0))],
        out_specs=pl.BlockSpec((1, 1, cap, w), out_index),
        out_shape=jax.ShapeDtypeStruct(out_shape, jnp.uint32),
        compiler_params=_params("parallel", "arbitrary"),
        name="gather",
    )(idx.reshape(b * e, 1, cap), hp)


FFN_CHUNK = 512


def _ffn_body(xw, wg_ref, wu_ref, wd_ref):
    x = _unpack_halves(xw)
    f = wg_ref.shape[2]
    fc = min(FFN_CHUNK, f)
    acc = jnp.zeros((x.shape[0], wd_ref.shape[2]), jnp.float32)
    for c in range(f // fc):
        a = _dot(x, wg_ref[0, :, c * fc:(c + 1) * fc])
        u = _dot(x, wu_ref[0, :, c * fc:(c + 1) * fc])
        acc = acc + _dot(_bf16(_silu(a) * u), wd_ref[0, c * fc:(c + 1) * fc, :])
    return _bf16(acc)


def _ffn_kernel(xl_ref, wg_ref, wu_ref, wd_ref, yl_ref):
    yl_ref[0, 0] = _ffn_body(xl_ref[0, 0], wg_ref, wu_ref, wd_ref)


def _ffn_ctx_kernel(xl_ref, xc_ref, wg_ref, wu_ref, wd_ref, yl_ref, yc_ref, *, nb):
    is_ctx = pl.program_id(1) == nb
    y = _ffn_body(jnp.where(is_ctx, xc_ref[0, 0], xl_ref[0, 0]), wg_ref, wu_ref, wd_ref)

    @pl.when(jnp.logical_not(is_ctx))
    def _():
        yl_ref[0, 0] = y

    @pl.when(is_ctx)
    def _():
        yc_ref[0, 0] = y


def expert_ffn(xe, xe_c, w_g, w_u, w_d):
    nb, e, cap, w = xe.shape
    d, f = w_g.shape[1:]
    wspec = lambda shape: pl.BlockSpec((1,) + shape, lambda ei, bi: (ei, 0, 0))
    weights = [wspec((d, f)), wspec((d, f)), wspec((f, d))]
    lat = lambda width: pl.BlockSpec((1, 1, cap, width), lambda ei, bi: (jnp.minimum(bi, nb - 1), ei, 0, 0))
    cspec = lambda width: pl.BlockSpec((1, 1, cap, width), lambda ei, bi: (0, ei, 0, 0))
    if xe_c is None:
        return pl.pallas_call(
            _ffn_kernel, grid=(e, nb),
            in_specs=[lat(w)] + weights, out_specs=lat(d),
            out_shape=jax.ShapeDtypeStruct((nb, e, cap, d), jnp.bfloat16),
            compiler_params=_params("parallel", "arbitrary"), name="expert_ffn",
        )(xe, w_g, w_u, w_d), None
    assert xe_c.shape == (1, e, cap, w)
    return pl.pallas_call(
        functools.partial(_ffn_ctx_kernel, nb=nb), grid=(e, nb + 1),
        in_specs=[lat(w), cspec(w)] + weights, out_specs=[lat(d), cspec(d)],
        out_shape=[jax.ShapeDtypeStruct((nb, e, cap, d), jnp.bfloat16),
                   jax.ShapeDtypeStruct((1, e, cap, d), jnp.bfloat16)],
        compiler_params=_params("parallel", "arbitrary"), name="expert_ffn_ctx",
    )(xe, xe_c, w_g, w_u, w_d)


SLOT_SPLIT = 16


def _combine_kernel(starts_ref, slot_ref, aff_ref, ye_ref, x_ref, gate_ref, o_ref, acc_ref, *, slots_per_sample):
    f32 = jnp.float32
    bi, ti = pl.program_id(0), pl.program_id(1)
    n_exp, d = ye_ref.shape[1], ye_ref.shape[3]
    base = bi * slots_per_sample
    acc_ref[...] = jnp.zeros_like(acc_ref)
    slot_t = slot_ref[0]
    s_hi = jnp.floor(slot_t * (1.0 / SLOT_SPLIT))
    s_hi, s_lo = _bf16(s_hi), _bf16(slot_t - SLOT_SPLIT * s_hi)
    a1, a2, a3 = _split3(aff_ref[0])
    lane = lax.broadcasted_iota(jnp.int32, (slot_t.shape[0], LANES), 1).astype(f32)
    erow = lax.broadcasted_iota(jnp.int32, (n_exp, LANES), 0)

    def expert(e, carry):
        sel = _bf16(jnp.where(erow == e, 1.0, 0.0))
        slot_b = SLOT_SPLIT * _dot(s_hi, sel) + _dot(s_lo, sel)
        slot_b = jnp.where(slot_b < 0.0, -1.0, slot_b + base.astype(f32))
        aff_b = _dot(a1, sel) + _dot(a2, sel) + _dot(a3, sel)
        aff_w = jnp.concatenate([aff_b] * (d // LANES), axis=1)
        lo = base + starts_ref[bi, e, ti]
        hi = base + starts_ref[bi, e, ti + 1]

        def window(w, c2):
            w0 = pl.multiple_of(w * LANES, LANES)
            onehot = _bf16(jnp.where(slot_b - w0.astype(f32) == lane, 1.0, 0.0))
            acc_ref[...] += aff_w * _dot(onehot, ye_ref[0, e, pl.ds(w0, LANES), :])
            return c2

        return lax.fori_loop(lo // LANES, (hi + LANES - 1) // LANES, window, carry)

    lax.fori_loop(0, n_exp, expert, 0)
    o_ref[0] = x_ref[0] + gate_ref[0, 0] * acc_ref[...]


def combine(starts, slot_t, aff_t, ye, x, mod, tm):
    b, l, d = x.shape
    e = slot_t.shape[2]
    pooled = ye.shape[0] == 1 and b > 1
    bm = (lambda bi: bi) if mod.shape[0] == b else (lambda bi: 0)
    tok = lambda w: pl.BlockSpec((1, tm, w), lambda bi, i, st: (bi, i, 0))
    ye_spec = pl.BlockSpec((1,) + ye.shape[1:], lambda bi, i, st: (0 if pooled else bi, 0, 0, 0),
                           pipeline_mode=pl.Buffered(1))
    return pl.pallas_call(
        functools.partial(_combine_kernel, slots_per_sample=ye.shape[2] // b if pooled else 0),
        grid_spec=pltpu.PrefetchScalarGridSpec(
            num_scalar_prefetch=1,
            grid=(b, l // tm),
            in_specs=[tok(e), tok(e), ye_spec, tok(d),
                      pl.BlockSpec((1, 1, 1, d), lambda bi, i, st: (bm(bi), N_MOD - 1, 0, 0))],
            out_specs=tok(d),
            scratch_shapes=[pltpu.VMEM((tm, d), jnp.float32)]),
        out_shape=jax.ShapeDtypeStruct((b, l, d), jnp.float32),
        compiler_params=_params("parallel", "arbitrary"),
        name="combine",
    )(starts, slot_t, aff_t, ye, x, mod)


TM_IN = 2048
TM_MERGE = 256
TM_COMBINE = 512
HG_ROWS = 512


def _route_and_gather(hp, lg, xe_shape, xe_index, cap, tile):
    slot, aff, idx, starts = route(lg, cap, tile)
    xe = gather(idx, hp, xe_shape, xe_index)
    return (starts, slot.transpose(0, 2, 1), aff.transpose(0, 2, 1)), xe


def kernel(x, c, ctx, c_ctx, w_mod, b_mod, norm1, w_in, conv_w, hg_lb_logits, hg_norm, na_q_norm, na_k_norm, na_rpb,
           w_br_a, w_br_b, w_br_c, w_out, norm2, w_router, w_e_gate, w_e_up, w_e_down):
    b, s, d = x.shape
    l = ctx.shape[1]
    depth = w_mod.shape[0]
    e = w_router.shape[-1]
    cap = CAP_FACTOR * s // e
    cap_c = CAP_FACTOR * l // e
    assert b * cap_c == cap, "context rows of all samples fill one expert tile"
    g0 = _gate_tiles(d)
    n_mix = 11 * GROUP_W

    lb_sm = jax.nn.softmax(hg_lb_logits.astype(jnp.float32), axis=0)
    lb_all = jnp.cumsum(lb_sm, axis=0) - lb_sm[0]
    rows = -(-(b + 1) // 8) * 8
    cc = jnp.zeros((rows, d), jnp.float32).at[:b].set(c).at[b].set(c_ctx)
    mod_all = modulation(cc, w_mod, b_mod).reshape(depth, rows, N_MOD, 1, d)

    w_in_r = _bf16(jnp.concatenate([w_in[..., n_mix:], w_in[..., :n_mix]], axis=-1))
    w_a, w_b, w_c, w_o = _bf16(w_br_a), _bf16(w_br_b), _bf16(w_br_c), _bf16(w_out)
    w_rt = _bf16(jnp.swapaxes(w_router, 1, 2))
    w_g, w_u, w_d = _bf16(w_e_gate), _bf16(w_e_up), _bf16(w_e_down)
    bias = jax.vmap(_natten_bias)(na_rpb)
    s0 = jnp.zeros((b, HG_HEADS, HG_D, HG_D), jnp.float32)
    hg_rows = min(HG_ROWS, s)
    tm_in = min(TM_IN, s)
    tm_cmb = min(TM_COMBINE, s)

    xc = ctx
    for li in range(depth):
        last = li == depth - 1
        mod = mod_all[li, :b]
        mod_c = mod_all[li, b:b + 1]
        z = input_projection(x, mod, norm1[li], w_in_r[li], na_q_norm[li], na_k_norm[li], tm_in)
        zc = input_projection(xc.reshape(1, b * l, d), mod_c, norm1[li], w_in_r[li], na_q_norm[li], na_k_norm[li],
                              min(TM_IN, b * l)).reshape(b, l, -1)
        lb_f, lb_b = lb_all[li, 0:1], lb_all[li, 1:2]
        oc_fw, s_fw = hgrn_scan(zc, g0, lb_f, s0, False, l)
        o_fw, _ = hgrn_scan(z, g0, lb_f, s_fw, False, hg_rows)
        oc_bw, s_bw = hgrn_scan(zc, g0, lb_b, s0, True, l)
        o_bw, _ = hgrn_scan(z, g0, lb_b, s_bw, True, hg_rows)
        y_na, yc_na = natten(z, zc, g0, bias[li])

        mw = (conv_w[li], hg_norm[li], w_a[li], w_b[li], w_c[li], w_o[li], norm2[li], w_rt[li])
        x, hp, lg = merge(z, o_fw, o_bw, y_na, x, mod, *mw, TM_MERGE)
        plan, xe = _route_and_gather(hp, lg, (b, e, cap, d // 2), lambda bi, ei: (bi, ei, 0, 0), cap, tm_cmb)
        if last:
            ye, _ = expert_ffn(xe, None, w_g[li], w_u[li], w_d[li])
        else:
            xc, hpc, lgc = merge(zc, oc_fw, oc_bw, yc_na, xc, mod_c, *mw, l)
            plan_c, xe_c = _route_and_gather(hpc, lgc, (1, e, cap, d // 2), lambda bi, ei: (0, ei, bi, 0), cap_c, l)
            ye, ye_c = expert_ffn(xe, xe_c, w_g[li], w_u[li], w_d[li])
            xc = combine(*plan_c, ye_c, xc, mod_c, l)
        x = combine(*plan, ye, x, mod, tm_cmb)
    return x
```

```python
import functools

import jax
import jax.numpy as jnp
from jax import lax
from jax.experimental import pallas as pl
from jax.experimental.pallas import tpu as pltpu

N_MOD = 6
EPS = 1e-6
MASK_VALUE = -1e30
GRID_W = 64
GROUP_W = 512
HG_HEADS = 4
HG_D = 128
HG_CHUNK = 64
HG_SUB = 16
NA_HEADS = 8
NA_HD = 64
WIN_R = 8
WIN_C = 16
CAP_FACTOR = 2
T_CONV_B, T_CONV_C, T_CONV_U, T_HG_Q, T_HG_I, T_HG_FF, T_HG_FB, T_HG_G, T_NA_Q, T_NA_K, T_NA_V = range(11)
N_GATES = 3


def _gate_tiles(d):
    assert (N_GATES * d) % GROUP_W == 0
    return N_GATES * d // GROUP_W

VMEM_LIMIT_BYTES = 48 * 1024 * 1024


def _params(*semantics):
    return pltpu.CompilerParams(dimension_semantics=semantics, vmem_limit_bytes=VMEM_LIMIT_BYTES)


def _silu(v):
    return v * jax.nn.sigmoid(v)


def _bf16(v):
    return v.astype(jnp.bfloat16)


def _dot(a, b):
    return jnp.dot(a, b, preferred_element_type=jnp.float32)


def _dot_nt(a, b):
    return lax.dot_general(a, b, (((1,), (1,)), ((), ())), preferred_element_type=jnp.float32)


def _mod_kernel(c_ref, w_ref, b_ref, o_ref):
    cond = _bf16(_silu(c_ref[...]))
    o_ref[0] = _dot(cond, _bf16(w_ref[0])) + b_ref[0]


def modulation(cc, w_mod, b_mod):
    depth, d, nm = w_mod.shape
    r = cc.shape[0]
    tn = d
    return pl.pallas_call(
        _mod_kernel,
        grid=(depth, nm // tn),
        in_specs=[pl.BlockSpec((r, d), lambda l, j: (0, 0)),
                  pl.BlockSpec((1, d, tn), lambda l, j: (l, 0, j)),
                  pl.BlockSpec((1, 1, tn), lambda l, j: (l, 0, j))],
        out_specs=pl.BlockSpec((1, r, tn), lambda l, j: (l, 0, j)),
        out_shape=jax.ShapeDtypeStruct((depth, r, nm), jnp.float32),
        compiler_params=_params("parallel", "parallel"),
        name="modulation",
    )(cc, w_mod, b_mod.reshape(depth, 1, nm))


def _modnorm(x, w, shift, scale):
    ms = jnp.mean(x * x, axis=-1, keepdims=True)
    return (x * lax.rsqrt(ms + EPS) * w) * (1.0 + scale) + shift


def _group_rms(acc, gmat, w_tiled):
    ms = _dot(_bf16(acc * acc), gmat)
    return acc * lax.rsqrt(ms + EPS) * w_tiled


def _inproj_kernel(x_ref, shift_ref, scale_ref, nw_ref, w_ref, g64_ref, qn_ref, kn_ref, o_ref, h_ref, *, g0):
    j = pl.program_id(2) - g0

    @pl.when(pl.program_id(2) == 0)
    def _():
        h_ref[...] = _bf16(_modnorm(x_ref[0], nw_ref[...], shift_ref[0, 0], scale_ref[0, 0]))

    acc = lambda: _dot(h_ref[...], w_ref[...])
    plain = (j != T_HG_Q) & (j != T_NA_Q) & (j != T_NA_K)

    @pl.when(plain)
    def _():
        o_ref[0] = acc().astype(o_ref.dtype)

    @pl.when(j == T_HG_Q)
    def _():
        o_ref[0] = (_silu(acc()) * (HG_D ** -0.5)).astype(o_ref.dtype)

    @pl.when(j == T_NA_Q)
    def _():
        o_ref[0] = (_group_rms(acc(), g64_ref[...], qn_ref[...]) * (NA_HD ** -0.5)).astype(o_ref.dtype)

    @pl.when(j == T_NA_K)
    def _():
        o_ref[0] = _group_rms(acc(), g64_ref[...], kn_ref[...]).astype(o_ref.dtype)


def _block_diag_mean(n, group):
    idx = jnp.arange(n) // group
    return _bf16(jnp.where(idx[:, None] == idx[None, :], 1.0 / group, 0.0))


def input_projection(x, mod, norm_w, w_in, qn, kn, tm):
    bx, s, d = x.shape
    n = w_in.shape[1]
    tn = GROUP_W
    g64 = _block_diag_mean(tn, NA_HD)
    tile = lambda v: jnp.tile(v.reshape(1, -1), (1, tn // v.shape[-1]))
    return pl.pallas_call(
        functools.partial(_inproj_kernel, g0=_gate_tiles(d)),
        grid=(bx, s // tm, n // tn),
        in_specs=[pl.BlockSpec((1, tm, d), lambda b, i, j: (b, i, 0)),
                  pl.BlockSpec((1, 1, 1, d), lambda b, i, j: (b, 0, 0, 0)),
                  pl.BlockSpec((1, 1, 1, d), lambda b, i, j: (b, 1, 0, 0)),
                  pl.BlockSpec((1, d), lambda b, i, j: (0, 0)),
                  pl.BlockSpec((d, tn), lambda b, i, j: (0, j)),
                  pl.BlockSpec((tn, tn), lambda b, i, j: (0, 0)),
                  pl.BlockSpec((1, tn), lambda b, i, j: (0, 0)),
                  pl.BlockSpec((1, tn), lambda b, i, j: (0, 0))],
        out_specs=pl.BlockSpec((1, tm, tn), lambda b, i, j: (b, i, j)),
        out_shape=jax.ShapeDtypeStruct((bx, s, n), jnp.bfloat16),
        scratch_shapes=[pltpu.VMEM((tm, d), jnp.bfloat16)],
        compiler_params=_params("parallel", "parallel", "arbitrary"),
        name="input_projection",
    )(x, mod, mod, norm_w.reshape(1, d), w_in, g64, tile(qn), tile(kn))


def _split3(v):
    h1 = _bf16(v)
    r1 = v - h1.astype(jnp.float32)
    h2 = _bf16(r1)
    h3 = _bf16(r1 - h2.astype(jnp.float32))
    return h1, h2, h3


def _hgrn_gates(f, lb, tri):
    sig = jax.nn.sigmoid(f)
    lg = jnp.log(lb + (1.0 - lb) * sig)
    k = (1.0 - lb) * (1.0 - sig)
    h1, h2, h3 = _split3(lg)
    return k, _dot(tri, h1) + _dot(tri, h2) + _dot(tri, h3)


HG_GRP = 8


def _hgrn_head_products(q, v, k, bc, st, ones, rev):
    c = HG_CHUNK
    o_inter = _dot_nt(_bf16(q * jnp.exp(bc)), _bf16(st))
    b_end = bc[0:1] if rev else bc[c - 1:c]
    kd = k * jnp.exp(b_end - bc)
    v_t = _bf16(v.astype(jnp.float32).T)
    st_new = st * jnp.exp(b_end) + _dot(v_t, _bf16(kd))

    row = lax.broadcasted_iota(jnp.int32, (HG_GRP, HG_D), 0)
    a_off, ps, where = [], [], []
    for i in range(c // HG_SUB):
        lo, hi = i * HG_SUB, (i + 1) * HG_SUB
        qi, bi = q[lo:hi], bc[lo:hi]
        prev = None
        if not rev and i > 0:
            ref, prev = bc[lo - 1:lo], slice(0, lo)
        if rev and hi < c:
            ref, prev = bc[hi:hi + 1], slice(hi, c)
        if prev is None:
            a_off.append(None)
        else:
            qt = _bf16(qi * jnp.exp(bi - ref))
            kt = _bf16(k[prev] * jnp.exp(ref - bc[prev]))
            pad = jnp.zeros((c - kt.shape[0], HG_D), kt.dtype)
            kt = jnp.concatenate([pad, kt] if rev else [kt, pad], axis=0)
            a_off.append(_dot_nt(qt, kt))
        for s in range(lo, hi):
            for g in range(lo // HG_GRP, hi // HG_GRP):
                g_lo = g * HG_GRP
                sees = (g_lo <= s) if rev else (g_lo + HG_GRP > s)
                if not sees:
                    continue
                diff = bc[g_lo:g_lo + HG_GRP] - bc[s:s + 1]
                if g_lo <= s < g_lo + HG_GRP:
                    valid = (row <= s - g_lo) if rev else (row >= s - g_lo)
                    diff = jnp.where(valid, diff, MASK_VALUE)
                where.append((g, s))
                ps.append(q[g_lo:g_lo + HG_GRP] * k[s:s + 1] * jnp.exp(diff))
    rsum = _dot(_bf16(jnp.concatenate(ps, axis=0)), ones)
    return o_inter, st_new, a_off, rsum, where


def _hgrn_head_output(v, o_inter, a_off, rsum, where):
    c = HG_CHUNK
    lane = lax.broadcasted_iota(jnp.int32, (HG_GRP, c), 1)
    groups = [None] * (c // HG_GRP)
    for i, a in enumerate(a_off):
        for g in range(i * HG_SUB // HG_GRP, (i + 1) * HG_SUB // HG_GRP):
            r = (g - i * HG_SUB // HG_GRP) * HG_GRP
            groups[g] = jnp.zeros((HG_GRP, c), jnp.float32) if a is None else a[r:r + HG_GRP]
    for n, (g, s) in enumerate(where):
        groups[g] = jnp.where(lane == s, rsum[n * HG_GRP:(n + 1) * HG_GRP, :c], groups[g])
    return o_inter + _dot(_bf16(jnp.concatenate(groups, axis=0)), v)


def _hgrn_kernel(qf_ref, vf_ref, ff_ref, qb_ref, vb_ref, fb_ref, lb_ref, s0_ref, tri_ref, ones_ref,
                 of_ref, ob_ref, sfin_ref, st_ref, *, n_chunks):
    cb = pl.program_id(1)

    @pl.when(cb == 0)
    def _():
        st_ref[...] = s0_ref[0]

    ones = ones_ref[...]
    heads = [slice(h * HG_D, (h + 1) * HG_D) for h in range(HG_HEADS)]
    streams = ((qf_ref, vf_ref, ff_ref, of_ref, False), (qb_ref, vb_ref, fb_ref, ob_ref, True))

    def chunk(it, carry):
        loaded = []
        for di, (q_ref, v_ref, f_ref, _, rev) in enumerate(streams):
            r0 = pl.multiple_of(((n_chunks - 1 - it) if rev else it) * HG_CHUNK, HG_CHUNK)
            q = q_ref[0, pl.ds(r0, HG_CHUNK), :].astype(jnp.float32)
            v = v_ref[0, pl.ds(r0, HG_CHUNK), :]
            k, bc = _hgrn_gates(f_ref[0, pl.ds(r0, HG_CHUNK), :].astype(jnp.float32), lb_ref[di:di + 1], tri_ref[di])
            loaded.append((r0, q, v, k, bc))
        stage = [[_hgrn_head_products(q[:, hs], v[:, hs], k[:, hs], bc[:, hs], st_ref[di, h], ones, streams[di][4])
                  for h, hs in enumerate(heads)] for di, (_, q, v, k, bc) in enumerate(loaded)]
        for di, (r0, _, v, _, _) in enumerate(loaded):
            outs = []
            for h, (hs, (o_inter, st_new, a_off, rsum, where)) in enumerate(zip(heads, stage[di])):
                st_ref[di, h] = st_new
                outs.append(_hgrn_head_output(v[:, hs], o_inter, a_off, rsum, where))
            streams[di][3][0, pl.ds(r0, HG_CHUNK), :] = jnp.concatenate(outs, axis=1)
        return carry

    lax.fori_loop(0, n_chunks, chunk, 0)

    @pl.when(cb == pl.num_programs(1) - 1)
    def _():
        sfin_ref[0] = st_ref[...]


def hgrn_scan(z, g0, lb, s0, tc):
    b, l, _ = z.shape
    nb = l // tc
    ci = jnp.arange(HG_CHUNK)
    tri = _bf16(jnp.stack([ci[None, :] <= ci[:, None], ci[None, :] >= ci[:, None]]))
    ones = jnp.ones((HG_D, HG_D), jnp.bfloat16)
    fwd = lambda t: pl.BlockSpec((1, tc, GROUP_W), lambda bi, c: (bi, c, g0 + t))
    bwd = lambda t: pl.BlockSpec((1, tc, GROUP_W), lambda bi, c: (bi, nb - 1 - c, g0 + t))
    st_spec = pl.BlockSpec((1, 2, HG_HEADS, HG_D, HG_D), lambda bi, c: (bi, 0, 0, 0, 0))
    return pl.pallas_call(
        functools.partial(_hgrn_kernel, n_chunks=tc // HG_CHUNK),
        grid=(b, nb),
        in_specs=[fwd(T_HG_Q), fwd(T_HG_I), fwd(T_HG_FF), bwd(T_HG_Q), bwd(T_HG_I), bwd(T_HG_FB),
                  pl.BlockSpec((2, GROUP_W), lambda bi, c: (0, 0)),
                  st_spec,
                  pl.BlockSpec((2, HG_CHUNK, HG_CHUNK), lambda bi, c: (0, 0, 0)),
                  pl.BlockSpec((HG_D, HG_D), lambda bi, c: (0, 0))],
        out_specs=[pl.BlockSpec((1, tc, GROUP_W), lambda bi, c: (bi, c, 0)),
                   pl.BlockSpec((1, tc, GROUP_W), lambda bi, c: (bi, nb - 1 - c, 0)), st_spec],
        out_shape=[jax.ShapeDtypeStruct((b, l, GROUP_W), jnp.float32),
                   jax.ShapeDtypeStruct((b, l, GROUP_W), jnp.float32),
                   jax.ShapeDtypeStruct(s0.shape, jnp.float32)],
        scratch_shapes=[pltpu.VMEM((2, HG_HEADS, HG_D, HG_D), jnp.float32)],
        compiler_params=_params("parallel", "arbitrary"),
        name="hgrn_scan",
    )(z, z, z, z, z, z, lb, s0, tri, ones)


NA_ROWS = 4


def _softmax_pv(scores, values):
    m = functools.reduce(jnp.maximum, [jnp.max(s, axis=-1, keepdims=True) for s in scores])
    ps = [jnp.exp(s - m) for s in scores]
    den = functools.reduce(jnp.add, [jnp.sum(p, axis=-1, keepdims=True) for p in ps])
    num = functools.reduce(jnp.add, [_dot(_bf16(p), v) for p, v in zip(ps, values)])
    return num / den


def _natten_kernel(q_ref, k_ref, v_ref, qc_ref, kc_ref, vc_ref, bias_ref, o_ref, oc_ref, *, rows):
    lane = lax.broadcasted_iota(jnp.int32, (1, 2 * NA_HD), 1)
    halves = (lane < NA_HD, lane >= NA_HD)
    kc = kc_ref[0]
    vc = vc_ref[0]
    qc = qc_ref[0]
    zero = jnp.zeros((), q_ref.dtype)

    oc = jnp.zeros(qc.shape, jnp.float32)
    for half in halves:
        o_h = _softmax_pv([_dot_nt(jnp.where(half, qc, zero), kc)], [vc])
        oc = jnp.where(half, o_h, oc)
    oc_ref[0] = oc.astype(oc_ref.dtype)

    def row_group(it, carry):
        work = []
        for j in range(NA_ROWS):
            r = it * NA_ROWS + j
            r0 = jnp.clip(r - WIN_R // 2, 0, rows - WIN_R)
            delta = r0 - r + WIN_R - 1
            qs = pl.multiple_of(r * GRID_W, GRID_W)
            ks = pl.multiple_of(r0 * GRID_W, GRID_W)
            qr = q_ref[0, pl.ds(qs, GRID_W), :]
            kl = k_ref[0, pl.ds(ks, WIN_R * GRID_W), :]
            vl = v_ref[0, pl.ds(ks, WIN_R * GRID_W), :]
            for hh, half in enumerate(halves):
                qm = jnp.where(half, qr, zero)
                work.append((qs, half, vl, _dot_nt(qm, kl) + bias_ref[0, hh, delta], _dot_nt(qm, kc)))
        outs = {}
        for j, (qs, half, vl, s_loc, s_ctx) in enumerate(work):
            o_h = _softmax_pv([s_loc, s_ctx], [vl, vc])
            outs[j // 2] = o_h if j % 2 == 0 else jnp.where(half, o_h, outs[j // 2])
        for j in range(NA_ROWS):
            o_ref[0, pl.ds(work[2 * j][0], GRID_W), :] = outs[j].astype(o_ref.dtype)
        return carry

    lax.fori_loop(0, rows // NA_ROWS, row_group, 0)


def _natten_bias(rpb):
    qcol = jnp.arange(GRID_W)[:, None]
    kcol = jnp.arange(GRID_W)[None, :]
    wstart = jnp.clip(qcol - WIN_C // 2, 0, GRID_W - WIN_C)
    in_win = (kcol >= wstart) & (kcol < wstart + WIN_C)
    dc = jnp.clip(kcol - qcol + WIN_C - 1, 0, 2 * WIN_C - 2)
    dr = jnp.arange(WIN_R)[:, None] + jnp.arange(WIN_R)[None, :]
    t = rpb[:, :, dc][:, dr]
    t = jnp.where(in_win[None, None, None], t.astype(jnp.float32), MASK_VALUE)
    t = t.transpose(0, 1, 3, 2, 4).reshape(NA_HEADS // 2, 2, WIN_R, GRID_W, WIN_R * GRID_W)
    return t


def natten(z, zc, g0, bias):
    b, s, _ = z.shape
    l = zc.shape[1]
    rows = s // GRID_W
    assert rows >= WIN_R
    pairs = NA_HEADS // 2
    per_tile = GROUP_W // (2 * NA_HD)
    spec = lambda n, t: pl.BlockSpec((1, n, 2 * NA_HD), lambda bi, p: (bi, 0, (g0 + t) * per_tile + p))
    return pl.pallas_call(
        functools.partial(_natten_kernel, rows=rows),
        grid=(b, pairs),
        in_specs=[spec(s, T_NA_Q), spec(s, T_NA_K), spec(s, T_NA_V),
                  spec(l, T_NA_Q), spec(l, T_NA_K), spec(l, T_NA_V),
                  pl.BlockSpec((1, 2, WIN_R, GRID_W, WIN_R * GRID_W), lambda bi, p: (p, 0, 0, 0, 0))],
        out_specs=[pl.BlockSpec((1, s, 2 * NA_HD), lambda bi, p: (bi, 0, p)),
                   pl.BlockSpec((1, l, 2 * NA_HD), lambda bi, p: (bi, 0, p))],
        out_shape=[jax.ShapeDtypeStruct((b, s, GROUP_W), jnp.bfloat16),
                   jax.ShapeDtypeStruct((b, l, GROUP_W), jnp.bfloat16)],
        compiler_params=_params("parallel", "parallel"),
        name="natten",
    )(z, z, z, zc, zc, zc, bias)


HALO = 16


def _pack_halves(h):
    half = h.shape[1] // 2
    lo = pltpu.bitcast(_bf16(h[:, :half]).astype(jnp.float32), jnp.uint32)
    hi = pltpu.bitcast(_bf16(h[:, half:]).astype(jnp.float32), jnp.uint32)
    return (lo >> 16) | (hi & jnp.uint32(0xFFFF0000))


def _unpack_halves(w):
    lo = pltpu.bitcast(w << 16, jnp.float32)
    hi = pltpu.bitcast(w & jnp.uint32(0xFFFF0000), jnp.float32)
    return _bf16(jnp.concatenate([lo, hi], axis=1))


def _merge_kernel(zb_ref, zc_ref, zu_ref, pc_ref, pu_ref, nc_ref, nu_ref, cw_ref,
                  ofw_ref, obw_ref, zg_ref, hgn_ref, g128_ref, yna_ref, ga_ref, gb_ref, gc_ref,
                  x_ref, gate_ref, wa_ref, wb_ref, wc_ref, wo_ref,
                  n2_ref, shift_ref, scale_ref, wr_ref,
                  xo_ref, hp_ref, lg_ref, vs_ref):
    i = pl.program_id(1)
    tm = zb_ref.shape[1]
    f32 = jnp.float32

    v = zc_ref[0].astype(f32) * zu_ref[0].astype(f32)
    vp = pc_ref[0, HALO - 1:HALO].astype(f32) * pu_ref[0, HALO - 1:HALO].astype(f32)
    vn = nc_ref[0, 0:1].astype(f32) * nu_ref[0, 0:1].astype(f32)
    vs_ref[7:8, :] = jnp.where(i == 0, 0.0, vp)
    vs_ref[8:8 + tm, :] = v
    vs_ref[8 + tm:9 + tm, :] = jnp.where(i == pl.num_programs(1) - 1, 0.0, vn)
    cw = cw_ref[...]
    y_cv = zb_ref[0].astype(f32) * (cw[0:1] * vs_ref[7:7 + tm, :] + cw[1:2] * v + cw[2:3] * vs_ref[9:9 + tm, :])

    o = ofw_ref[0] + obw_ref[0]
    y_hg = _group_rms(o, g128_ref[...], hgn_ref[...]) * _silu(zg_ref[0].astype(f32))

    m = (jax.nn.sigmoid(ga_ref[0].astype(f32)) * _dot(_bf16(y_cv), wa_ref[...])
         + jax.nn.sigmoid(gb_ref[0].astype(f32)) * _dot(_bf16(y_hg), wb_ref[...])
         + jax.nn.sigmoid(gc_ref[0].astype(f32)) * _dot(yna_ref[0], wc_ref[...]))
    x_new = x_ref[0] + gate_ref[0, 0] * _dot(_bf16(m), wo_ref[...])
    xo_ref[0] = x_new

    h2 = _modnorm(x_new, n2_ref[...], shift_ref[0, 0], scale_ref[0, 0])
    hp_ref[0] = _pack_halves(h2)
    lg_ref[0] = _dot_nt(wr_ref[...], _bf16(h2))


def merge(z, o_fw, o_bw, y_na, x, mod, conv_w, hg_norm, w_a, w_b, w_c, w_o, norm2, w_rt, tm):
    b, l, d = x.shape
    e = w_rt.shape[0]
    nt = l // tm
    per = tm // HALO
    n_halo = l // HALO
    g0 = _gate_tiles(d)
    bm = (lambda bi: bi) if mod.shape[0] == b else (lambda bi: 0)
    zt = lambda t: pl.BlockSpec((1, tm, GROUP_W), lambda bi, i: (bi, i, g0 + t))
    zprev = lambda t: pl.BlockSpec((1, HALO, GROUP_W), lambda bi, i: (bi, jnp.maximum(i * per - 1, 0), g0 + t))
    znext = lambda t: pl.BlockSpec((1, HALO, GROUP_W),
                                   lambda bi, i: (bi, jnp.minimum((i + 1) * per, n_halo - 1), g0 + t))
    act = lambda w: pl.BlockSpec((1, tm, w), lambda bi, i: (bi, i, 0))
    zgate = lambda k: pl.BlockSpec((1, tm, d), lambda bi, i: (bi, i, k))
    modrow = lambda k: pl.BlockSpec((1, 1, 1, d), lambda bi, i: (bm(bi), k, 0, 0))
    full = lambda a: pl.BlockSpec(a.shape, lambda bi, i: (0,) * a.ndim)
    g128 = _block_diag_mean(GROUP_W, HG_D)
    hgn = jnp.tile(hg_norm.reshape(1, -1), (1, GROUP_W // HG_D))
    n2 = norm2.reshape(1, d)
    return pl.pallas_call(
        _merge_kernel,
        grid=(b, nt),
        in_specs=[zt(T_CONV_B), zt(T_CONV_C), zt(T_CONV_U),
                  zprev(T_CONV_C), zprev(T_CONV_U), znext(T_CONV_C), znext(T_CONV_U), full(conv_w),
                  act(GROUP_W), act(GROUP_W), zt(T_HG_G), full(hgn), full(g128), act(GROUP_W),
                  zgate(0), zgate(1), zgate(2),
                  act(d), modrow(2), full(w_a), full(w_b), full(w_c), full(w_o),
                  full(n2), modrow(3), modrow(4), full(w_rt)],
        out_specs=[act(d), act(d // 2), pl.BlockSpec((1, e, tm), lambda bi, i: (bi, 0, i))],
        out_shape=[jax.ShapeDtypeStruct((b, l, d), jnp.float32),
                   jax.ShapeDtypeStruct((b, l, d // 2), jnp.uint32),
                   jax.ShapeDtypeStruct((b, e, l), jnp.float32)],
        scratch_shapes=[pltpu.VMEM((tm + 16, GROUP_W), jnp.float32)],
        compiler_params=_params("parallel", "arbitrary"),
        name="merge",
    )(z, z, z, z, z, z, z, conv_w, o_fw, o_bw, z, hgn, g128, y_na, z, z, z,
      x, mod, w_a, w_b, w_c, w_o, n2, mod, mod, w_rt)


LANES = 128
ROUTE_K_CHUNK = 1024


def _count(mask):
    return jnp.sum(jnp.where(mask, 1.0, 0.0), axis=1, keepdims=True)


def _route_kernel(lg_ref, tmat_ref, excl_ref, slot_ref, aff_ref, idx_ref, starts_ref, *, cap, tile):
    f32 = jnp.float32
    lg = lg_ref[0]
    e, s = lg.shape
    ex = jnp.exp(lg - jnp.max(lg, axis=0, keepdims=True))
    aff = ex / jnp.sum(ex, axis=0, keepdims=True)
    aff_ref[0] = aff
    bits = pltpu.bitcast(aff, jnp.int32)

    def thr_bit(it, thr):
        cand = thr | (jnp.int32(1) << (30 - it))
        return jnp.where(_count(bits >= cand) >= cap, cand, thr)

    thr = lax.fori_loop(0, 31, thr_bit, jnp.zeros((e, 1), jnp.int32))
    gt = bits > thr
    eq = bits == thr
    need = cap - _count(gt)
    tok = lax.broadcasted_iota(jnp.int32, (e, s), 1)
    nbits = s.bit_length()

    def end_bit(it, end):
        cand = end + (jnp.int32(1) << (nbits - 1 - it))
        ok = (cand <= s) & (_count(eq & (tok < cand)) <= need)
        return jnp.where(ok, cand, end)

    end = lax.fori_loop(0, nbits, end_bit, jnp.zeros((e, 1), jnp.int32))
    sel = gt | (eq & (tok < end))

    self = jnp.where(sel, 1.0, 0.0)
    offs = jnp.zeros((e, 1), f32)
    pieces = []
    tile_lane = lax.broadcasted_iota(jnp.int32, (e, LANES), 1)
    starts = jnp.zeros((e, LANES), f32)
    for c in range(s // LANES):
        if (c * LANES) % tile == 0:
            starts = jnp.where(tile_lane == c * LANES // tile, offs, starts)
        blk = self[:, c * LANES:(c + 1) * LANES]
        pieces.append(_dot(_bf16(blk), excl_ref[...]) + offs)
        offs = offs + jnp.sum(blk, axis=1, keepdims=True)
    starts_ref[0] = jnp.where(tile_lane == s // tile, offs, starts).astype(jnp.int32)
    slot_ref[0] = jnp.where(sel, jnp.concatenate(pieces, axis=1), -1.0)

    piota = lax.broadcasted_iota(jnp.int32, (cap, 1), 0).astype(f32)
    lane = lax.broadcasted_iota(jnp.int32, (cap, LANES), 1)
    kc = min(ROUTE_K_CHUNK, s)
    idx_ref[0] = jnp.zeros((cap, LANES), jnp.int32)

    def expert(ee, carry):
        res = jnp.zeros((cap, LANES), f32)
        for c in range(s // kc):
            row = slot_ref[0, pl.ds(ee, 1), c * kc:(c + 1) * kc]
            onehot = _bf16(jnp.where(row == piota, 1.0, 0.0))
            res = res + _dot(onehot, tmat_ref[c * kc:(c + 1) * kc, :])
        tok_idx = (res[:, 0:1] * 64.0 + res[:, 1:2]).astype(jnp.int32)
        idx_ref[0] = jnp.where(lane == ee, tok_idx, idx_ref[0])
        return carry

    lax.fori_loop(0, e, expert, 0)


def route(lg, cap, tile):
    b, e, s = lg.shape
    assert tile % LANES == 0 and s % tile == 0 and s // tile < LANES
    t = jnp.arange(s)
    tmat = jnp.zeros((s, LANES), jnp.bfloat16).at[:, 0].set(_bf16(t // 64)).at[:, 1].set(_bf16(t % 64))
    a = jnp.arange(LANES)
    excl = _bf16(a[:, None] < a[None, :])
    spec = pl.BlockSpec((1, e, s), lambda bi: (bi, 0, 0))
    slot, aff, idx, starts = pl.pallas_call(
        functools.partial(_route_kernel, cap=cap, tile=tile),
        grid=(b,),
        in_specs=[spec, pl.BlockSpec((s, LANES), lambda bi: (0, 0)), pl.BlockSpec((LANES, LANES), lambda bi: (0, 0))],
        out_specs=[spec, spec, pl.BlockSpec((1, cap, LANES), lambda bi: (bi, 0, 0)),
                   pl.BlockSpec((1, e, LANES), lambda bi: (bi, 0, 0))],
        out_shape=[jax.ShapeDtypeStruct((b, e, s), jnp.float32), jax.ShapeDtypeStruct((b, e, s), jnp.float32),
                   jax.ShapeDtypeStruct((b, cap, LANES), jnp.int32), jax.ShapeDtypeStruct((b, e, LANES), jnp.int32)],
        compiler_params=_params("parallel"),
        name="route",
    )(lg, tmat, excl)
    return slot, aff, idx[:, :, :e].transpose(0, 2, 1), starts[:, :, :s // tile + 1]


def _gather_kernel(idx_ref, h_ref, o_ref, *, cap):
    def body(j, carry):
        t = idx_ref[0, 0, j]
        o_ref[0, 0, pl.ds(j, 1), :] = h_ref[0, pl.ds(t, 1), :]
        return carry

    lax.fori_loop(0, cap, body, 0, unroll=8)


def gather(idx, hp, out_shape, out_index):
    b, e, cap = idx.shape
    _, s, w = hp.shape
    return pl.pallas_call(
        functools.partial(_gather_kernel, cap=cap),
        grid=(b, e),
        in_specs=[pl.BlockSpec((1, 1, cap), lambda bi, ei: (bi * e + ei, 0, 0), memory_space=pltpu.SMEM),
                  pl.BlockSpec((1, s, w), lambda bi, ei: (bi, 0, 0))],
        out_specs=pl.BlockSpec((1, 1, cap, w), out_index),
        out_shape=jax.ShapeDtypeStruct(out_shape, jnp.uint32),
        compiler_params=_params("parallel", "arbitrary"),
        name="gather",
    )(idx.reshape(b * e, 1, cap), hp)


FFN_CHUNK = 512


def _ffn_body(xw, wg_ref, wu_ref, wd_ref):
    x = _unpack_halves(xw)
    f = wg_ref.shape[2]
    fc = min(FFN_CHUNK, f)
    acc = jnp.zeros((x.shape[0], wd_ref.shape[2]), jnp.float32)
    for c in range(f // fc):
        a = _dot(x, wg_ref[0, :, c * fc:(c + 1) * fc])
        u = _dot(x, wu_ref[0, :, c * fc:(c + 1) * fc])
        acc = acc + _dot(_bf16(_silu(a) * u), wd_ref[0, c * fc:(c + 1) * fc, :])
    return _bf16(acc)


def _ffn_kernel(xl_ref, wg_ref, wu_ref, wd_ref, yl_ref):
    yl_ref[0, 0] = _ffn_body(xl_ref[0, 0], wg_ref, wu_ref, wd_ref)


def _ffn_ctx_kernel(xl_ref, xc_ref, wg_ref, wu_ref, wd_ref, yl_ref, yc_ref, *, nb):
    is_ctx = pl.program_id(1) == nb
    y = _ffn_body(jnp.where(is_ctx, xc_ref[0, 0], xl_ref[0, 0]), wg_ref, wu_ref, wd_ref)

    @pl.when(jnp.logical_not(is_ctx))
    def _():
        yl_ref[0, 0] = y

    @pl.when(is_ctx)
    def _():
        yc_ref[0, 0] = y


def expert_ffn(xe, xe_c, w_g, w_u, w_d):
    nb, e, cap, w = xe.shape
    d, f = w_g.shape[1:]
    wspec = lambda shape: pl.BlockSpec((1,) + shape, lambda ei, bi: (ei, 0, 0))
    weights = [wspec((d, f)), wspec((d, f)), wspec((f, d))]
    lat = lambda width: pl.BlockSpec((1, 1, cap, width), lambda ei, bi: (jnp.minimum(bi, nb - 1), ei, 0, 0))
    cspec = lambda width: pl.BlockSpec((1, 1, cap, width), lambda ei, bi: (0, ei, 0, 0))
    if xe_c is None:
        return pl.pallas_call(
            _ffn_kernel, grid=(e, nb),
            in_specs=[lat(w)] + weights, out_specs=lat(d),
            out_shape=jax.ShapeDtypeStruct((nb, e, cap, d), jnp.bfloat16),
            compiler_params=_params("parallel", "arbitrary"), name="expert_ffn",
        )(xe, w_g, w_u, w_d), None
    assert xe_c.shape == (1, e, cap, w)
    return pl.pallas_call(
        functools.partial(_ffn_ctx_kernel, nb=nb), grid=(e, nb + 1),
        in_specs=[lat(w), cspec(w)] + weights, out_specs=[lat(d), cspec(d)],
        out_shape=[jax.ShapeDtypeStruct((nb, e, cap, d), jnp.bfloat16),
                   jax.ShapeDtypeStruct((1, e, cap, d), jnp.bfloat16)],
        compiler_params=_params("parallel", "arbitrary"), name="expert_ffn_ctx",
    )(xe, xe_c, w_g, w_u, w_d)


SLOT_SPLIT = 16
COMBINE_WIN = 256


def _combine_kernel(starts_ref, slot_ref, aff_ref, ye_ref, x_ref, gate_ref, o_ref, acc_ref, slotb_ref, affb_ref, *,
                    slots_per_sample):
    f32 = jnp.float32
    bi, ti = pl.program_id(0), pl.program_id(1)
    n_exp, n_slots, d = ye_ref.shape[1:]
    win = min(COMBINE_WIN, n_slots)
    base = bi * slots_per_sample
    slot_t = slot_ref[0]
    tm = slot_t.shape[0]
    aff_t = aff_ref[0]
    slot_t = jnp.where(slot_t < 0.0, -1.0, slot_t + base.astype(f32))
    expert_lane = lax.broadcasted_iota(jnp.int32, slot_t.shape, 1)
    lane = lax.broadcasted_iota(jnp.int32, (tm, win), 1).astype(f32)

    def column(tile, e):
        col = jnp.sum(jnp.where(expert_lane == e, tile, 0.0), axis=1, keepdims=True)
        return jnp.broadcast_to(col, (tm, LANES))
    widen = lambda v, width: jnp.concatenate([v] * (width // LANES), axis=1)

    def first_window(e):
        lo = base + starts_ref[bi, e, ti]
        return jnp.minimum(lo // LANES * LANES, n_slots - win)

    def pick(slot_b, e, nominal):
        start = pl.multiple_of(jnp.minimum(nominal, n_slots - win), LANES)
        slot_w = widen(slot_b, win)
        hit = (slot_w - start.astype(f32) == lane) & (slot_w >= nominal.astype(f32))
        return _dot(_bf16(jnp.where(hit, 1.0, 0.0)), ye_ref[0, e, pl.ds(start, win), :])

    for e in range(n_exp):
        slot_b = column(slot_t, e)
        aff_b = column(aff_t, e)
        slotb_ref[e] = slot_b
        affb_ref[e] = aff_b
        term = widen(aff_b, d) * pick(slot_b, e, first_window(e))
        if e == 0:
            acc_ref[...] = term
        else:
            acc_ref[...] += term

    def expert(e, carry):
        ws = first_window(e)
        hi = base + starts_ref[bi, e, ti + 1]

        def window(k, c2):
            acc_ref[...] += widen(affb_ref[e], d) * pick(slotb_ref[e], e, ws + k * win)
            return c2

        return lax.fori_loop(1, (hi - ws + win - 1) // win, window, carry)

    lax.fori_loop(0, n_exp, expert, 0)
    o_ref[0] = x_ref[0] + gate_ref[0, 0] * acc_ref[...]


def combine(starts, slot_t, aff_t, ye, x, mod, tm):
    b, l, d = x.shape
    e = slot_t.shape[2]
    pooled = ye.shape[0] == 1 and b > 1
    bm = (lambda bi: bi) if mod.shape[0] == b else (lambda bi: 0)
    tok = lambda w: pl.BlockSpec((1, tm, w), lambda bi, i, st: (bi, i, 0))
    ye_spec = pl.BlockSpec((1,) + ye.shape[1:], lambda bi, i, st: (0 if pooled else bi, 0, 0, 0),
                           pipeline_mode=pl.Buffered(1))
    return pl.pallas_call(
        functools.partial(_combine_kernel, slots_per_sample=ye.shape[2] // b if pooled else 0),
        grid_spec=pltpu.PrefetchScalarGridSpec(
            num_scalar_prefetch=1,
            grid=(b, l // tm),
            in_specs=[tok(e), tok(e), ye_spec, tok(d),
                      pl.BlockSpec((1, 1, 1, d), lambda bi, i, st: (bm(bi), N_MOD - 1, 0, 0))],
            out_specs=tok(d),
            scratch_shapes=[pltpu.VMEM((tm, d), jnp.float32), pltpu.VMEM((e, tm, LANES), jnp.float32),
                            pltpu.VMEM((e, tm, LANES), jnp.float32)]),
        out_shape=jax.ShapeDtypeStruct((b, l, d), jnp.float32),
        compiler_params=_params("parallel", "arbitrary"),
        name="combine",
    )(starts, slot_t, aff_t, ye, x, mod)


TM_IN = 2048
TM_MERGE = 256
TM_COMBINE = 512
HG_ROWS = 512


def _route_and_gather(hp, lg, xe_shape, xe_index, cap, tile):
    slot, aff, idx, starts = route(lg, cap, tile)
    xe = gather(idx, hp, xe_shape, xe_index)
    return (starts, slot.transpose(0, 2, 1), aff.transpose(0, 2, 1)), xe


def kernel(x, c, ctx, c_ctx, w_mod, b_mod, norm1, w_in, conv_w, hg_lb_logits, hg_norm, na_q_norm, na_k_norm, na_rpb,
           w_br_a, w_br_b, w_br_c, w_out, norm2, w_router, w_e_gate, w_e_up, w_e_down):
    b, s, d = x.shape
    l = ctx.shape[1]
    depth = w_mod.shape[0]
    e = w_router.shape[-1]
    cap = CAP_FACTOR * s // e
    cap_c = CAP_FACTOR * l // e
    assert b * cap_c == cap, "context rows of all samples fill one expert tile"
    g0 = _gate_tiles(d)
    n_mix = 11 * GROUP_W

    lb_sm = jax.nn.softmax(hg_lb_logits.astype(jnp.float32), axis=0)
    lb_all = jnp.cumsum(lb_sm, axis=0) - lb_sm[0]
    rows = -(-(b + 1) // 8) * 8
    cc = jnp.zeros((rows, d), jnp.float32).at[:b].set(c).at[b].set(c_ctx)
    mod_all = modulation(cc, w_mod, b_mod).reshape(depth, rows, N_MOD, 1, d)

    w_in_r = _bf16(jnp.concatenate([w_in[..., n_mix:], w_in[..., :n_mix]], axis=-1))
    w_a, w_b, w_c, w_o = _bf16(w_br_a), _bf16(w_br_b), _bf16(w_br_c), _bf16(w_out)
    w_rt = _bf16(jnp.swapaxes(w_router, 1, 2))
    w_g, w_u, w_d = _bf16(w_e_gate), _bf16(w_e_up), _bf16(w_e_down)
    bias = jax.vmap(_natten_bias)(na_rpb)
    s0 = jnp.zeros((b, 2, HG_HEADS, HG_D, HG_D), jnp.float32)
    hg_rows = min(HG_ROWS, s)
    tm_in = min(TM_IN, s)
    tm_cmb = min(TM_COMBINE, s)

    xc = ctx
    for li in range(depth):
        last = li == depth - 1
        mod = mod_all[li, :b]
        mod_c = mod_all[li, b:b + 1]
        z = input_projection(x, mod, norm1[li], w_in_r[li], na_q_norm[li], na_k_norm[li], tm_in)
        zc = input_projection(xc.reshape(1, b * l, d), mod_c, norm1[li], w_in_r[li], na_q_norm[li], na_k_norm[li],
                              min(TM_IN, b * l)).reshape(b, l, -1)
        oc_fw, oc_bw, s_ctx = hgrn_scan(zc, g0, lb_all[li], s0, l)
        o_fw, o_bw, _ = hgrn_scan(z, g0, lb_all[li], s_ctx, hg_rows)
        y_na, yc_na = natten(z, zc, g0, bias[li])

        mw = (conv_w[li], hg_norm[li], w_a[li], w_b[li], w_c[li], w_o[li], norm2[li], w_rt[li])
        x, hp, lg = merge(z, o_fw, o_bw, y_na, x, mod, *mw, TM_MERGE)
        plan, xe = _route_and_gather(hp, lg, (b, e, cap, d // 2), lambda bi, ei: (bi, ei, 0, 0), cap, tm_cmb)
        if last:
            ye, _ = expert_ffn(xe, None, w_g[li], w_u[li], w_d[li])
        else:
            xc, hpc, lgc = merge(zc, oc_fw, oc_bw, yc_na, xc, mod_c, *mw, l)
            plan_c, xe_c = _route_and_gather(hpc, lgc, (1, e, cap, d // 2), lambda bi, ei: (0, ei, bi, 0), cap_c, l)
            ye, ye_c = expert_ffn(xe, xe_c, w_g[li], w_u[li], w_d[li])
            xc = combine(*plan_c, ye_c, xc, mod_c, l)
        x = combine(*plan, ye, x, mod, tm_cmb)
    return x
```

```python
import functools

import jax
import jax.numpy as jnp
from jax import lax
from jax.experimental import pallas as pl
from jax.experimental.pallas import tpu as pltpu

N_MOD = 6
EPS = 1e-6
MASK_VALUE = -1e30
GRID_W = 64
GROUP_W = 512
HG_HEADS = 4
HG_D = 128
HG_CHUNK = 64
HG_SUB = 8
LOG2_E = 1.4426950408889634
NA_HEADS = 8
NA_HD = 64
WIN_R = 8
WIN_C = 16
CAP_FACTOR = 2
T_CONV_B, T_CONV_C, T_CONV_U, T_HG_Q, T_HG_I, T_HG_FF, T_HG_FB, T_HG_G, T_NA_Q, T_NA_K, T_NA_V, T_GATES = range(12)
N_GATES = 3

VMEM_LIMIT_BYTES = 48 * 1024 * 1024


def _params(*semantics):
    return pltpu.CompilerParams(dimension_semantics=semantics, vmem_limit_bytes=VMEM_LIMIT_BYTES)


def _silu(v):
    return v * jax.nn.sigmoid(v)


def _bf16(v):
    return v.astype(jnp.bfloat16)


def _dot(a, b):
    return jnp.dot(a, b, preferred_element_type=jnp.float32)


def _dot_nt(a, b):
    return lax.dot_general(a, b, (((1,), (1,)), ((), ())), preferred_element_type=jnp.float32)


def _mod_kernel(c_ref, w_ref, b_ref, o_ref):
    cond = _bf16(_silu(c_ref[...]))
    o_ref[0] = _dot(cond, _bf16(w_ref[0])) + b_ref[0]


def modulation(cc, w_mod, b_mod):
    depth, d, nm = w_mod.shape
    r = cc.shape[0]
    tn = d
    return pl.pallas_call(
        _mod_kernel,
        grid=(depth, nm // tn),
        in_specs=[pl.BlockSpec((r, d), lambda l, j: (0, 0)),
                  pl.BlockSpec((1, d, tn), lambda l, j: (l, 0, j)),
                  pl.BlockSpec((1, 1, tn), lambda l, j: (l, 0, j))],
        out_specs=pl.BlockSpec((1, r, tn), lambda l, j: (l, 0, j)),
        out_shape=jax.ShapeDtypeStruct((depth, r, nm), jnp.float32),
        compiler_params=_params("parallel", "parallel"),
        name="modulation",
    )(cc, w_mod, b_mod.reshape(depth, 1, nm))


def _modnorm(x, w, shift, scale):
    ms = jnp.mean(x * x, axis=-1, keepdims=True)
    return (x * lax.rsqrt(ms + EPS) * w) * (1.0 + scale) + shift


def _group_rms(acc, gmat, w_tiled):
    ms = _dot(_bf16(acc * acc), gmat)
    return acc * lax.rsqrt(ms + EPS) * w_tiled


def _inproj_kernel(x_ref, shift_ref, scale_ref, nw_ref, w_ref, g64_ref, qn_ref, kn_ref, o_ref, h_ref):
    j = pl.program_id(2)

    @pl.when(j == 0)
    def _():
        h_ref[...] = _bf16(_modnorm(x_ref[0], nw_ref[...], shift_ref[0, 0], scale_ref[0, 0]))

    acc = lambda: _dot(h_ref[...], w_ref[...])
    plain = (j != T_HG_Q) & (j != T_NA_Q) & (j != T_NA_K)

    @pl.when(plain)
    def _():
        o_ref[0] = acc().astype(o_ref.dtype)

    @pl.when(j == T_HG_Q)
    def _():
        o_ref[0] = (_silu(acc()) * (HG_D ** -0.5)).astype(o_ref.dtype)

    @pl.when(j == T_NA_Q)
    def _():
        o_ref[0] = (_group_rms(acc(), g64_ref[...], qn_ref[...]) * (NA_HD ** -0.5)).astype(o_ref.dtype)

    @pl.when(j == T_NA_K)
    def _():
        o_ref[0] = _group_rms(acc(), g64_ref[...], kn_ref[...]).astype(o_ref.dtype)


def _block_diag_mean(n, group):
    idx = jnp.arange(n) // group
    return _bf16(jnp.where(idx[:, None] == idx[None, :], 1.0 / group, 0.0))


def input_projection(x, mod, norm_w, w_in, qn, kn, tm):
    bx, s, d = x.shape
    n = w_in.shape[1]
    tn = GROUP_W
    g64 = _block_diag_mean(tn, NA_HD)
    tile = lambda v: jnp.tile(v.reshape(1, -1), (1, tn // v.shape[-1]))
    return pl.pallas_call(
        _inproj_kernel,
        grid=(bx, s // tm, n // tn),
        in_specs=[pl.BlockSpec((1, tm, d), lambda b, i, j: (b, i, 0)),
                  pl.BlockSpec((1, 1, 1, d), lambda b, i, j: (b, 0, 0, 0)),
                  pl.BlockSpec((1, 1, 1, d), lambda b, i, j: (b, 1, 0, 0)),
                  pl.BlockSpec((1, d), lambda b, i, j: (0, 0)),
                  pl.BlockSpec((d, tn), lambda b, i, j: (0, j)),
                  pl.BlockSpec((tn, tn), lambda b, i, j: (0, 0)),
                  pl.BlockSpec((1, tn), lambda b, i, j: (0, 0)),
                  pl.BlockSpec((1, tn), lambda b, i, j: (0, 0))],
        out_specs=pl.BlockSpec((1, tm, tn), lambda b, i, j: (b, i, j)),
        out_shape=jax.ShapeDtypeStruct((bx, s, n), jnp.bfloat16),
        scratch_shapes=[pltpu.VMEM((tm, d), jnp.bfloat16)],
        compiler_params=_params("parallel", "parallel", "arbitrary"),
        name="input_projection",
    )(x, mod, mod, norm_w.reshape(1, d), w_in, g64, tile(qn), tile(kn))


def _split3(v):
    h1 = _bf16(v)
    r1 = v - h1.astype(jnp.float32)
    h2 = _bf16(r1)
    h3 = _bf16(r1 - h2.astype(jnp.float32))
    return h1, h2, h3


def _hgrn_gates(f, lb, tri):
    sig = jax.nn.sigmoid(f)
    lg = jnp.log(lb + (1.0 - lb) * sig) * LOG2_E
    k = (1.0 - lb) * (1.0 - sig)
    h1, h2, h3 = _split3(lg)
    return k, _dot(tri, h1) + _dot(tri, h2) + _dot(tri, h3)


HG_GRP = 8


def _hgrn_head_products(q, v, k, bc, st, ones, rev, k_row, bc_row):
    c = HG_CHUNK
    o_inter = _dot_nt(_bf16(q * jnp.exp2(bc)), _bf16(st))
    end = 0 if rev else c - 1
    kd = k * jnp.exp2(bc_row(end, c) - bc)
    v_t = _bf16(v.astype(jnp.float32).T)
    st_new = st * jnp.exp2(bc_row(end, st.shape[0])) + _dot(v_t, _bf16(kd))

    row = lax.broadcasted_iota(jnp.int32, (HG_GRP, HG_D), 0)
    a_off, ps, where = [], [], []
    for i in range(c // HG_SUB):
        lo, hi = i * HG_SUB, (i + 1) * HG_SUB
        qi, bi = q[lo:hi], bc[lo:hi]
        prev = None
        if not rev and i > 0:
            ref, prev = lo - 1, slice(0, lo)
        if rev and hi < c:
            ref, prev = hi, slice(hi, c)
        if prev is None:
            a_off.append(None)
        else:
            qt = _bf16(qi * jnp.exp2(bi - bc_row(ref, HG_SUB)))
            kt = k[prev] * jnp.exp2(bc_row(ref, prev.stop - prev.start) - bc[prev])
            pad = jnp.zeros((c - kt.shape[0], HG_D), kt.dtype)
            kt = _bf16(jnp.concatenate([pad, kt] if rev else [kt, pad], axis=0))
            a_off.append(_dot_nt(qt, kt))
        for s in range(lo, hi):
            for g in range(lo // HG_GRP, hi // HG_GRP):
                g_lo = g * HG_GRP
                sees = (g_lo <= s) if rev else (g_lo + HG_GRP > s)
                if not sees:
                    continue
                diff = bc[g_lo:g_lo + HG_GRP] - bc_row(s, HG_GRP)
                if g_lo <= s < g_lo + HG_GRP:
                    valid = (row <= s - g_lo) if rev else (row >= s - g_lo)
                    diff = jnp.where(valid, diff, MASK_VALUE)
                where.append((g, s))
                ps.append(q[g_lo:g_lo + HG_GRP] * k_row(s, HG_GRP) * jnp.exp2(diff))
    rsum = _dot(_bf16(jnp.concatenate(ps, axis=0)), ones)
    return o_inter, st_new, a_off, rsum, where


def _hgrn_head_output(v, o_inter, a_off, rsum, where):
    c = HG_CHUNK
    lane = lax.broadcasted_iota(jnp.int32, (HG_GRP, c), 1)
    groups = [None] * (c // HG_GRP)
    for i, a in enumerate(a_off):
        for g in range(i * HG_SUB // HG_GRP, (i + 1) * HG_SUB // HG_GRP):
            r = (g - i * HG_SUB // HG_GRP) * HG_GRP
            groups[g] = jnp.zeros((HG_GRP, c), jnp.float32) if a is None else a[r:r + HG_GRP]
    for n, (g, s) in enumerate(where):
        groups[g] = jnp.where(lane == s, rsum[n * HG_GRP:(n + 1) * HG_GRP, :c], groups[g])
    return o_inter + _dot(_bf16(jnp.concatenate(groups, axis=0)), v)


def _hgrn_kernel(qf_ref, vf_ref, ff_ref, qb_ref, vb_ref, fb_ref, lb_ref, s0_ref, tri_ref, ones_ref,
                 of_ref, ob_ref, sfin_ref, st_ref, kb_ref, bb_ref, *, n_chunks):
    cb = pl.program_id(1)

    @pl.when(cb == 0)
    def _():
        st_ref[...] = s0_ref[0]

    ones = ones_ref[...]
    heads = [slice(h * HG_D, (h + 1) * HG_D) for h in range(HG_HEADS)]
    streams = ((qf_ref, vf_ref, ff_ref, of_ref, False), (qb_ref, vb_ref, fb_ref, ob_ref, True))

    def chunk(it, carry):
        loaded = []
        for di, (q_ref, v_ref, f_ref, _, rev) in enumerate(streams):
            r0 = pl.multiple_of(((n_chunks - 1 - it) if rev else it) * HG_CHUNK, HG_CHUNK)
            q = q_ref[0, pl.ds(r0, HG_CHUNK), :].astype(jnp.float32)
            v = v_ref[0, pl.ds(r0, HG_CHUNK), :]
            k, bc = _hgrn_gates(f_ref[0, pl.ds(r0, HG_CHUNK), :].astype(jnp.float32), lb_ref[di:di + 1], tri_ref[di])
            kb_ref[di] = k
            bb_ref[di] = bc
            loaded.append((r0, q, v, k, bc))

        def repeat_row(ref, di, hs):
            return lambda s, n: jnp.broadcast_to(ref[di, s:s + 1, hs], (n, HG_D))

        stage = [[_hgrn_head_products(q[:, hs], v[:, hs], k[:, hs], bc[:, hs], st_ref[di, h], ones, streams[di][4],
                                      repeat_row(kb_ref, di, hs), repeat_row(bb_ref, di, hs))
                  for h, hs in enumerate(heads)] for di, (_, q, v, k, bc) in enumerate(loaded)]
        for di, (r0, _, v, _, _) in enumerate(loaded):
            outs = []
            for h, (hs, (o_inter, st_new, a_off, rsum, where)) in enumerate(zip(heads, stage[di])):
                st_ref[di, h] = st_new
                outs.append(_hgrn_head_output(v[:, hs], o_inter, a_off, rsum, where))
            streams[di][3][0, pl.ds(r0, HG_CHUNK), :] = jnp.concatenate(outs, axis=1)
        return carry

    lax.fori_loop(0, n_chunks, chunk, 0)

    @pl.when(cb == pl.num_programs(1) - 1)
    def _():
        sfin_ref[0] = st_ref[...]


def hgrn_scan(z, lb, s0, tc):
    b, l, _ = z.shape
    nb = l // tc
    ci = jnp.arange(HG_CHUNK)
    tri = _bf16(jnp.stack([ci[None, :] <= ci[:, None], ci[None, :] >= ci[:, None]]))
    ones = jnp.ones((HG_D, HG_D), jnp.bfloat16)
    fwd = lambda t: pl.BlockSpec((1, tc, GROUP_W), lambda bi, c: (bi, c, t))
    bwd = lambda t: pl.BlockSpec((1, tc, GROUP_W), lambda bi, c: (bi, nb - 1 - c, t))
    st_spec = pl.BlockSpec((1, 2, HG_HEADS, HG_D, HG_D), lambda bi, c: (bi, 0, 0, 0, 0))
    return pl.pallas_call(
        functools.partial(_hgrn_kernel, n_chunks=tc // HG_CHUNK),
        grid=(b, nb),
        in_specs=[fwd(T_HG_Q), fwd(T_HG_I), fwd(T_HG_FF), bwd(T_HG_Q), bwd(T_HG_I), bwd(T_HG_FB),
                  pl.BlockSpec((2, GROUP_W), lambda bi, c: (0, 0)),
                  st_spec,
                  pl.BlockSpec((2, HG_CHUNK, HG_CHUNK), lambda bi, c: (0, 0, 0)),
                  pl.BlockSpec((HG_D, HG_D), lambda bi, c: (0, 0))],
        out_specs=[pl.BlockSpec((1, tc, GROUP_W), lambda bi, c: (bi, c, 0)),
                   pl.BlockSpec((1, tc, GROUP_W), lambda bi, c: (bi, nb - 1 - c, 0)), st_spec],
        out_shape=[jax.ShapeDtypeStruct((b, l, GROUP_W), jnp.float32),
                   jax.ShapeDtypeStruct((b, l, GROUP_W), jnp.float32),
                   jax.ShapeDtypeStruct(s0.shape, jnp.float32)],
        scratch_shapes=[pltpu.VMEM((2, HG_HEADS, HG_D, HG_D), jnp.float32),
                        pltpu.VMEM((2, HG_CHUNK, GROUP_W), jnp.float32), pltpu.VMEM((2, HG_CHUNK, GROUP_W), jnp.float32)],
        compiler_params=_params("parallel", "arbitrary"),
        name="hgrn_scan",
    )(z, z, z, z, z, z, lb, s0, tri, ones)


NA_ROWS = 4


def _softmax_pv(scores, values):
    m = functools.reduce(jnp.maximum, [jnp.max(s, axis=-1, keepdims=True) for s in scores])
    ps = [jnp.exp(s - m) for s in scores]
    den = functools.reduce(jnp.add, [jnp.sum(p, axis=-1, keepdims=True) for p in ps])
    num = functools.reduce(jnp.add, [_dot(_bf16(p), v) for p, v in zip(ps, values)])
    return num / den


def _natten_kernel(q_ref, k_ref, v_ref, qc_ref, kc_ref, vc_ref, bias_ref, o_ref, oc_ref, *, rows):
    lane = lax.broadcasted_iota(jnp.int32, (1, 2 * NA_HD), 1)
    first = lane < NA_HD
    kc = kc_ref[0]
    vc = vc_ref[0]
    zero = jnp.zeros((), q_ref.dtype)

    def stack(q):
        return jnp.concatenate([jnp.where(first, q, zero), jnp.where(first, zero, q)], axis=0)

    def unstack(o):
        n = o.shape[0] // 2
        return jnp.where(first, o[:n], o[n:])

    oc_ref[0] = unstack(_softmax_pv([_dot_nt(stack(qc_ref[0]), kc)], [vc])).astype(oc_ref.dtype)

    def row_group(it, carry):
        work = []
        for j in range(NA_ROWS):
            r = it * NA_ROWS + j
            r0 = jnp.clip(r - WIN_R // 2, 0, rows - WIN_R)
            delta = r0 - r + WIN_R - 1
            qs = pl.multiple_of(r * GRID_W, GRID_W)
            ks = pl.multiple_of(r0 * GRID_W, GRID_W)
            q2 = stack(q_ref[0, pl.ds(qs, GRID_W), :])
            kl = k_ref[0, pl.ds(ks, WIN_R * GRID_W), :]
            vl = v_ref[0, pl.ds(ks, WIN_R * GRID_W), :]
            work.append((qs, vl, _dot_nt(q2, kl) + bias_ref[0, delta], _dot_nt(q2, kc)))
        outs = [unstack(_softmax_pv([s_loc, s_ctx], [vl, vc])) for _, vl, s_loc, s_ctx in work]
        for (qs, _, _, _), o in zip(work, outs):
            o_ref[0, pl.ds(qs, GRID_W), :] = o.astype(o_ref.dtype)
        return carry

    lax.fori_loop(0, rows // NA_ROWS, row_group, 0)


def _natten_bias(rpb):
    qcol = jnp.arange(GRID_W)[:, None]
    kcol = jnp.arange(GRID_W)[None, :]
    wstart = jnp.clip(qcol - WIN_C // 2, 0, GRID_W - WIN_C)
    in_win = (kcol >= wstart) & (kcol < wstart + WIN_C)
    dc = jnp.clip(kcol - qcol + WIN_C - 1, 0, 2 * WIN_C - 2)
    dr = jnp.arange(WIN_R)[:, None] + jnp.arange(WIN_R)[None, :]
    t = rpb[:, :, dc][:, dr]
    t = jnp.where(in_win[None, None, None], t.astype(jnp.float32), MASK_VALUE)
    t = t.transpose(0, 1, 3, 2, 4).reshape(NA_HEADS // 2, 2, WIN_R, GRID_W, WIN_R * GRID_W)
    return t.transpose(0, 2, 1, 3, 4).reshape(NA_HEADS // 2, WIN_R, 2 * GRID_W, WIN_R * GRID_W)


def natten(z, zc, bias):
    b, s, _ = z.shape
    l = zc.shape[1]
    rows = s // GRID_W
    assert rows >= WIN_R
    pairs = NA_HEADS // 2
    per_tile = GROUP_W // (2 * NA_HD)
    spec = lambda n, t: pl.BlockSpec((1, n, 2 * NA_HD), lambda bi, p: (bi, 0, t * per_tile + p))
    return pl.pallas_call(
        functools.partial(_natten_kernel, rows=rows),
        grid=(b, pairs),
        in_specs=[spec(s, T_NA_Q), spec(s, T_NA_K), spec(s, T_NA_V),
                  spec(l, T_NA_Q), spec(l, T_NA_K), spec(l, T_NA_V),
                  pl.BlockSpec((1, WIN_R, 2 * GRID_W, WIN_R * GRID_W), lambda bi, p: (p, 0, 0, 0))],
        out_specs=[pl.BlockSpec((1, s, 2 * NA_HD), lambda bi, p: (bi, 0, p)),
                   pl.BlockSpec((1, l, 2 * NA_HD), lambda bi, p: (bi, 0, p))],
        out_shape=[jax.ShapeDtypeStruct((b, s, GROUP_W), jnp.bfloat16),
                   jax.ShapeDtypeStruct((b, l, GROUP_W), jnp.bfloat16)],
        compiler_params=_params("parallel", "parallel"),
        name="natten",
    )(z, z, z, zc, zc, zc, bias)


HALO = 16


def _pack_halves(h):
    half = h.shape[1] // 2
    lo = pltpu.bitcast(_bf16(h[:, :half]).astype(jnp.float32), jnp.uint32)
    hi = pltpu.bitcast(_bf16(h[:, half:]).astype(jnp.float32), jnp.uint32)
    return (lo >> 16) | (hi & jnp.uint32(0xFFFF0000))


def _unpack_halves(w):
    lo = pltpu.bitcast(w << 16, jnp.float32)
    hi = pltpu.bitcast(w & jnp.uint32(0xFFFF0000), jnp.float32)
    return _bf16(jnp.concatenate([lo, hi], axis=1))


def _sigmoid(v):
    return 0.5 * jnp.tanh(0.5 * v) + 0.5


def _merge_kernel(zb_ref, zc_ref, zu_ref, pc_ref, pu_ref, nc_ref, nu_ref, cw_ref,
                  ofw_ref, obw_ref, zg_ref, hgn_ref, g128_ref, yna_ref, *rest):
    n_gate = len(rest) - 13
    gate_refs, rest = rest[:n_gate], rest[n_gate:]
    (x_ref, gate_ref, wa_ref, wb_ref, wc_ref, wo_ref, n2_ref, shift_ref, scale_ref, wr_ref,
     xo_ref, hp_ref, lg_ref) = rest
    per_gate = n_gate // N_GATES
    gate = lambda k: _sigmoid(jnp.concatenate(
        [r[0] for r in gate_refs[k * per_gate:(k + 1) * per_gate]], axis=1).astype(jnp.float32))
    i = pl.program_id(1)
    tm = zb_ref.shape[1]
    f32 = jnp.float32

    v = zc_ref[0].astype(f32) * zu_ref[0].astype(f32)
    vp = pc_ref[0, HALO - 1:HALO].astype(f32) * pu_ref[0, HALO - 1:HALO].astype(f32)
    vn = nc_ref[0, 0:1].astype(f32) * nu_ref[0, 0:1].astype(f32)
    vp = jnp.where(i == 0, 0.0, vp)
    vn = jnp.where(i == pl.num_programs(1) - 1, 0.0, vn)
    row = lax.broadcasted_iota(jnp.int32, v.shape, 0)
    v_prev = jnp.where(row == 0, vp, pltpu.roll(v, 1, 0))
    v_next = jnp.where(row == tm - 1, vn, pltpu.roll(v, tm - 1, 0))
    cw = cw_ref[...]
    y_cv = zb_ref[0].astype(f32) * (cw[0:1] * v_prev + cw[1:2] * v + cw[2:3] * v_next)

    o = ofw_ref[0] + obw_ref[0]
    y_hg = _group_rms(o, g128_ref[...], hgn_ref[...]) * _silu(zg_ref[0].astype(f32))

    m = (gate(0) * _dot(_bf16(y_cv), wa_ref[...]) + gate(1) * _dot(_bf16(y_hg), wb_ref[...])
         + gate(2) * _dot(yna_ref[0], wc_ref[...]))
    x_new = x_ref[0] + gate_ref[0, 0] * _dot(_bf16(m), wo_ref[...])
    xo_ref[0] = x_new

    h2 = _modnorm(x_new, n2_ref[...], shift_ref[0, 0], scale_ref[0, 0])
    hp_ref[0] = _pack_halves(h2)
    lg_ref[0] = _dot_nt(wr_ref[...], _bf16(h2))


def merge(z, o_fw, o_bw, y_na, x, mod, conv_w, hg_norm, w_a, w_b, w_c, w_o, norm2, w_rt, tm):
    b, l, d = x.shape
    e = w_rt.shape[0]
    nt = l // tm
    per = tm // HALO
    n_halo = l // HALO
    assert d % GROUP_W == 0
    n_gate = N_GATES * d // GROUP_W
    bm = (lambda bi: bi) if mod.shape[0] == b else (lambda bi: 0)
    zt = lambda t: pl.BlockSpec((1, tm, GROUP_W), lambda bi, i: (bi, i, t))
    zprev = lambda t: pl.BlockSpec((1, HALO, GROUP_W), lambda bi, i: (bi, jnp.maximum(i * per - 1, 0), t))
    znext = lambda t: pl.BlockSpec((1, HALO, GROUP_W),
                                   lambda bi, i: (bi, jnp.minimum((i + 1) * per, n_halo - 1), t))
    act = lambda w: pl.BlockSpec((1, tm, w), lambda bi, i: (bi, i, 0))
    modrow = lambda k: pl.BlockSpec((1, 1, 1, d), lambda bi, i: (bm(bi), k, 0, 0))
    full = lambda a: pl.BlockSpec(a.shape, lambda bi, i: (0,) * a.ndim)
    g128 = _block_diag_mean(GROUP_W, HG_D)
    hgn = jnp.tile(hg_norm.reshape(1, -1), (1, GROUP_W // HG_D))
    n2 = norm2.reshape(1, d)
    return pl.pallas_call(
        _merge_kernel,
        grid=(b, nt),
        in_specs=[zt(T_CONV_B), zt(T_CONV_C), zt(T_CONV_U),
                  zprev(T_CONV_C), zprev(T_CONV_U), znext(T_CONV_C), znext(T_CONV_U), full(conv_w),
                  act(GROUP_W), act(GROUP_W), zt(T_HG_G), full(hgn), full(g128), act(GROUP_W)]
                 + [zt(T_GATES + k) for k in range(n_gate)]
                 + [act(d), modrow(2), full(w_a), full(w_b), full(w_c), full(w_o),
                  full(n2), modrow(3), modrow(4), full(w_rt)],
        out_specs=[act(d), act(d // 2), pl.BlockSpec((1, e, tm), lambda bi, i: (bi, 0, i))],
        out_shape=[jax.ShapeDtypeStruct((b, l, d), jnp.float32),
                   jax.ShapeDtypeStruct((b, l, d // 2), jnp.uint32),
                   jax.ShapeDtypeStruct((b, e, l), jnp.float32)],
        compiler_params=_params("parallel", "arbitrary"),
        name="merge",
    )(z, z, z, z, z, z, z, conv_w, o_fw, o_bw, z, hgn, g128, y_na, *([z] * n_gate),
      x, mod, w_a, w_b, w_c, w_o, n2, mod, mod, w_rt)


LANES = 128
ROUTE_K_CHUNK = 1024


def _count(mask):
    return jnp.sum(jnp.where(mask, 1.0, 0.0), axis=1, keepdims=True)


def _route_kernel(lg_ref, tmat_ref, excl_ref, slot_ref, aff_ref, idx_ref, starts_ref, *, cap, tile):
    f32 = jnp.float32
    lg = lg_ref[0]
    e, s = lg.shape
    ex = jnp.exp(lg - jnp.max(lg, axis=0, keepdims=True))
    aff = ex / jnp.sum(ex, axis=0, keepdims=True)
    aff_ref[0] = aff
    bits = pltpu.bitcast(aff, jnp.int32)

    def thr_bit(it, thr):
        cand = thr | (jnp.int32(1) << (30 - it))
        return jnp.where(_count(bits >= cand) >= cap, cand, thr)

    thr = lax.fori_loop(0, 31, thr_bit, jnp.zeros((e, 1), jnp.int32))
    gt = bits > thr
    eq = bits == thr
    need = cap - _count(gt)
    tok = lax.broadcasted_iota(jnp.int32, (e, s), 1)
    nbits = s.bit_length()

    def end_bit(it, end):
        cand = end + (jnp.int32(1) << (nbits - 1 - it))
        ok = (cand <= s) & (_count(eq & (tok < cand)) <= need)
        return jnp.where(ok, cand, end)

    end = lax.fori_loop(0, nbits, end_bit, jnp.zeros((e, 1), jnp.int32))
    sel = gt | (eq & (tok < end))

    self = jnp.where(sel, 1.0, 0.0)
    offs = jnp.zeros((e, 1), f32)
    pieces = []
    tile_lane = lax.broadcasted_iota(jnp.int32, (e, LANES), 1)
    starts = jnp.zeros((e, LANES), f32)
    for c in range(s // LANES):
        if (c * LANES) % tile == 0:
            starts = jnp.where(tile_lane == c * LANES // tile, offs, starts)
        blk = self[:, c * LANES:(c + 1) * LANES]
        pieces.append(_dot(_bf16(blk), excl_ref[...]) + offs)
        offs = offs + jnp.sum(blk, axis=1, keepdims=True)
    starts_ref[0] = jnp.where(tile_lane == s // tile, offs, starts).astype(jnp.int32)
    slot_ref[0] = jnp.where(sel, jnp.concatenate(pieces, axis=1), -1.0)

    piota = lax.broadcasted_iota(jnp.int32, (cap, 1), 0).astype(f32)
    lane = lax.broadcasted_iota(jnp.int32, (cap, LANES), 1)
    kc = min(ROUTE_K_CHUNK, s)
    idx_ref[0] = jnp.zeros((cap, LANES), jnp.int32)

    def expert(ee, carry):
        res = jnp.zeros((cap, LANES), f32)
        for c in range(s // kc):
            row = slot_ref[0, pl.ds(ee, 1), c * kc:(c + 1) * kc]
            onehot = _bf16(jnp.where(row == piota, 1.0, 0.0))
            res = res + _dot(onehot, tmat_ref[c * kc:(c + 1) * kc, :])
        tok_idx = (res[:, 0:1] * 64.0 + res[:, 1:2]).astype(jnp.int32)
        idx_ref[0] = jnp.where(lane == ee, tok_idx, idx_ref[0])
        return carry

    lax.fori_loop(0, e, expert, 0)


def route(lg, cap, tile):
    b, e, s = lg.shape
    assert tile % LANES == 0 and s % tile == 0 and s // tile < LANES
    t = jnp.arange(s)
    tmat = jnp.zeros((s, LANES), jnp.bfloat16).at[:, 0].set(_bf16(t // 64)).at[:, 1].set(_bf16(t % 64))
    a = jnp.arange(LANES)
    excl = _bf16(a[:, None] < a[None, :])
    spec = pl.BlockSpec((1, e, s), lambda bi: (bi, 0, 0))
    slot, aff, idx, starts = pl.pallas_call(
        functools.partial(_route_kernel, cap=cap, tile=tile),
        grid=(b,),
        in_specs=[spec, pl.BlockSpec((s, LANES), lambda bi: (0, 0)), pl.BlockSpec((LANES, LANES), lambda bi: (0, 0))],
        out_specs=[spec, spec, pl.BlockSpec((1, cap, LANES), lambda bi: (bi, 0, 0)),
                   pl.BlockSpec((1, e, LANES), lambda bi: (bi, 0, 0))],
        out_shape=[jax.ShapeDtypeStruct((b, e, s), jnp.float32), jax.ShapeDtypeStruct((b, e, s), jnp.float32),
                   jax.ShapeDtypeStruct((b, cap, LANES), jnp.int32), jax.ShapeDtypeStruct((b, e, LANES), jnp.int32)],
        compiler_params=_params("parallel"),
        name="route",
    )(lg, tmat, excl)
    return slot, aff, idx[:, :, :e].transpose(0, 2, 1), starts[:, :, :s // tile + 1]


def _gather_kernel(idx_ref, h_ref, o_ref, *, cap):
    def body(j, carry):
        t = idx_ref[0, 0, j]
        o_ref[0, 0, pl.ds(j, 1), :] = h_ref[0, pl.ds(t, 1), :]
        return carry

    lax.fori_loop(0, cap, body, 0, unroll=8)


def gather(idx, hp, out_shape, out_index):
    b, e, cap = idx.shape
    _, s, w = hp.shape
    return pl.pallas_call(
        functools.partial(_gather_kernel, cap=cap),
        grid=(b, e),
        in_specs=[pl.BlockSpec((1, 1, cap), lambda bi, ei: (bi * e + ei, 0, 0), memory_space=pltpu.SMEM),
                  pl.BlockSpec((1, s, w), lambda bi, ei: (bi, 0, 0))],
        out_specs=pl.BlockSpec((1, 1, cap, w), out_index),
        out_shape=jax.ShapeDtypeStruct(out_shape, jnp.uint32),
        compiler_params=_params("parallel", "arbitrary"),
        name="gather",
    )(idx.reshape(b * e, 1, cap), hp)


FFN_CHUNK = 512


def _ffn_body(xw, wg_ref, wu_ref, wd_ref):
    x = _unpack_halves(xw)
    f = wg_ref.shape[2]
    fc = min(FFN_CHUNK, f)
    acc = jnp.zeros((x.shape[0], wd_ref.shape[2]), jnp.float32)
    for c in range(f // fc):
        a = _dot(x, wg_ref[0, :, c * fc:(c + 1) * fc])
        u = _dot(x, wu_ref[0, :, c * fc:(c + 1) * fc])
        acc = acc + _dot(_bf16(_silu(a) * u), wd_ref[0, c * fc:(c + 1) * fc, :])
    return _bf16(acc)


def _ffn_kernel(xl_ref, wg_ref, wu_ref, wd_ref, yl_ref):
    yl_ref[0, 0] = _ffn_body(xl_ref[0, 0], wg_ref, wu_ref, wd_ref)


def _ffn_ctx_kernel(xl_ref, xc_ref, wg_ref, wu_ref, wd_ref, yl_ref, yc_ref, *, nb):
    is_ctx = pl.program_id(1) == nb
    y = _ffn_body(jnp.where(is_ctx, xc_ref[0, 0], xl_ref[0, 0]), wg_ref, wu_ref, wd_ref)

    @pl.when(jnp.logical_not(is_ctx))
    def _():
        yl_ref[0, 0] = y

    @pl.when(is_ctx)
    def _():
        yc_ref[0, 0] = y


def expert_ffn(xe, xe_c, w_g, w_u, w_d):
    nb, e, cap, w = xe.shape
    d, f = w_g.shape[1:]
    wspec = lambda shape: pl.BlockSpec((1,) + shape, lambda ei, bi: (ei, 0, 0))
    weights = [wspec((d, f)), wspec((d, f)), wspec((f, d))]
    lat = lambda width: pl.BlockSpec((1, 1, cap, width), lambda ei, bi: (jnp.minimum(bi, nb - 1), ei, 0, 0))
    cspec = lambda width: pl.BlockSpec((1, 1, cap, width), lambda ei, bi: (0, ei, 0, 0))
    if xe_c is None:
        return pl.pallas_call(
            _ffn_kernel, grid=(e, nb),
            in_specs=[lat(w)] + weights, out_specs=lat(d),
            out_shape=jax.ShapeDtypeStruct((nb, e, cap, d), jnp.bfloat16),
            compiler_params=_params("parallel", "arbitrary"), name="expert_ffn",
        )(xe, w_g, w_u, w_d), None
    assert xe_c.shape == (1, e, cap, w)
    return pl.pallas_call(
        functools.partial(_ffn_ctx_kernel, nb=nb), grid=(e, nb + 1),
        in_specs=[lat(w), cspec(w)] + weights, out_specs=[lat(d), cspec(d)],
        out_shape=[jax.ShapeDtypeStruct((nb, e, cap, d), jnp.bfloat16),
                   jax.ShapeDtypeStruct((1, e, cap, d), jnp.bfloat16)],
        compiler_params=_params("parallel", "arbitrary"), name="expert_ffn_ctx",
    )(xe, xe_c, w_g, w_u, w_d)


SLOT_SPLIT = 16
COMBINE_WIN = 256


def _combine_kernel(starts_ref, slot_ref, aff_ref, ye_ref, x_ref, gate_ref, o_ref, *, slots_per_sample):
    f32 = jnp.float32
    bi, ti = pl.program_id(0), pl.program_id(1)
    n_exp, n_slots, d = ye_ref.shape[1:]
    win = min(COMBINE_WIN, n_slots)
    base = bi * slots_per_sample
    slot_t = slot_ref[0]
    tm = slot_t.shape[0]
    aff_t = aff_ref[0]
    slot_t = jnp.where(slot_t < 0.0, -1.0, slot_t + base.astype(f32))
    expert_lane = lax.broadcasted_iota(jnp.int32, slot_t.shape, 1)
    lane = lax.broadcasted_iota(jnp.int32, (tm, win), 1).astype(f32)

    def column(tile, e):
        col = jnp.sum(jnp.where(expert_lane == e, tile, 0.0), axis=1, keepdims=True)
        return jnp.broadcast_to(col, (tm, LANES))
    widen = lambda v, width: jnp.concatenate([v] * (width // LANES), axis=1)

    def first_window(e):
        lo = base + starts_ref[bi, e, ti]
        return jnp.minimum(lo // LANES * LANES, n_slots - win)

    def pick(slot_b, e, nominal):
        start = pl.multiple_of(jnp.minimum(nominal, n_slots - win), LANES)
        slot_w = widen(slot_b, win)
        hit = (slot_w - start.astype(f32) == lane) & (slot_w >= nominal.astype(f32))
        return _dot(_bf16(jnp.where(hit, 1.0, 0.0)), ye_ref[0, e, pl.ds(start, win), :])

    for e in range(n_exp):
        term = widen(column(aff_t, e), d) * pick(column(slot_t, e), e, first_window(e))
        if e == 0:
            o_ref[0] = term
        else:
            o_ref[0] += term

    def expert(e, carry):
        ws = first_window(e)
        hi = base + starts_ref[bi, e, ti + 1]

        def window(k, c2):
            o_ref[0] += widen(column(aff_t, e), d) * pick(column(slot_t, e), e, ws + k * win)
            return c2

        return lax.fori_loop(1, (hi - ws + win - 1) // win, window, carry)

    lax.fori_loop(0, n_exp, expert, 0)
    o_ref[0] = x_ref[0] + gate_ref[0, 0] * o_ref[0]


def combine(starts, slot_t, aff_t, ye, x, mod, tm):
    b, l, d = x.shape
    e = slot_t.shape[2]
    pooled = ye.shape[0] == 1 and b > 1
    bm = (lambda bi: bi) if mod.shape[0] == b else (lambda bi: 0)
    tok = lambda w: pl.BlockSpec((1, tm, w), lambda bi, i, st: (bi, i, 0))
    ye_spec = pl.BlockSpec((1,) + ye.shape[1:], lambda bi, i, st: (0 if pooled else bi, 0, 0, 0),
                           pipeline_mode=pl.Buffered(1))
    return pl.pallas_call(
        functools.partial(_combine_kernel, slots_per_sample=ye.shape[2] // b if pooled else 0),
        grid_spec=pltpu.PrefetchScalarGridSpec(
            num_scalar_prefetch=1,
            grid=(b, l // tm),
            in_specs=[tok(e), tok(e), ye_spec, tok(d),
                      pl.BlockSpec((1, 1, 1, d), lambda bi, i, st: (bm(bi), N_MOD - 1, 0, 0))],
            out_specs=tok(d)),
        out_shape=jax.ShapeDtypeStruct((b, l, d), jnp.float32),
        compiler_params=_params("parallel", "arbitrary"),
        name="combine",
    )(starts, slot_t, aff_t, ye, x, mod)


TM_IN = 2048
TM_MERGE = 256
TM_COMBINE = 512
HG_ROWS = 512


def _route_and_gather(hp, lg, xe_shape, xe_index, cap, tile):
    slot, aff, idx, starts = route(lg, cap, tile)
    xe = gather(idx, hp, xe_shape, xe_index)
    return (starts, slot.transpose(0, 2, 1), aff.transpose(0, 2, 1)), xe


def kernel(x, c, ctx, c_ctx, w_mod, b_mod, norm1, w_in, conv_w, hg_lb_logits, hg_norm, na_q_norm, na_k_norm, na_rpb,
           w_br_a, w_br_b, w_br_c, w_out, norm2, w_router, w_e_gate, w_e_up, w_e_down):
    b, s, d = x.shape
    l = ctx.shape[1]
    depth = w_mod.shape[0]
    e = w_router.shape[-1]
    cap = CAP_FACTOR * s // e
    cap_c = CAP_FACTOR * l // e
    assert b * cap_c == cap, "context rows of all samples fill one expert tile"

    lb_sm = jax.nn.softmax(hg_lb_logits.astype(jnp.float32), axis=0)
    lb_all = jnp.cumsum(lb_sm, axis=0) - lb_sm[0]
    rows = -(-(b + 1) // 8) * 8
    cc = jnp.zeros((rows, d), jnp.float32).at[:b].set(c).at[b].set(c_ctx)
    mod_all = modulation(cc, w_mod, b_mod).reshape(depth, rows, N_MOD, 1, d)

    w_in_r = _bf16(w_in)
    w_a, w_b, w_c, w_o = _bf16(w_br_a), _bf16(w_br_b), _bf16(w_br_c), _bf16(w_out)
    w_rt = _bf16(jnp.swapaxes(w_router, 1, 2))
    w_g, w_u, w_d = _bf16(w_e_gate), _bf16(w_e_up), _bf16(w_e_down)
    bias = jax.vmap(_natten_bias)(na_rpb)
    s0 = jnp.zeros((b, 2, HG_HEADS, HG_D, HG_D), jnp.float32)
    hg_rows = min(HG_ROWS, s)
    tm_in = min(TM_IN, s)
    tm_cmb = min(TM_COMBINE, s)

    xc = ctx
    for li in range(depth):
        last = li == depth - 1
        mod = mod_all[li, :b]
        mod_c = mod_all[li, b:b + 1]
        z = input_projection(x, mod, norm1[li], w_in_r[li], na_q_norm[li], na_k_norm[li], tm_in)
        zc = input_projection(xc.reshape(1, b * l, d), mod_c, norm1[li], w_in_r[li], na_q_norm[li], na_k_norm[li],
                              min(TM_IN, b * l)).reshape(b, l, -1)
        oc_fw, oc_bw, s_ctx = hgrn_scan(zc, lb_all[li], s0, l)
        o_fw, o_bw, _ = hgrn_scan(z, lb_all[li], s_ctx, hg_rows)
        y_na, yc_na = natten(z, zc, bias[li])

        mw = (conv_w[li], hg_norm[li], w_a[li], w_b[li], w_c[li], w_o[li], norm2[li], w_rt[li])
        x, hp, lg = merge(z, o_fw, o_bw, y_na, x, mod, *mw, TM_MERGE)
        plan, xe = _route_and_gather(hp, lg, (b, e, cap, d // 2), lambda bi, ei: (bi, ei, 0, 0), cap, tm_cmb)
        if last:
            ye, _ = expert_ffn(xe, None, w_g[li], w_u[li], w_d[li])
        else:
            xc, hpc, lgc = merge(zc, oc_fw, oc_bw, yc_na, xc, mod_c, *mw, l)
            plan_c, xe_c = _route_and_gather(hpc, lgc, (1, e, cap, d // 2), lambda bi, ei: (0, ei, bi, 0), cap_c, l)
            ye, ye_c = expert_ffn(xe, xe_c, w_g[li], w_u[li], w_d[li])
            xc = combine(*plan_c, ye_c, xc, mod_c, l)
        x = combine(*plan, ye, x, mod, tm_cmb)
    return x
```

```python
import functools

import jax
import jax.numpy as jnp
from jax import lax
from jax.experimental import pallas as pl
from jax.experimental.pallas import tpu as pltpu

N_MOD = 6
EPS = 1e-6
MASK_VALUE = -1e30
GRID_W = 64
GROUP_W = 512
HG_HEADS = 4
HG_D = 128
HG_CHUNK = 64
HG_SUB = 8
LOG2_E = 1.4426950408889634
NA_HEADS = 8
NA_HD = 64
WIN_R = 8
WIN_C = 16
CAP_FACTOR = 2
T_CONV_B, T_CONV_C, T_CONV_U, T_HG_Q, T_HG_I, T_HG_FF, T_HG_FB, T_HG_G, T_NA_Q, T_NA_K, T_NA_V, T_GATES = range(12)
N_GATES = 3

VMEM_LIMIT_BYTES = 48 * 1024 * 1024


def _params(*semantics):
    return pltpu.CompilerParams(dimension_semantics=semantics, vmem_limit_bytes=VMEM_LIMIT_BYTES)


def _silu(v):
    return v * jax.nn.sigmoid(v)


def _bf16(v):
    return v.astype(jnp.bfloat16)


def _dot(a, b):
    return jnp.dot(a, b, preferred_element_type=jnp.float32)


def _dot_nt(a, b):
    return lax.dot_general(a, b, (((1,), (1,)), ((), ())), preferred_element_type=jnp.float32)


def _mod_kernel(c_ref, w_ref, b_ref, o_ref):
    cond = _bf16(_silu(c_ref[...]))
    o_ref[0] = _dot(cond, _bf16(w_ref[0])) + b_ref[0]


def modulation(cc, w_mod, b_mod):
    depth, d, nm = w_mod.shape
    r = cc.shape[0]
    tn = d
    return pl.pallas_call(
        _mod_kernel,
        grid=(depth, nm // tn),
        in_specs=[pl.BlockSpec((r, d), lambda l, j: (0, 0)),
                  pl.BlockSpec((1, d, tn), lambda l, j: (l, 0, j)),
                  pl.BlockSpec((1, 1, tn), lambda l, j: (l, 0, j))],
        out_specs=pl.BlockSpec((1, r, tn), lambda l, j: (l, 0, j)),
        out_shape=jax.ShapeDtypeStruct((depth, r, nm), jnp.float32),
        compiler_params=_params("parallel", "parallel"),
        name="modulation",
    )(cc, w_mod, b_mod.reshape(depth, 1, nm))


def _modnorm(x, w, shift, scale):
    ms = jnp.mean(x * x, axis=-1, keepdims=True)
    return (x * lax.rsqrt(ms + EPS) * w) * (1.0 + scale) + shift


def _group_rms(acc, gmat, w_tiled):
    ms = _dot(_bf16(acc * acc), gmat)
    return acc * lax.rsqrt(ms + EPS) * w_tiled


def _inproj_kernel(x_ref, shift_ref, scale_ref, nw_ref, w_ref, g64_ref, qn_ref, kn_ref, o_ref, h_ref):
    j = pl.program_id(2)

    @pl.when(j == 0)
    def _():
        h_ref[...] = _bf16(_modnorm(x_ref[0], nw_ref[...], shift_ref[0, 0], scale_ref[0, 0]))

    acc = lambda: _dot(h_ref[...], w_ref[0])
    plain = (j != T_HG_Q) & (j != T_NA_Q) & (j != T_NA_K)

    @pl.when(plain)
    def _():
        o_ref[0] = acc().astype(o_ref.dtype)

    @pl.when(j == T_HG_Q)
    def _():
        o_ref[0] = (_silu(acc()) * (HG_D ** -0.5)).astype(o_ref.dtype)

    @pl.when(j == T_NA_Q)
    def _():
        o_ref[0] = (_group_rms(acc(), g64_ref[...], qn_ref[...]) * (NA_HD ** -0.5)).astype(o_ref.dtype)

    @pl.when(j == T_NA_K)
    def _():
        o_ref[0] = _group_rms(acc(), g64_ref[...], kn_ref[...]).astype(o_ref.dtype)


def _block_diag_mean(n, group):
    idx = jnp.arange(n) // group
    return _bf16(jnp.where(idx[:, None] == idx[None, :], 1.0 / group, 0.0))


def input_projection(x, mod, norm_w, w_in, li, qn, kn, tm):
    bx, s, d = x.shape
    n = w_in.shape[2]
    tn = GROUP_W
    g64 = _block_diag_mean(tn, NA_HD)
    tile = lambda v: jnp.tile(v.reshape(1, -1), (1, tn // v.shape[-1]))
    return pl.pallas_call(
        _inproj_kernel,
        grid=(bx, s // tm, n // tn),
        in_specs=[pl.BlockSpec((1, tm, d), lambda b, i, j: (b, i, 0)),
                  pl.BlockSpec((1, 1, 1, d), lambda b, i, j: (b, 0, 0, 0)),
                  pl.BlockSpec((1, 1, 1, d), lambda b, i, j: (b, 1, 0, 0)),
                  pl.BlockSpec((1, d), lambda b, i, j: (0, 0)),
                  pl.BlockSpec((1, d, tn), lambda b, i, j: (li, 0, j)),
                  pl.BlockSpec((tn, tn), lambda b, i, j: (0, 0)),
                  pl.BlockSpec((1, tn), lambda b, i, j: (0, 0)),
                  pl.BlockSpec((1, tn), lambda b, i, j: (0, 0))],
        out_specs=pl.BlockSpec((1, tm, tn), lambda b, i, j: (b, i, j)),
        out_shape=jax.ShapeDtypeStruct((bx, s, n), jnp.bfloat16),
        scratch_shapes=[pltpu.VMEM((tm, d), jnp.bfloat16)],
        compiler_params=_params("parallel", "parallel", "arbitrary"),
        name="input_projection",
    )(x, mod, mod, norm_w.reshape(1, d), w_in, g64, tile(qn), tile(kn))


def _split3(v):
    h1 = _bf16(v)
    r1 = v - h1.astype(jnp.float32)
    h2 = _bf16(r1)
    h3 = _bf16(r1 - h2.astype(jnp.float32))
    return h1, h2, h3


def _hgrn_gates(f, lb, tri):
    sig = jax.nn.sigmoid(f)
    lg = jnp.log(lb + (1.0 - lb) * sig) * LOG2_E
    k = (1.0 - lb) * (1.0 - sig)
    h1, h2, h3 = _split3(lg)
    return k, _dot(tri, h1) + _dot(tri, h2) + _dot(tri, h3)


HG_GRP = 8
HG_UNROLL = 2


def _hgrn_head_products(q, v, k, bc, st, ones, rev, k_row, bc_row):
    c = HG_CHUNK
    end = 0 if rev else c - 1
    kd = k * jnp.exp2(bc_row(end, c) - bc)
    v_t = _bf16(v.astype(jnp.float32).T)
    st_new = st * jnp.exp2(bc_row(end, st.shape[0])) + _dot(v_t, _bf16(kd))

    row = lax.broadcasted_iota(jnp.int32, (HG_GRP, HG_D), 0)
    a_off, ps, where = [], [], []
    for i in range(c // HG_SUB):
        lo, hi = i * HG_SUB, (i + 1) * HG_SUB
        qi, bi = q[lo:hi], bc[lo:hi]
        prev = None
        if not rev and i > 0:
            ref, prev = lo - 1, slice(0, lo)
        if rev and hi < c:
            ref, prev = hi, slice(hi, c)
        if prev is None:
            a_off.append(None)
        else:
            qt = _bf16(qi * jnp.exp2(bi - bc_row(ref, HG_SUB)))
            kt = k[prev] * jnp.exp2(bc_row(ref, prev.stop - prev.start) - bc[prev])
            pad = jnp.zeros((c - kt.shape[0], HG_D), kt.dtype)
            kt = _bf16(jnp.concatenate([pad, kt] if rev else [kt, pad], axis=0))
            a_off.append(_dot_nt(qt, kt))
        for s in range(lo, hi):
            for g in range(lo // HG_GRP, hi // HG_GRP):
                g_lo = g * HG_GRP
                sees = (g_lo <= s) if rev else (g_lo + HG_GRP > s)
                if not sees:
                    continue
                diff = bc[g_lo:g_lo + HG_GRP] - bc_row(s, HG_GRP)
                if g_lo <= s < g_lo + HG_GRP:
                    valid = (row <= s - g_lo) if rev else (row >= s - g_lo)
                    diff = jnp.where(valid, diff, MASK_VALUE)
                where.append((g, s))
                ps.append(q[g_lo:g_lo + HG_GRP] * k_row(s, HG_GRP) * jnp.exp2(diff))
    rsum = _dot(_bf16(jnp.concatenate(ps, axis=0)), ones)
    o_inter = _dot_nt(_bf16(q * jnp.exp2(bc)), _bf16(st))
    return o_inter, st_new, a_off, rsum, where


def _hgrn_head_output(v, o_inter, a_off, rsum, where):
    c = HG_CHUNK
    lane = lax.broadcasted_iota(jnp.int32, (HG_GRP, c), 1)
    groups = [None] * (c // HG_GRP)
    for i, a in enumerate(a_off):
        for g in range(i * HG_SUB // HG_GRP, (i + 1) * HG_SUB // HG_GRP):
            r = (g - i * HG_SUB // HG_GRP) * HG_GRP
            groups[g] = jnp.zeros((HG_GRP, c), jnp.float32) if a is None else a[r:r + HG_GRP]
    for n, (g, s) in enumerate(where):
        groups[g] = jnp.where(lane == s, rsum[n * HG_GRP:(n + 1) * HG_GRP, :c], groups[g])
    return o_inter + _dot(_bf16(jnp.concatenate(groups, axis=0)), v)


def _hgrn_kernel(qf_ref, vf_ref, ff_ref, qb_ref, vb_ref, fb_ref, lb_ref, s0_ref, tri_ref, ones_ref,
                 of_ref, ob_ref, sfin_ref, st_ref, kb_ref, bb_ref, *, n_chunks):
    cb = pl.program_id(1)

    @pl.when(cb == 0)
    def _():
        st_ref[...] = s0_ref[0]

    ones = ones_ref[...]
    heads = [slice(h * HG_D, (h + 1) * HG_D) for h in range(HG_HEADS)]
    streams = ((qf_ref, vf_ref, ff_ref, of_ref, False), (qb_ref, vb_ref, fb_ref, ob_ref, True))

    def repeat_row(ref, u, di, hs):
        return lambda s, n: jnp.broadcast_to(ref[u, di, s:s + 1, hs], (n, HG_D))

    def chunks(it, carry):
        state = [[st_ref[di, h] for h in range(HG_HEADS)] for di in range(2)]
        loaded = []
        for u in range(HG_UNROLL):
            ci = it * HG_UNROLL + u
            for di, (q_ref, v_ref, f_ref, o_ref, rev) in enumerate(streams):
                r0 = pl.multiple_of(((n_chunks - 1 - ci) if rev else ci) * HG_CHUNK, HG_CHUNK)
                q = q_ref[0, pl.ds(r0, HG_CHUNK), :].astype(jnp.float32)
                v = v_ref[0, pl.ds(r0, HG_CHUNK), :]
                f = f_ref[0, pl.ds(r0, HG_CHUNK), :].astype(jnp.float32)
                k, bc = _hgrn_gates(f, lb_ref[di:di + 1], tri_ref[di])
                kb_ref[u, di] = k
                bb_ref[u, di] = bc
                loaded.append((u, di, o_ref, rev, r0, q, v, k, bc))
        pending = []
        for u, di, o_ref, rev, r0, q, v, k, bc in loaded:
            stage = []
            for h, hs in enumerate(heads):
                res = _hgrn_head_products(q[:, hs], v[:, hs], k[:, hs], bc[:, hs], state[di][h], ones, rev,
                                          repeat_row(kb_ref, u, di, hs), repeat_row(bb_ref, u, di, hs))
                state[di][h] = res[1]
                stage.append(res)
            pending.append((o_ref, r0, v, stage))
        for o_ref, r0, v, stage in pending:
            outs = [_hgrn_head_output(v[:, hs], o_inter, a_off, rsum, where)
                    for hs, (o_inter, _, a_off, rsum, where) in zip(heads, stage)]
            o_ref[0, pl.ds(r0, HG_CHUNK), :] = jnp.concatenate(outs, axis=1)
        for di in range(2):
            for h in range(HG_HEADS):
                st_ref[di, h] = state[di][h]
        return carry

    lax.fori_loop(0, n_chunks // HG_UNROLL, chunks, 0)

    @pl.when(cb == pl.num_programs(1) - 1)
    def _():
        sfin_ref[0] = st_ref[...]


def hgrn_scan(z, lb, s0, tc):
    b, l, _ = z.shape
    nb = l // tc
    ci = jnp.arange(HG_CHUNK)
    tri = _bf16(jnp.stack([ci[None, :] <= ci[:, None], ci[None, :] >= ci[:, None]]))
    ones = jnp.ones((HG_D, HG_D), jnp.bfloat16)
    fwd = lambda t: pl.BlockSpec((1, tc, GROUP_W), lambda bi, c: (bi, c, t))
    bwd = lambda t: pl.BlockSpec((1, tc, GROUP_W), lambda bi, c: (bi, nb - 1 - c, t))
    st_spec = pl.BlockSpec((1, 2, HG_HEADS, HG_D, HG_D), lambda bi, c: (bi, 0, 0, 0, 0))
    return pl.pallas_call(
        functools.partial(_hgrn_kernel, n_chunks=tc // HG_CHUNK),
        grid=(b, nb),
        in_specs=[fwd(T_HG_Q), fwd(T_HG_I), fwd(T_HG_FF), bwd(T_HG_Q), bwd(T_HG_I), bwd(T_HG_FB),
                  pl.BlockSpec((2, GROUP_W), lambda bi, c: (0, 0)),
                  st_spec,
                  pl.BlockSpec((2, HG_CHUNK, HG_CHUNK), lambda bi, c: (0, 0, 0)),
                  pl.BlockSpec((HG_D, HG_D), lambda bi, c: (0, 0))],
        out_specs=[pl.BlockSpec((1, tc, GROUP_W), lambda bi, c: (bi, c, 0)),
                   pl.BlockSpec((1, tc, GROUP_W), lambda bi, c: (bi, nb - 1 - c, 0)), st_spec],
        out_shape=[jax.ShapeDtypeStruct((b, l, GROUP_W), jnp.float32),
                   jax.ShapeDtypeStruct((b, l, GROUP_W), jnp.float32),
                   jax.ShapeDtypeStruct(s0.shape, jnp.float32)],
        scratch_shapes=[pltpu.VMEM((2, HG_HEADS, HG_D, HG_D), jnp.float32),
                        pltpu.VMEM((HG_UNROLL, 2, HG_CHUNK, GROUP_W), jnp.float32),
                        pltpu.VMEM((HG_UNROLL, 2, HG_CHUNK, GROUP_W), jnp.float32)],
        compiler_params=_params("parallel", "arbitrary"),
        name="hgrn_scan",
    )(z, z, z, z, z, z, lb, s0, tri, ones)


NA_ROWS = 4


def _softmax_pv(scores, values):
    m = functools.reduce(jnp.maximum, [jnp.max(s, axis=-1, keepdims=True) for s in scores])
    ps = [jnp.exp(s - m) for s in scores]
    den = functools.reduce(jnp.add, [jnp.sum(p, axis=-1, keepdims=True) for p in ps])
    num = functools.reduce(jnp.add, [_dot(_bf16(p), v) for p, v in zip(ps, values)])
    return num / den


def _natten_kernel(q_ref, k_ref, v_ref, qc_ref, kc_ref, vc_ref, bias_ref, o_ref, oc_ref, *, rows):
    lane = lax.broadcasted_iota(jnp.int32, (1, 2 * NA_HD), 1)
    first = lane < NA_HD
    kc = kc_ref[0]
    vc = vc_ref[0]
    zero = jnp.zeros((), q_ref.dtype)

    def stack(q):
        return jnp.concatenate([jnp.where(first, q, zero), jnp.where(first, zero, q)], axis=0)

    def unstack(o):
        n = o.shape[0] // 2
        return jnp.where(first, o[:n], o[n:])

    oc_ref[0] = unstack(_softmax_pv([_dot_nt(stack(qc_ref[0]), kc)], [vc])).astype(oc_ref.dtype)

    def row_group(it, carry):
        work = []
        for j in range(NA_ROWS):
            r = it * NA_ROWS + j
            r0 = jnp.clip(r - WIN_R // 2, 0, rows - WIN_R)
            delta = r0 - r + WIN_R - 1
            qs = pl.multiple_of(r * GRID_W, GRID_W)
            ks = pl.multiple_of(r0 * GRID_W, GRID_W)
            q2 = stack(q_ref[0, pl.ds(qs, GRID_W), :])
            kl = k_ref[0, pl.ds(ks, WIN_R * GRID_W), :]
            vl = v_ref[0, pl.ds(ks, WIN_R * GRID_W), :]
            work.append((qs, vl, _dot_nt(q2, kl) + bias_ref[0, 0, delta], _dot_nt(q2, kc)))
        outs = [unstack(_softmax_pv([s_loc, s_ctx], [vl, vc])) for _, vl, s_loc, s_ctx in work]
        for (qs, _, _, _), o in zip(work, outs):
            o_ref[0, pl.ds(qs, GRID_W), :] = o.astype(o_ref.dtype)
        return carry

    lax.fori_loop(0, rows // NA_ROWS, row_group, 0)


def _natten_bias(rpb):
    qcol = jnp.arange(GRID_W)[:, None]
    kcol = jnp.arange(GRID_W)[None, :]
    wstart = jnp.clip(qcol - WIN_C // 2, 0, GRID_W - WIN_C)
    in_win = (kcol >= wstart) & (kcol < wstart + WIN_C)
    dc = jnp.clip(kcol - qcol + WIN_C - 1, 0, 2 * WIN_C - 2)
    dr = jnp.arange(WIN_R)[:, None] + jnp.arange(WIN_R)[None, :]
    t = rpb[:, :, dc][:, dr]
    t = jnp.where(in_win[None, None, None], t.astype(jnp.float32), MASK_VALUE)
    t = t.transpose(0, 1, 3, 2, 4).reshape(NA_HEADS // 2, 2, WIN_R, GRID_W, WIN_R * GRID_W)
    return t.transpose(0, 2, 1, 3, 4).reshape(NA_HEADS // 2, WIN_R, 2 * GRID_W, WIN_R * GRID_W)


def natten(z, zc, bias, li):
    b, s, _ = z.shape
    l = zc.shape[1]
    rows = s // GRID_W
    assert rows >= WIN_R
    pairs = NA_HEADS // 2
    per_tile = GROUP_W // (2 * NA_HD)
    spec = lambda n, t: pl.BlockSpec((1, n, 2 * NA_HD), lambda bi, p: (bi, 0, t * per_tile + p))
    return pl.pallas_call(
        functools.partial(_natten_kernel, rows=rows),
        grid=(b, pairs),
        in_specs=[spec(s, T_NA_Q), spec(s, T_NA_K), spec(s, T_NA_V),
                  spec(l, T_NA_Q), spec(l, T_NA_K), spec(l, T_NA_V),
                  pl.BlockSpec((1, 1, WIN_R, 2 * GRID_W, WIN_R * GRID_W), lambda bi, p: (li, p, 0, 0, 0))],
        out_specs=[pl.BlockSpec((1, s, 2 * NA_HD), lambda bi, p: (bi, 0, p)),
                   pl.BlockSpec((1, l, 2 * NA_HD), lambda bi, p: (bi, 0, p))],
        out_shape=[jax.ShapeDtypeStruct((b, s, GROUP_W), jnp.bfloat16),
                   jax.ShapeDtypeStruct((b, l, GROUP_W), jnp.bfloat16)],
        compiler_params=_params("parallel", "parallel"),
        name="natten",
    )(z, z, z, zc, zc, zc, bias)


HALO = 16


def _pack_halves(h):
    half = h.shape[1] // 2
    lo = pltpu.bitcast(_bf16(h[:, :half]).astype(jnp.float32), jnp.uint32)
    hi = pltpu.bitcast(_bf16(h[:, half:]).astype(jnp.float32), jnp.uint32)
    return (lo >> 16) | (hi & jnp.uint32(0xFFFF0000))


def _unpack_halves(w):
    lo = pltpu.bitcast(w << 16, jnp.float32)
    hi = pltpu.bitcast(w & jnp.uint32(0xFFFF0000), jnp.float32)
    return _bf16(jnp.concatenate([lo, hi], axis=1))


def _sigmoid(v):
    return 0.5 * jnp.tanh(0.5 * v) + 0.5


def _merge_kernel(zb_ref, zc_ref, zu_ref, pc_ref, pu_ref, nc_ref, nu_ref, cw_ref,
                  ofw_ref, obw_ref, zg_ref, hgn_ref, g128_ref, yna_ref, *rest):
    n_gate = len(rest) - 13
    gate_refs, rest = rest[:n_gate], rest[n_gate:]
    (x_ref, gate_ref, wa_ref, wb_ref, wc_ref, wo_ref, n2_ref, shift_ref, scale_ref, wr_ref,
     xo_ref, hp_ref, lg_ref) = rest
    per_gate = n_gate // N_GATES
    gate = lambda k: _sigmoid(jnp.concatenate(
        [r[0] for r in gate_refs[k * per_gate:(k + 1) * per_gate]], axis=1).astype(jnp.float32))
    i = pl.program_id(1)
    tm = zb_ref.shape[1]
    f32 = jnp.float32

    v = zc_ref[0].astype(f32) * zu_ref[0].astype(f32)
    vp = pc_ref[0, HALO - 1:HALO].astype(f32) * pu_ref[0, HALO - 1:HALO].astype(f32)
    vn = nc_ref[0, 0:1].astype(f32) * nu_ref[0, 0:1].astype(f32)
    vp = jnp.where(i == 0, 0.0, vp)
    vn = jnp.where(i == pl.num_programs(1) - 1, 0.0, vn)
    row = lax.broadcasted_iota(jnp.int32, v.shape, 0)
    v_prev = jnp.where(row == 0, vp, pltpu.roll(v, 1, 0))
    v_next = jnp.where(row == tm - 1, vn, pltpu.roll(v, tm - 1, 0))
    cw = cw_ref[...]
    y_cv = zb_ref[0].astype(f32) * (cw[0:1] * v_prev + cw[1:2] * v + cw[2:3] * v_next)

    o = ofw_ref[0] + obw_ref[0]
    y_hg = _group_rms(o, g128_ref[...], hgn_ref[...]) * _silu(zg_ref[0].astype(f32))

    m = (gate(0) * _dot(_bf16(y_cv), wa_ref[0]) + gate(1) * _dot(_bf16(y_hg), wb_ref[0])
         + gate(2) * _dot(yna_ref[0], wc_ref[0]))
    x_new = x_ref[0] + gate_ref[0, 0] * _dot(_bf16(m), wo_ref[0])
    xo_ref[0] = x_new

    h2 = _modnorm(x_new, n2_ref[...], shift_ref[0, 0], scale_ref[0, 0])
    hp_ref[0] = _pack_halves(h2)
    lg_ref[0] = _dot_nt(wr_ref[0], _bf16(h2))


def merge(z, o_fw, o_bw, y_na, x, mod, conv_w, hg_norm, w_a, w_b, w_c, w_o, norm2, w_rt, li, tm):
    b, l, d = x.shape
    e = w_rt.shape[1]
    layer = lambda a: pl.BlockSpec((1,) + a.shape[1:], lambda bi, i: (li,) + (0,) * (a.ndim - 1))
    nt = l // tm
    per = tm // HALO
    n_halo = l // HALO
    assert d % GROUP_W == 0
    n_gate = N_GATES * d // GROUP_W
    bm = (lambda bi: bi) if mod.shape[0] == b else (lambda bi: 0)
    zt = lambda t: pl.BlockSpec((1, tm, GROUP_W), lambda bi, i: (bi, i, t))
    zprev = lambda t: pl.BlockSpec((1, HALO, GROUP_W), lambda bi, i: (bi, jnp.maximum(i * per - 1, 0), t))
    znext = lambda t: pl.BlockSpec((1, HALO, GROUP_W),
                                   lambda bi, i: (bi, jnp.minimum((i + 1) * per, n_halo - 1), t))
    act = lambda w: pl.BlockSpec((1, tm, w), lambda bi, i: (bi, i, 0))
    modrow = lambda k: pl.BlockSpec((1, 1, 1, d), lambda bi, i: (bm(bi), k, 0, 0))
    full = lambda a: pl.BlockSpec(a.shape, lambda bi, i: (0,) * a.ndim)
    g128 = _block_diag_mean(GROUP_W, HG_D)
    hgn = jnp.tile(hg_norm.reshape(1, -1), (1, GROUP_W // HG_D))
    n2 = norm2.reshape(1, d)
    return pl.pallas_call(
        _merge_kernel,
        grid=(b, nt),
        in_specs=[zt(T_CONV_B), zt(T_CONV_C), zt(T_CONV_U),
                  zprev(T_CONV_C), zprev(T_CONV_U), znext(T_CONV_C), znext(T_CONV_U), full(conv_w),
                  act(GROUP_W), act(GROUP_W), zt(T_HG_G), full(hgn), full(g128), act(GROUP_W)]
                 + [zt(T_GATES + k) for k in range(n_gate)]
                 + [act(d), modrow(2), layer(w_a), layer(w_b), layer(w_c), layer(w_o),
                  full(n2), modrow(3), modrow(4), layer(w_rt)],
        out_specs=[act(d), act(d // 2), pl.BlockSpec((1, e, tm), lambda bi, i: (bi, 0, i))],
        out_shape=[jax.ShapeDtypeStruct((b, l, d), jnp.float32),
                   jax.ShapeDtypeStruct((b, l, d // 2), jnp.uint32),
                   jax.ShapeDtypeStruct((b, e, l), jnp.float32)],
        compiler_params=_params("parallel", "arbitrary"),
        name="merge",
    )(z, z, z, z, z, z, z, conv_w, o_fw, o_bw, z, hgn, g128, y_na, *([z] * n_gate),
      x, mod, w_a, w_b, w_c, w_o, n2, mod, mod, w_rt)


LANES = 128
ROUTE_K_CHUNK = 1024


def _count(mask):
    return jnp.sum(jnp.where(mask, 1.0, 0.0), axis=1, keepdims=True)


def _route_kernel(lg_ref, tmat_ref, excl_ref, slot_ref, aff_ref, idx_ref, starts_ref, *, cap, tile):
    f32 = jnp.float32
    lg = lg_ref[0]
    e, s = lg.shape
    ex = jnp.exp(lg - jnp.max(lg, axis=0, keepdims=True))
    aff = ex / jnp.sum(ex, axis=0, keepdims=True)
    aff_ref[0] = aff
    bits = pltpu.bitcast(aff, jnp.int32)

    def thr_bit(it, thr):
        cand = thr | (jnp.int32(1) << (30 - it))
        return jnp.where(_count(bits >= cand) >= cap, cand, thr)

    thr = lax.fori_loop(0, 31, thr_bit, jnp.zeros((e, 1), jnp.int32))
    gt = bits > thr
    eq = bits == thr
    need = cap - _count(gt)
    tok = lax.broadcasted_iota(jnp.int32, (e, s), 1)
    nbits = s.bit_length()

    def end_bit(it, end):
        cand = end + (jnp.int32(1) << (nbits - 1 - it))
        ok = (cand <= s) & (_count(eq & (tok < cand)) <= need)
        return jnp.where(ok, cand, end)

    end = lax.fori_loop(0, nbits, end_bit, jnp.zeros((e, 1), jnp.int32))
    sel = gt | (eq & (tok < end))

    self = jnp.where(sel, 1.0, 0.0)
    offs = jnp.zeros((e, 1), f32)
    pieces = []
    tile_lane = lax.broadcasted_iota(jnp.int32, (e, LANES), 1)
    starts = jnp.zeros((e, LANES), f32)
    for c in range(s // LANES):
        if (c * LANES) % tile == 0:
            starts = jnp.where(tile_lane == c * LANES // tile, offs, starts)
        blk = self[:, c * LANES:(c + 1) * LANES]
        pieces.append(_dot(_bf16(blk), excl_ref[...]) + offs)
        offs = offs + jnp.sum(blk, axis=1, keepdims=True)
    starts_ref[0] = jnp.where(tile_lane == s // tile, offs, starts).astype(jnp.int32)
    slot_ref[0] = jnp.where(sel, jnp.concatenate(pieces, axis=1), -1.0)

    piota = lax.broadcasted_iota(jnp.int32, (cap, 1), 0).astype(f32)
    lane = lax.broadcasted_iota(jnp.int32, (cap, LANES), 1)
    kc = min(ROUTE_K_CHUNK, s)
    idx_ref[0] = jnp.zeros((cap, LANES), jnp.int32)

    def expert(ee, carry):
        res = jnp.zeros((cap, LANES), f32)
        for c in range(s // kc):
            row = slot_ref[0, pl.ds(ee, 1), c * kc:(c + 1) * kc]
            onehot = _bf16(jnp.where(row == piota, 1.0, 0.0))
            res = res + _dot(onehot, tmat_ref[c * kc:(c + 1) * kc, :])
        tok_idx = (res[:, 0:1] * 64.0 + res[:, 1:2]).astype(jnp.int32)
        idx_ref[0] = jnp.where(lane == ee, tok_idx, idx_ref[0])
        return carry

    lax.fori_loop(0, e, expert, 0)


def route(lg, cap, tile):
    b, e, s = lg.shape
    assert tile % LANES == 0 and s % tile == 0 and s // tile < LANES
    t = jnp.arange(s)
    tmat = jnp.zeros((s, LANES), jnp.bfloat16).at[:, 0].set(_bf16(t // 64)).at[:, 1].set(_bf16(t % 64))
    a = jnp.arange(LANES)
    excl = _bf16(a[:, None] < a[None, :])
    spec = pl.BlockSpec((1, e, s), lambda bi: (bi, 0, 0))
    slot, aff, idx, starts = pl.pallas_call(
        functools.partial(_route_kernel, cap=cap, tile=tile),
        grid=(b,),
        in_specs=[spec, pl.BlockSpec((s, LANES), lambda bi: (0, 0)), pl.BlockSpec((LANES, LANES), lambda bi: (0, 0))],
        out_specs=[spec, spec, pl.BlockSpec((1, cap, LANES), lambda bi: (bi, 0, 0)),
                   pl.BlockSpec((1, e, LANES), lambda bi: (bi, 0, 0))],
        out_shape=[jax.ShapeDtypeStruct((b, e, s), jnp.float32), jax.ShapeDtypeStruct((b, e, s), jnp.float32),
                   jax.ShapeDtypeStruct((b, cap, LANES), jnp.int32), jax.ShapeDtypeStruct((b, e, LANES), jnp.int32)],
        compiler_params=_params("parallel"),
        name="route",
    )(lg, tmat, excl)
    return slot, aff, idx[:, :, :e].transpose(0, 2, 1), starts[:, :, :s // tile + 1]


def _gather_kernel(idx_ref, h_ref, o_ref, *, cap):
    def body(j, carry):
        t = idx_ref[0, 0, j]
        o_ref[0, 0, pl.ds(j, 1), :] = h_ref[0, pl.ds(t, 1), :]
        return carry

    lax.fori_loop(0, cap, body, 0, unroll=8)


def gather(idx, hp, out_shape, out_index):
    b, e, cap = idx.shape
    _, s, w = hp.shape
    return pl.pallas_call(
        functools.partial(_gather_kernel, cap=cap),
        grid=(b, e),
        in_specs=[pl.BlockSpec((1, 1, cap), lambda bi, ei: (bi * e + ei, 0, 0), memory_space=pltpu.SMEM),
                  pl.BlockSpec((1, s, w), lambda bi, ei: (bi, 0, 0))],
        out_specs=pl.BlockSpec((1, 1, cap, w), out_index),
        out_shape=jax.ShapeDtypeStruct(out_shape, jnp.uint32),
        compiler_params=_params("parallel", "arbitrary"),
        name="gather",
    )(idx.reshape(b * e, 1, cap), hp)


FFN_CHUNK = 512


def _ffn_body(xw, wg_ref, wu_ref, wd_ref):
    x = _unpack_halves(xw)
    f = wg_ref.shape[3]
    fc = min(FFN_CHUNK, f)
    acc = jnp.zeros((x.shape[0], wd_ref.shape[3]), jnp.float32)
    for c in range(f // fc):
        a = _dot(x, wg_ref[0, 0, :, c * fc:(c + 1) * fc])
        u = _dot(x, wu_ref[0, 0, :, c * fc:(c + 1) * fc])
        acc = acc + _dot(_bf16(_silu(a) * u), wd_ref[0, 0, c * fc:(c + 1) * fc, :])
    return _bf16(acc)


def _ffn_kernel(xl_ref, wg_ref, wu_ref, wd_ref, yl_ref):
    yl_ref[0, 0] = _ffn_body(xl_ref[0, 0], wg_ref, wu_ref, wd_ref)


def _ffn_ctx_kernel(xl_ref, xc_ref, wg_ref, wu_ref, wd_ref, yl_ref, yc_ref, *, nb):
    is_ctx = pl.program_id(1) == nb
    y = _ffn_body(jnp.where(is_ctx, xc_ref[0, 0], xl_ref[0, 0]), wg_ref, wu_ref, wd_ref)

    @pl.when(jnp.logical_not(is_ctx))
    def _():
        yl_ref[0, 0] = y

    @pl.when(is_ctx)
    def _():
        yc_ref[0, 0] = y


def expert_ffn(xe, xe_c, w_g, w_u, w_d, li):
    nb, e, cap, w = xe.shape
    d, f = w_g.shape[2:]
    wspec = lambda shape: pl.BlockSpec((1, 1) + shape, lambda ei, bi: (li, ei, 0, 0))
    weights = [wspec((d, f)), wspec((d, f)), wspec((f, d))]
    lat = lambda width: pl.BlockSpec((1, 1, cap, width), lambda ei, bi: (jnp.minimum(bi, nb - 1), ei, 0, 0))
    cspec = lambda width: pl.BlockSpec((1, 1, cap, width), lambda ei, bi: (0, ei, 0, 0))
    if xe_c is None:
        return pl.pallas_call(
            _ffn_kernel, grid=(e, nb),
            in_specs=[lat(w)] + weights, out_specs=lat(d),
            out_shape=jax.ShapeDtypeStruct((nb, e, cap, d), jnp.bfloat16),
            compiler_params=_params("parallel", "arbitrary"), name="expert_ffn",
        )(xe, w_g, w_u, w_d), None
    assert xe_c.shape == (1, e, cap, w)
    return pl.pallas_call(
        functools.partial(_ffn_ctx_kernel, nb=nb), grid=(e, nb + 1),
        in_specs=[lat(w), cspec(w)] + weights, out_specs=[lat(d), cspec(d)],
        out_shape=[jax.ShapeDtypeStruct((nb, e, cap, d), jnp.bfloat16),
                   jax.ShapeDtypeStruct((1, e, cap, d), jnp.bfloat16)],
        compiler_params=_params("parallel", "arbitrary"), name="expert_ffn_ctx",
    )(xe, xe_c, w_g, w_u, w_d)


SLOT_SPLIT = 16
COMBINE_WIN = 256


def _combine_kernel(starts_ref, slot_ref, aff_ref, ye_ref, x_ref, gate_ref, o_ref, *, slots_per_sample):
    f32 = jnp.float32
    bi, ti = pl.program_id(0), pl.program_id(1)
    n_exp, n_slots, d = ye_ref.shape[1:]
    win = min(COMBINE_WIN, n_slots)
    base = bi * slots_per_sample
    slot_t = slot_ref[0]
    tm = slot_t.shape[0]
    aff_t = aff_ref[0]
    slot_t = jnp.where(slot_t < 0.0, -1.0, slot_t + base.astype(f32))
    expert_lane = lax.broadcasted_iota(jnp.int32, slot_t.shape, 1)
    lane = lax.broadcasted_iota(jnp.int32, (tm, win), 1).astype(f32)

    def column(tile, e):
        col = jnp.sum(jnp.where(expert_lane == e, tile, 0.0), axis=1, keepdims=True)
        return jnp.broadcast_to(col, (tm, LANES))
    widen = lambda v, width: jnp.concatenate([v] * (width // LANES), axis=1)

    def first_window(e):
        lo = base + starts_ref[bi, e, ti]
        return jnp.minimum(lo // LANES * LANES, n_slots - win)

    def pick(slot_b, e, nominal):
        start = pl.multiple_of(jnp.minimum(nominal, n_slots - win), LANES)
        slot_w = widen(slot_b, win)
        hit = (slot_w - start.astype(f32) == lane) & (slot_w >= nominal.astype(f32))
        return _dot(_bf16(jnp.where(hit, 1.0, 0.0)), ye_ref[0, e, pl.ds(start, win), :])

    for e in range(n_exp):
        term = widen(column(aff_t, e), d) * pick(column(slot_t, e), e, first_window(e))
        if e == 0:
            o_ref[0] = term
        else:
            o_ref[0] += term

    def expert(e, carry):
        ws = first_window(e)
        hi = base + starts_ref[bi, e, ti + 1]

        def window(k, c2):
            o_ref[0] += widen(column(aff_t, e), d) * pick(column(slot_t, e), e, ws + k * win)
            return c2

        return lax.fori_loop(1, (hi - ws + win - 1) // win, window, carry)

    lax.fori_loop(0, n_exp, expert, 0)
    o_ref[0] = x_ref[0] + gate_ref[0, 0] * o_ref[0]


def combine(starts, slot_t, aff_t, ye, x, mod, tm):
    b, l, d = x.shape
    e = slot_t.shape[2]
    pooled = ye.shape[0] == 1 and b > 1
    bm = (lambda bi: bi) if mod.shape[0] == b else (lambda bi: 0)
    tok = lambda w: pl.BlockSpec((1, tm, w), lambda bi, i, st: (bi, i, 0))
    ye_spec = pl.BlockSpec((1,) + ye.shape[1:], lambda bi, i, st: (0 if pooled else bi, 0, 0, 0),
                           pipeline_mode=pl.Buffered(1))
    return pl.pallas_call(
        functools.partial(_combine_kernel, slots_per_sample=ye.shape[2] // b if pooled else 0),
        grid_spec=pltpu.PrefetchScalarGridSpec(
            num_scalar_prefetch=1,
            grid=(b, l // tm),
            in_specs=[tok(e), tok(e), ye_spec, tok(d),
                      pl.BlockSpec((1, 1, 1, d), lambda bi, i, st: (bm(bi), N_MOD - 1, 0, 0))],
            out_specs=tok(d)),
        out_shape=jax.ShapeDtypeStruct((b, l, d), jnp.float32),
        compiler_params=_params("parallel", "arbitrary"),
        name="combine",
    )(starts, slot_t, aff_t, ye, x, mod)


TM_IN = 2048
TM_MERGE = 256
TM_COMBINE = 512
HG_ROWS = 512


def _route_and_gather(hp, lg, xe_shape, xe_index, cap, tile):
    slot, aff, idx, starts = route(lg, cap, tile)
    xe = gather(idx, hp, xe_shape, xe_index)
    return (starts, slot.transpose(0, 2, 1), aff.transpose(0, 2, 1)), xe


def kernel(x, c, ctx, c_ctx, w_mod, b_mod, norm1, w_in, conv_w, hg_lb_logits, hg_norm, na_q_norm, na_k_norm, na_rpb,
           w_br_a, w_br_b, w_br_c, w_out, norm2, w_router, w_e_gate, w_e_up, w_e_down):
    b, s, d = x.shape
    l = ctx.shape[1]
    depth = w_mod.shape[0]
    e = w_router.shape[-1]
    cap = CAP_FACTOR * s // e
    cap_c = CAP_FACTOR * l // e
    assert b * cap_c == cap, "context rows of all samples fill one expert tile"

    lb_sm = jax.nn.softmax(hg_lb_logits.astype(jnp.float32), axis=0)
    lb_all = jnp.cumsum(lb_sm, axis=0) - lb_sm[0]
    rows = -(-(b + 1) // 8) * 8
    cc = jnp.zeros((rows, d), jnp.float32).at[:b].set(c).at[b].set(c_ctx)
    mod_all = modulation(cc, w_mod, b_mod).reshape(depth, rows, N_MOD, 1, d)

    w_in_r = _bf16(w_in)
    w_a, w_b, w_c, w_o = _bf16(w_br_a), _bf16(w_br_b), _bf16(w_br_c), _bf16(w_out)
    w_rt = _bf16(jnp.swapaxes(w_router, 1, 2))
    w_g, w_u, w_d = _bf16(w_e_gate), _bf16(w_e_up), _bf16(w_e_down)
    bias = jax.vmap(_natten_bias)(na_rpb)
    s0 = jnp.zeros((b, 2, HG_HEADS, HG_D, HG_D), jnp.float32)
    hg_rows = min(HG_ROWS, s)
    tm_in = min(TM_IN, s)
    tm_cmb = min(TM_COMBINE, s)

    xc = ctx
    for li in range(depth):
        last = li == depth - 1
        mod = mod_all[li, :b]
        mod_c = mod_all[li, b:b + 1]
        z = input_projection(x, mod, norm1[li], w_in_r, li, na_q_norm[li], na_k_norm[li], tm_in)
        zc = input_projection(xc.reshape(1, b * l, d), mod_c, norm1[li], w_in_r, li, na_q_norm[li], na_k_norm[li],
                              min(TM_IN, b * l)).reshape(b, l, -1)
        oc_fw, oc_bw, s_ctx = hgrn_scan(zc, lb_all[li], s0, l)
        o_fw, o_bw, _ = hgrn_scan(z, lb_all[li], s_ctx, hg_rows)
        y_na, yc_na = natten(z, zc, bias, li)

        mw = (conv_w[li], hg_norm[li], w_a, w_b, w_c, w_o, norm2[li], w_rt, li)
        x, hp, lg = merge(z, o_fw, o_bw, y_na, x, mod, *mw, TM_MERGE)
        plan, xe = _route_and_gather(hp, lg, (b, e, cap, d // 2), lambda bi, ei: (bi, ei, 0, 0), cap, tm_cmb)
        if last:
            ye, _ = expert_ffn(xe, None, w_g, w_u, w_d, li)
        else:
            xc, hpc, lgc = merge(zc, oc_fw, oc_bw, yc_na, xc, mod_c, *mw, l)
            plan_c, xe_c = _route_and_gather(hpc, lgc, (1, e, cap, d // 2), lambda bi, ei: (0, ei, bi, 0), cap_c, l)
            ye, ye_c = expert_ffn(xe, xe_c, w_g, w_u, w_d, li)
            xc = combine(*plan_c, ye_c, xc, mod_c, l)
        x = combine(*plan, ye, x, mod, tm_cmb)
    return x
```

```python
import functools

import jax
import jax.numpy as jnp
from jax import lax
from jax.experimental import pallas as pl
from jax.experimental.pallas import tpu as pltpu

N_MOD = 6
EPS = 1e-6
MASK_VALUE = -1e30
GRID_W = 64
GROUP_W = 512
HG_HEADS = 4
HG_D = 128
HG_CHUNK = 64
HG_SUB = 8
LOG2_E = 1.4426950408889634
NA_HEADS = 8
NA_HD = 64
WIN_R = 8
WIN_C = 16
CAP_FACTOR = 2
T_CONV_B, T_CONV_C, T_CONV_U, T_HG_Q, T_HG_I, T_HG_FF, T_HG_FB, T_HG_G, T_NA_Q, T_NA_K, T_NA_V, T_GATES = range(12)
N_GATES = 3

VMEM_LIMIT_BYTES = 48 * 1024 * 1024


def _params(*semantics):
    return pltpu.CompilerParams(dimension_semantics=semantics, vmem_limit_bytes=VMEM_LIMIT_BYTES)


def _silu(v):
    return v * jax.nn.sigmoid(v)


def _bf16(v):
    return v.astype(jnp.bfloat16)


def _dot(a, b):
    return jnp.dot(a, b, preferred_element_type=jnp.float32)


def _dot_nt(a, b):
    return lax.dot_general(a, b, (((1,), (1,)), ((), ())), preferred_element_type=jnp.float32)


def _mod_kernel(c_ref, w_ref, b_ref, o_ref):
    cond = _bf16(_silu(c_ref[...]))
    o_ref[0] = _dot(cond, _bf16(w_ref[0])) + b_ref[0]


def modulation(cc, w_mod, b_mod):
    depth, d, nm = w_mod.shape
    r = cc.shape[0]
    tn = d
    return pl.pallas_call(
        _mod_kernel,
        grid=(depth, nm // tn),
        in_specs=[pl.BlockSpec((r, d), lambda l, j: (0, 0)),
                  pl.BlockSpec((1, d, tn), lambda l, j: (l, 0, j)),
                  pl.BlockSpec((1, 1, tn), lambda l, j: (l, 0, j))],
        out_specs=pl.BlockSpec((1, r, tn), lambda l, j: (l, 0, j)),
        out_shape=jax.ShapeDtypeStruct((depth, r, nm), jnp.float32),
        compiler_params=_params("parallel", "parallel"),
        name="modulation",
    )(cc, w_mod, b_mod.reshape(depth, 1, nm))


def _modnorm(x, w, shift, scale):
    ms = jnp.mean(x * x, axis=-1, keepdims=True)
    return (x * lax.rsqrt(ms + EPS) * w) * (1.0 + scale) + shift


def _group_rms(acc, gmat, w_tiled):
    ms = _dot(_bf16(acc * acc), gmat)
    return acc * lax.rsqrt(ms + EPS) * w_tiled


def _inproj_kernel(x_ref, shift_ref, scale_ref, nw_ref, w_ref, g64_ref, qn_ref, kn_ref, o_ref, h_ref):
    j = pl.program_id(2)

    @pl.when(j == 0)
    def _():
        h_ref[...] = _bf16(_modnorm(x_ref[0], nw_ref[...], shift_ref[0, 0], scale_ref[0, 0]))

    acc = lambda: _dot(h_ref[...], w_ref[0])
    plain = (j != T_HG_Q) & (j != T_NA_Q) & (j != T_NA_K)

    @pl.when(plain)
    def _():
        o_ref[0] = acc().astype(o_ref.dtype)

    @pl.when(j == T_HG_Q)
    def _():
        o_ref[0] = (_silu(acc()) * (HG_D ** -0.5)).astype(o_ref.dtype)

    @pl.when(j == T_NA_Q)
    def _():
        o_ref[0] = (_group_rms(acc(), g64_ref[...], qn_ref[...]) * (NA_HD ** -0.5)).astype(o_ref.dtype)

    @pl.when(j == T_NA_K)
    def _():
        o_ref[0] = _group_rms(acc(), g64_ref[...], kn_ref[...]).astype(o_ref.dtype)


def _block_diag_mean(n, group):
    idx = jnp.arange(n) // group
    return _bf16(jnp.where(idx[:, None] == idx[None, :], 1.0 / group, 0.0))


def input_projection(x, mod, norm_w, w_in, li, qn, kn, tm):
    bx, s, d = x.shape
    n = w_in.shape[2]
    tn = GROUP_W
    g64 = _block_diag_mean(tn, NA_HD)
    tile = lambda v: jnp.tile(v.reshape(1, -1), (1, tn // v.shape[-1]))
    return pl.pallas_call(
        _inproj_kernel,
        grid=(bx, s // tm, n // tn),
        in_specs=[pl.BlockSpec((1, tm, d), lambda b, i, j: (b, i, 0)),
                  pl.BlockSpec((1, 1, 1, d), lambda b, i, j: (b, 0, 0, 0)),
                  pl.BlockSpec((1, 1, 1, d), lambda b, i, j: (b, 1, 0, 0)),
                  pl.BlockSpec((1, d), lambda b, i, j: (0, 0)),
                  pl.BlockSpec((1, d, tn), lambda b, i, j: (li, 0, j)),
                  pl.BlockSpec((tn, tn), lambda b, i, j: (0, 0)),
                  pl.BlockSpec((1, tn), lambda b, i, j: (0, 0)),
                  pl.BlockSpec((1, tn), lambda b, i, j: (0, 0))],
        out_specs=pl.BlockSpec((1, tm, tn), lambda b, i, j: (b, i, j)),
        out_shape=jax.ShapeDtypeStruct((bx, s, n), jnp.bfloat16),
        scratch_shapes=[pltpu.VMEM((tm, d), jnp.bfloat16)],
        compiler_params=_params("parallel", "parallel", "arbitrary"),
        name="input_projection",
    )(x, mod, mod, norm_w.reshape(1, d), w_in, g64, tile(qn), tile(kn))


def _split3(v):
    h1 = _bf16(v)
    r1 = v - h1.astype(jnp.float32)
    h2 = _bf16(r1)
    h3 = _bf16(r1 - h2.astype(jnp.float32))
    return h1, h2, h3


def _hgrn_gates(f, lb, tri):
    sig = jax.nn.sigmoid(f)
    lg = jnp.log(lb + (1.0 - lb) * sig) * LOG2_E
    k = (1.0 - lb) * (1.0 - sig)
    h1, h2, h3 = _split3(lg)
    return k, _dot(tri, h1) + _dot(tri, h2) + _dot(tri, h3)


HG_GRP = 8
HG_UNROLL = 2


def _hgrn_head_products(q, v, k, bc, st, ones, rev, k_row, bc_row):
    c = HG_CHUNK
    end = 0 if rev else c - 1
    kd = k * jnp.exp2(bc_row(end, c) - bc)
    v_t = _bf16(v.astype(jnp.float32).T)
    st_new = st * jnp.exp2(bc_row(end, st.shape[0])) + _dot(v_t, _bf16(kd))

    row = lax.broadcasted_iota(jnp.int32, (HG_GRP, HG_D), 0)
    a_off, ps, where = [], [], []
    for i in range(c // HG_SUB):
        lo, hi = i * HG_SUB, (i + 1) * HG_SUB
        qi, bi = q[lo:hi], bc[lo:hi]
        prev = None
        if not rev and i > 0:
            ref, prev = lo - 1, slice(0, lo)
        if rev and hi < c:
            ref, prev = hi, slice(hi, c)
        if prev is None:
            a_off.append(None)
        else:
            qt = _bf16(qi * jnp.exp2(bi - bc_row(ref, HG_SUB)))
            kt = k[prev] * jnp.exp2(bc_row(ref, prev.stop - prev.start) - bc[prev])
            pad = jnp.zeros((c - kt.shape[0], HG_D), kt.dtype)
            kt = _bf16(jnp.concatenate([pad, kt] if rev else [kt, pad], axis=0))
            a_off.append(_dot_nt(qt, kt))
        for s in range(lo, hi):
            for g in range(lo // HG_GRP, hi // HG_GRP):
                g_lo = g * HG_GRP
                sees = (g_lo <= s) if rev else (g_lo + HG_GRP > s)
                if not sees:
                    continue
                diff = bc[g_lo:g_lo + HG_GRP] - bc_row(s, HG_GRP)
                if g_lo <= s < g_lo + HG_GRP:
                    valid = (row <= s - g_lo) if rev else (row >= s - g_lo)
                    diff = jnp.where(valid, diff, MASK_VALUE)
                where.append((g, s))
                ps.append(q[g_lo:g_lo + HG_GRP] * k_row(s, HG_GRP) * jnp.exp2(diff))
    rsum = _dot(_bf16(jnp.concatenate(ps, axis=0)), ones)
    o_inter = _dot_nt(_bf16(q * jnp.exp2(bc)), _bf16(st))
    return o_inter, st_new, a_off, rsum, where


def _hgrn_head_output(v, o_inter, a_off, rsum, where):
    c = HG_CHUNK
    lane = lax.broadcasted_iota(jnp.int32, (HG_GRP, c), 1)
    groups = [None] * (c // HG_GRP)
    for i, a in enumerate(a_off):
        for g in range(i * HG_SUB // HG_GRP, (i + 1) * HG_SUB // HG_GRP):
            r = (g - i * HG_SUB // HG_GRP) * HG_GRP
            groups[g] = jnp.zeros((HG_GRP, c), jnp.float32) if a is None else a[r:r + HG_GRP]
    for n, (g, s) in enumerate(where):
        groups[g] = jnp.where(lane == s, rsum[n * HG_GRP:(n + 1) * HG_GRP, :c], groups[g])
    return o_inter + _dot(_bf16(jnp.concatenate(groups, axis=0)), v)


def _hgrn_kernel(qf_ref, vf_ref, ff_ref, qb_ref, vb_ref, fb_ref, lb_ref, s0_ref, tri_ref, ones_ref,
                 of_ref, ob_ref, sfin_ref, st_ref, kb_ref, bb_ref, *, n_chunks):
    cb = pl.program_id(1)

    @pl.when(cb == 0)
    def _():
        st_ref[...] = s0_ref[0]

    ones = ones_ref[...]
    heads = [slice(h * HG_D, (h + 1) * HG_D) for h in range(HG_HEADS)]
    streams = ((qf_ref, vf_ref, ff_ref, of_ref, False), (qb_ref, vb_ref, fb_ref, ob_ref, True))

    def repeat_row(ref, u, di, hs):
        return lambda s, n: jnp.broadcast_to(ref[u, di, s:s + 1, hs], (n, HG_D))

    def chunks(it, carry):
        state = [[st_ref[di, h] for h in range(HG_HEADS)] for di in range(2)]
        loaded = []
        for u in range(HG_UNROLL):
            ci = it * HG_UNROLL + u
            for di, (q_ref, v_ref, f_ref, o_ref, rev) in enumerate(streams):
                r0 = pl.multiple_of(((n_chunks - 1 - ci) if rev else ci) * HG_CHUNK, HG_CHUNK)
                q = q_ref[0, pl.ds(r0, HG_CHUNK), :].astype(jnp.float32)
                v = v_ref[0, pl.ds(r0, HG_CHUNK), :]
                f = f_ref[0, pl.ds(r0, HG_CHUNK), :].astype(jnp.float32)
                k, bc = _hgrn_gates(f, lb_ref[di:di + 1], tri_ref[di])
                kb_ref[u, di] = k
                bb_ref[u, di] = bc
                loaded.append((u, di, o_ref, rev, r0, q, v, k, bc))
        pending = []
        for u, di, o_ref, rev, r0, q, v, k, bc in loaded:
            stage = []
            for h, hs in enumerate(heads):
                res = _hgrn_head_products(q[:, hs], v[:, hs], k[:, hs], bc[:, hs], state[di][h], ones, rev,
                                          repeat_row(kb_ref, u, di, hs), repeat_row(bb_ref, u, di, hs))
                state[di][h] = res[1]
                stage.append(res)
            pending.append((o_ref, r0, v, stage))
        for o_ref, r0, v, stage in pending:
            outs = [_hgrn_head_output(v[:, hs], o_inter, a_off, rsum, where)
                    for hs, (o_inter, _, a_off, rsum, where) in zip(heads, stage)]
            o_ref[0, pl.ds(r0, HG_CHUNK), :] = jnp.concatenate(outs, axis=1)
        for di in range(2):
            for h in range(HG_HEADS):
                st_ref[di, h] = state[di][h]
        return carry

    lax.fori_loop(0, n_chunks // HG_UNROLL, chunks, 0)

    @pl.when(cb == pl.num_programs(1) - 1)
    def _():
        sfin_ref[0] = st_ref[...]


def hgrn_scan(z, lb, s0, tc):
    b, l, _ = z.shape
    nb = l // tc
    ci = jnp.arange(HG_CHUNK)
    tri = _bf16(jnp.stack([ci[None, :] <= ci[:, None], ci[None, :] >= ci[:, None]]))
    ones = jnp.ones((HG_D, HG_D), jnp.bfloat16)
    fwd = lambda t: pl.BlockSpec((1, tc, GROUP_W), lambda bi, c: (bi, c, t))
    bwd = lambda t: pl.BlockSpec((1, tc, GROUP_W), lambda bi, c: (bi, nb - 1 - c, t))
    st_spec = pl.BlockSpec((1, 2, HG_HEADS, HG_D, HG_D), lambda bi, c: (bi, 0, 0, 0, 0))
    return pl.pallas_call(
        functools.partial(_hgrn_kernel, n_chunks=tc // HG_CHUNK),
        grid=(b, nb),
        in_specs=[fwd(T_HG_Q), fwd(T_HG_I), fwd(T_HG_FF), bwd(T_HG_Q), bwd(T_HG_I), bwd(T_HG_FB),
                  pl.BlockSpec((2, GROUP_W), lambda bi, c: (0, 0)),
                  st_spec,
                  pl.BlockSpec((2, HG_CHUNK, HG_CHUNK), lambda bi, c: (0, 0, 0)),
                  pl.BlockSpec((HG_D, HG_D), lambda bi, c: (0, 0))],
        out_specs=[pl.BlockSpec((1, tc, GROUP_W), lambda bi, c: (bi, c, 0)),
                   pl.BlockSpec((1, tc, GROUP_W), lambda bi, c: (bi, nb - 1 - c, 0)), st_spec],
        out_shape=[jax.ShapeDtypeStruct((b, l, GROUP_W), jnp.float32),
                   jax.ShapeDtypeStruct((b, l, GROUP_W), jnp.float32),
                   jax.ShapeDtypeStruct(s0.shape, jnp.float32)],
        scratch_shapes=[pltpu.VMEM((2, HG_HEADS, HG_D, HG_D), jnp.float32),
                        pltpu.VMEM((HG_UNROLL, 2, HG_CHUNK, GROUP_W), jnp.float32),
                        pltpu.VMEM((HG_UNROLL, 2, HG_CHUNK, GROUP_W), jnp.float32)],
        compiler_params=_params("parallel", "arbitrary"),
        name="hgrn_scan",
    )(z, z, z, z, z, z, lb, s0, tri, ones)


NA_ROWS = 4


def _softmax_pv(scores, values):
    m = functools.reduce(jnp.maximum, [jnp.max(s, axis=-1, keepdims=True) for s in scores])
    ps = [jnp.exp(s - m) for s in scores]
    den = functools.reduce(jnp.add, [jnp.sum(p, axis=-1, keepdims=True) for p in ps])
    num = functools.reduce(jnp.add, [_dot(_bf16(p), v) for p, v in zip(ps, values)])
    return num / den


def _natten_kernel(q_ref, k_ref, v_ref, qc_ref, kc_ref, vc_ref, bias_ref, o_ref, oc_ref, *, rows):
    lane = lax.broadcasted_iota(jnp.int32, (1, 2 * NA_HD), 1)
    first = lane < NA_HD
    kc = kc_ref[0]
    vc = vc_ref[0]
    zero = jnp.zeros((), q_ref.dtype)

    def stack(q):
        return jnp.concatenate([jnp.where(first, q, zero), jnp.where(first, zero, q)], axis=0)

    def unstack(o):
        n = o.shape[0] // 2
        return jnp.where(first, o[:n], o[n:])

    oc_ref[0] = unstack(_softmax_pv([_dot_nt(stack(qc_ref[0]), kc)], [vc])).astype(oc_ref.dtype)

    def row_group(it, carry):
        work = []
        for j in range(NA_ROWS):
            r = it * NA_ROWS + j
            r0 = jnp.clip(r - WIN_R // 2, 0, rows - WIN_R)
            delta = r0 - r + WIN_R - 1
            qs = pl.multiple_of(r * GRID_W, GRID_W)
            ks = pl.multiple_of(r0 * GRID_W, GRID_W)
            q2 = stack(q_ref[0, pl.ds(qs, GRID_W), :])
            kl = k_ref[0, pl.ds(ks, WIN_R * GRID_W), :]
            vl = v_ref[0, pl.ds(ks, WIN_R * GRID_W), :]
            work.append((qs, vl, _dot_nt(q2, kl) + bias_ref[0, 0, delta], _dot_nt(q2, kc)))
        outs = [unstack(_softmax_pv([s_loc, s_ctx], [vl, vc])) for _, vl, s_loc, s_ctx in work]
        for (qs, _, _, _), o in zip(work, outs):
            o_ref[0, pl.ds(qs, GRID_W), :] = o.astype(o_ref.dtype)
        return carry

    lax.fori_loop(0, rows // NA_ROWS, row_group, 0)


def _natten_bias(rpb):
    qcol = jnp.arange(GRID_W)[:, None]
    kcol = jnp.arange(GRID_W)[None, :]
    wstart = jnp.clip(qcol - WIN_C // 2, 0, GRID_W - WIN_C)
    in_win = (kcol >= wstart) & (kcol < wstart + WIN_C)
    dc = jnp.clip(kcol - qcol + WIN_C - 1, 0, 2 * WIN_C - 2)
    dr = jnp.arange(WIN_R)[:, None] + jnp.arange(WIN_R)[None, :]
    t = rpb[:, :, dc][:, dr]
    t = jnp.where(in_win[None, None, None], t.astype(jnp.float32), MASK_VALUE)
    t = t.transpose(0, 1, 3, 2, 4).reshape(NA_HEADS // 2, 2, WIN_R, GRID_W, WIN_R * GRID_W)
    return t.transpose(0, 2, 1, 3, 4).reshape(NA_HEADS // 2, WIN_R, 2 * GRID_W, WIN_R * GRID_W)


def natten(z, zc, bias, li):
    b, s, _ = z.shape
    l = zc.shape[1]
    rows = s // GRID_W
    assert rows >= WIN_R
    pairs = NA_HEADS // 2
    per_tile = GROUP_W // (2 * NA_HD)
    spec = lambda n, t: pl.BlockSpec((1, n, 2 * NA_HD), lambda bi, p: (bi, 0, t * per_tile + p))
    return pl.pallas_call(
        functools.partial(_natten_kernel, rows=rows),
        grid=(b, pairs),
        in_specs=[spec(s, T_NA_Q), spec(s, T_NA_K), spec(s, T_NA_V),
                  spec(l, T_NA_Q), spec(l, T_NA_K), spec(l, T_NA_V),
                  pl.BlockSpec((1, 1, WIN_R, 2 * GRID_W, WIN_R * GRID_W), lambda bi, p: (li, p, 0, 0, 0))],
        out_specs=[pl.BlockSpec((1, s, 2 * NA_HD), lambda bi, p: (bi, 0, p)),
                   pl.BlockSpec((1, l, 2 * NA_HD), lambda bi, p: (bi, 0, p))],
        out_shape=[jax.ShapeDtypeStruct((b, s, GROUP_W), jnp.bfloat16),
                   jax.ShapeDtypeStruct((b, l, GROUP_W), jnp.bfloat16)],
        compiler_params=_params("parallel", "parallel"),
        name="natten",
    )(z, z, z, zc, zc, zc, bias)


HALO = 16


def _pack_halves(h):
    half = h.shape[1] // 2
    lo = pltpu.bitcast(_bf16(h[:, :half]).astype(jnp.float32), jnp.uint32)
    hi = pltpu.bitcast(_bf16(h[:, half:]).astype(jnp.float32), jnp.uint32)
    return (lo >> 16) | (hi & jnp.uint32(0xFFFF0000))


def _unpack_halves(w):
    lo = pltpu.bitcast(w << 16, jnp.float32)
    hi = pltpu.bitcast(w & jnp.uint32(0xFFFF0000), jnp.float32)
    return _bf16(jnp.concatenate([lo, hi], axis=1))


def _sigmoid(v):
    return 0.5 * jnp.tanh(0.5 * v) + 0.5


def _merge_kernel(zb_ref, zc_ref, zu_ref, pc_ref, pu_ref, nc_ref, nu_ref, cw_ref,
                  ofw_ref, obw_ref, zg_ref, hgn_ref, g128_ref, yna_ref, *rest):
    n_gate = len(rest) - 13
    gate_refs, rest = rest[:n_gate], rest[n_gate:]
    (x_ref, gate_ref, wa_ref, wb_ref, wc_ref, wo_ref, n2_ref, shift_ref, scale_ref, wr_ref,
     xo_ref, hp_ref, lg_ref) = rest
    per_gate = n_gate // N_GATES
    gate = lambda k: _sigmoid(jnp.concatenate(
        [r[0] for r in gate_refs[k * per_gate:(k + 1) * per_gate]], axis=1).astype(jnp.float32))
    i = pl.program_id(1)
    tm = zb_ref.shape[1]
    f32 = jnp.float32

    v = zc_ref[0].astype(f32) * zu_ref[0].astype(f32)
    vp = pc_ref[0, HALO - 1:HALO].astype(f32) * pu_ref[0, HALO - 1:HALO].astype(f32)
    vn = nc_ref[0, 0:1].astype(f32) * nu_ref[0, 0:1].astype(f32)
    vp = jnp.where(i == 0, 0.0, vp)
    vn = jnp.where(i == pl.num_programs(1) - 1, 0.0, vn)
    row = lax.broadcasted_iota(jnp.int32, v.shape, 0)
    v_prev = jnp.where(row == 0, vp, pltpu.roll(v, 1, 0))
    v_next = jnp.where(row == tm - 1, vn, pltpu.roll(v, tm - 1, 0))
    cw = cw_ref[...]
    y_cv = zb_ref[0].astype(f32) * (cw[0:1] * v_prev + cw[1:2] * v + cw[2:3] * v_next)

    o = ofw_ref[0] + obw_ref[0]
    y_hg = _group_rms(o, g128_ref[...], hgn_ref[...]) * _silu(zg_ref[0].astype(f32))

    m = (gate(0) * _dot(_bf16(y_cv), wa_ref[0]) + gate(1) * _dot(_bf16(y_hg), wb_ref[0])
         + gate(2) * _dot(yna_ref[0], wc_ref[0]))
    x_new = x_ref[0] + gate_ref[0, 0] * _dot(_bf16(m), wo_ref[0])
    xo_ref[0] = x_new

    h2 = _modnorm(x_new, n2_ref[...], shift_ref[0, 0], scale_ref[0, 0])
    hp_ref[0] = _pack_halves(h2)
    lg_ref[0] = _dot_nt(wr_ref[0], _bf16(h2))


def merge(z, o_fw, o_bw, y_na, x, mod, conv_w, hg_norm, w_a, w_b, w_c, w_o, norm2, w_rt, li, tm):
    b, l, d = x.shape
    e = w_rt.shape[1]
    layer = lambda a: pl.BlockSpec((1,) + a.shape[1:], lambda bi, i: (li,) + (0,) * (a.ndim - 1))
    nt = l // tm
    per = tm // HALO
    n_halo = l // HALO
    assert d % GROUP_W == 0
    n_gate = N_GATES * d // GROUP_W
    bm = (lambda bi: bi) if mod.shape[0] == b else (lambda bi: 0)
    zt = lambda t: pl.BlockSpec((1, tm, GROUP_W), lambda bi, i: (bi, i, t))
    zprev = lambda t: pl.BlockSpec((1, HALO, GROUP_W), lambda bi, i: (bi, jnp.maximum(i * per - 1, 0), t))
    znext = lambda t: pl.BlockSpec((1, HALO, GROUP_W),
                                   lambda bi, i: (bi, jnp.minimum((i + 1) * per, n_halo - 1), t))
    act = lambda w: pl.BlockSpec((1, tm, w), lambda bi, i: (bi, i, 0))
    modrow = lambda k: pl.BlockSpec((1, 1, 1, d), lambda bi, i: (bm(bi), k, 0, 0))
    full = lambda a: pl.BlockSpec(a.shape, lambda bi, i: (0,) * a.ndim)
    g128 = _block_diag_mean(GROUP_W, HG_D)
    hgn = jnp.tile(hg_norm.reshape(1, -1), (1, GROUP_W // HG_D))
    n2 = norm2.reshape(1, d)
    return pl.pallas_call(
        _merge_kernel,
        grid=(b, nt),
        in_specs=[zt(T_CONV_B), zt(T_CONV_C), zt(T_CONV_U),
                  zprev(T_CONV_C), zprev(T_CONV_U), znext(T_CONV_C), znext(T_CONV_U), full(conv_w),
                  act(GROUP_W), act(GROUP_W), zt(T_HG_G), full(hgn), full(g128), act(GROUP_W)]
                 + [zt(T_GATES + k) for k in range(n_gate)]
                 + [act(d), modrow(2), layer(w_a), layer(w_b), layer(w_c), layer(w_o),
                  full(n2), modrow(3), modrow(4), layer(w_rt)],
        out_specs=[act(d), act(d // 2), pl.BlockSpec((1, e, tm), lambda bi, i: (bi, 0, i))],
        out_shape=[jax.ShapeDtypeStruct((b, l, d), jnp.float32),
                   jax.ShapeDtypeStruct((b, l, d // 2), jnp.uint32),
                   jax.ShapeDtypeStruct((b, e, l), jnp.float32)],
        compiler_params=_params("parallel", "arbitrary"),
        name="merge",
    )(z, z, z, z, z, z, z, conv_w, o_fw, o_bw, z, hgn, g128, y_na, *([z] * n_gate),
      x, mod, w_a, w_b, w_c, w_o, n2, mod, mod, w_rt)


LANES = 128
ROUTE_K_CHUNK = 1024


def _count(mask):
    return jnp.sum(jnp.where(mask, 1.0, 0.0), axis=1, keepdims=True)


def _route_kernel(lg_ref, tmat_ref, excl_ref, slot_ref, idx_ref, wgt_ref, starts_ref, table_ref, *, cap, tile):
    f32 = jnp.float32
    lg = lg_ref[0]
    e, s = lg.shape
    ex = jnp.exp(lg - jnp.max(lg, axis=0, keepdims=True))
    aff = ex / jnp.sum(ex, axis=0, keepdims=True)
    bits = pltpu.bitcast(aff, jnp.int32)

    def thr_bit(it, thr):
        cand = thr | (jnp.int32(1) << (30 - it))
        return jnp.where(_count(bits >= cand) >= cap, cand, thr)

    thr = lax.fori_loop(0, 31, thr_bit, jnp.zeros((e, 1), jnp.int32))
    gt = bits > thr
    eq = bits == thr
    need = cap - _count(gt)
    tok = lax.broadcasted_iota(jnp.int32, (e, s), 1)
    nbits = s.bit_length()

    def end_bit(it, end):
        cand = end + (jnp.int32(1) << (nbits - 1 - it))
        ok = (cand <= s) & (_count(eq & (tok < cand)) <= need)
        return jnp.where(ok, cand, end)

    end = lax.fori_loop(0, nbits, end_bit, jnp.zeros((e, 1), jnp.int32))
    sel = gt | (eq & (tok < end))

    self = jnp.where(sel, 1.0, 0.0)
    offs = jnp.zeros((e, 1), f32)
    pieces = []
    tile_lane = lax.broadcasted_iota(jnp.int32, (e, LANES), 1)
    starts = jnp.zeros((e, LANES), f32)
    for c in range(s // LANES):
        if (c * LANES) % tile == 0:
            starts = jnp.where(tile_lane == c * LANES // tile, offs, starts)
        blk = self[:, c * LANES:(c + 1) * LANES]
        pieces.append(_dot(_bf16(blk), excl_ref[...]) + offs)
        offs = offs + jnp.sum(blk, axis=1, keepdims=True)
    starts_ref[0] = jnp.where(tile_lane == s // tile, offs, starts).astype(jnp.int32)
    slot_ref[0] = jnp.where(sel, jnp.concatenate(pieces, axis=1), -1.0)

    a_t = aff.T
    plane = lax.broadcasted_iota(jnp.int32, (e, LANES), 1) - lax.broadcasted_iota(jnp.int32, (e, LANES), 0)
    table = tmat_ref[...].astype(f32)
    for piece, part in enumerate(_split3(a_t)):
        table = table + _dot(part, _bf16(jnp.where(plane == 2 + piece * e, 1.0, 0.0)))
    table_ref[...] = _bf16(table)

    piota = lax.broadcasted_iota(jnp.int32, (cap, 1), 0).astype(f32)
    lane = lax.broadcasted_iota(jnp.int32, (cap, LANES), 1)
    kc = min(ROUTE_K_CHUNK, s)
    idx_ref[0] = jnp.zeros((cap, LANES), jnp.int32)
    wgt_ref[0] = jnp.zeros((cap, LANES), f32)

    def expert(ee, carry):
        res = jnp.zeros((cap, LANES), f32)
        for c in range(s // kc):
            row = slot_ref[0, pl.ds(ee, 1), c * kc:(c + 1) * kc]
            onehot = _bf16(jnp.where(row == piota, 1.0, 0.0))
            res = res + _dot(onehot, table_ref[c * kc:(c + 1) * kc, :])
        tok_idx = (res[:, 0:1] * 64.0 + res[:, 1:2]).astype(jnp.int32)
        idx_ref[0] = jnp.where(lane == ee, tok_idx, idx_ref[0])
        wgt = sum(pltpu.roll(res, LANES - 2 - piece * e, 1) for piece in range(3))
        wgt_ref[0] = jnp.where(lane == ee, wgt, wgt_ref[0])
        return carry

    lax.fori_loop(0, e, expert, 0)


def route(lg, cap, tile):
    b, e, s = lg.shape
    assert tile % LANES == 0 and s % tile == 0 and s // tile < LANES
    t = jnp.arange(s)
    tmat = jnp.zeros((s, LANES), jnp.bfloat16).at[:, 0].set(_bf16(t // 64)).at[:, 1].set(_bf16(t % 64))
    a = jnp.arange(LANES)
    excl = _bf16(a[:, None] < a[None, :])
    spec = pl.BlockSpec((1, e, s), lambda bi: (bi, 0, 0))
    per_slot = pl.BlockSpec((1, cap, LANES), lambda bi: (bi, 0, 0))
    slot, idx, wgt, starts = pl.pallas_call(
        functools.partial(_route_kernel, cap=cap, tile=tile),
        grid=(b,),
        in_specs=[spec, pl.BlockSpec((s, LANES), lambda bi: (0, 0)), pl.BlockSpec((LANES, LANES), lambda bi: (0, 0))],
        out_specs=[spec, per_slot, per_slot, pl.BlockSpec((1, e, LANES), lambda bi: (bi, 0, 0))],
        out_shape=[jax.ShapeDtypeStruct((b, e, s), jnp.float32), jax.ShapeDtypeStruct((b, cap, LANES), jnp.int32),
                   jax.ShapeDtypeStruct((b, cap, LANES), jnp.float32), jax.ShapeDtypeStruct((b, e, LANES), jnp.int32)],
        scratch_shapes=[pltpu.VMEM((s, LANES), jnp.bfloat16)],
        compiler_params=_params("parallel"),
        name="route",
    )(lg, tmat, excl)
    return slot, idx[:, :, :e].transpose(0, 2, 1), wgt, starts[:, :, :s // tile + 1]


def _gather_kernel(idx_ref, h_ref, o_ref, *, cap):
    def body(j, carry):
        t = idx_ref[0, 0, j]
        o_ref[0, 0, pl.ds(j, 1), :] = h_ref[0, pl.ds(t, 1), :]
        return carry

    lax.fori_loop(0, cap, body, 0, unroll=8)


def gather(idx, hp, out_shape, out_index):
    b, e, cap = idx.shape
    _, s, w = hp.shape
    return pl.pallas_call(
        functools.partial(_gather_kernel, cap=cap),
        grid=(b, e),
        in_specs=[pl.BlockSpec((1, 1, cap), lambda bi, ei: (bi * e + ei, 0, 0), memory_space=pltpu.SMEM),
                  pl.BlockSpec((1, s, w), lambda bi, ei: (bi, 0, 0))],
        out_specs=pl.BlockSpec((1, 1, cap, w), out_index),
        out_shape=jax.ShapeDtypeStruct(out_shape, jnp.uint32),
        compiler_params=_params("parallel", "arbitrary"),
        name="gather",
    )(idx.reshape(b * e, 1, cap), hp)


FFN_CHUNK = 512


def _ffn_body(xw, wgt, wg_ref, wu_ref, wd_ref):
    x = _unpack_halves(xw)
    f = wg_ref.shape[3]
    fc = min(FFN_CHUNK, f)
    acc = jnp.zeros((x.shape[0], wd_ref.shape[3]), jnp.float32)
    for c in range(f // fc):
        a = _dot(x, wg_ref[0, 0, :, c * fc:(c + 1) * fc])
        u = _dot(x, wu_ref[0, 0, :, c * fc:(c + 1) * fc])
        acc = acc + _dot(_bf16(_silu(a) * u), wd_ref[0, 0, c * fc:(c + 1) * fc, :])
    mine = lax.broadcasted_iota(jnp.int32, wgt.shape, 1) == pl.program_id(0)
    return _bf16(acc * jnp.sum(jnp.where(mine, wgt, 0.0), axis=1, keepdims=True))


def _ffn_kernel(xl_ref, al_ref, wg_ref, wu_ref, wd_ref, yl_ref):
    yl_ref[0, 0] = _ffn_body(xl_ref[0, 0], al_ref[0], wg_ref, wu_ref, wd_ref)


def _ffn_ctx_kernel(xl_ref, al_ref, xc_ref, ac_ref, wg_ref, wu_ref, wd_ref, yl_ref, yc_ref, *, nb):
    is_ctx = pl.program_id(1) == nb
    y = _ffn_body(jnp.where(is_ctx, xc_ref[0, 0], xl_ref[0, 0]), jnp.where(is_ctx, ac_ref[0], al_ref[0]),
                  wg_ref, wu_ref, wd_ref)

    @pl.when(jnp.logical_not(is_ctx))
    def _():
        yl_ref[0, 0] = y

    @pl.when(is_ctx)
    def _():
        yc_ref[0, 0] = y


def expert_ffn(xe, wgt, xe_c, wgt_c, w_g, w_u, w_d, li):
    nb, e, cap, w = xe.shape
    d, f = w_g.shape[2:]
    wspec = lambda shape: pl.BlockSpec((1, 1) + shape, lambda ei, bi: (li, ei, 0, 0))
    weights = [wspec((d, f)), wspec((d, f)), wspec((f, d))]
    lat = lambda width: pl.BlockSpec((1, 1, cap, width), lambda ei, bi: (jnp.minimum(bi, nb - 1), ei, 0, 0))
    lat_w = pl.BlockSpec((1, cap, LANES), lambda ei, bi: (jnp.minimum(bi, nb - 1), 0, 0))
    cspec = lambda width: pl.BlockSpec((1, 1, cap, width), lambda ei, bi: (0, ei, 0, 0))
    ctx_w = pl.BlockSpec((1, cap, LANES), lambda ei, bi: (0, 0, 0))
    if xe_c is None:
        return pl.pallas_call(
            _ffn_kernel, grid=(e, nb),
            in_specs=[lat(w), lat_w] + weights, out_specs=lat(d),
            out_shape=jax.ShapeDtypeStruct((nb, e, cap, d), jnp.bfloat16),
            compiler_params=_params("parallel", "arbitrary"), name="expert_ffn",
        )(xe, wgt, w_g, w_u, w_d), None
    assert xe_c.shape == (1, e, cap, w) and wgt_c.shape == (1, cap, LANES)
    return pl.pallas_call(
        functools.partial(_ffn_ctx_kernel, nb=nb), grid=(e, nb + 1),
        in_specs=[lat(w), lat_w, cspec(w), ctx_w] + weights, out_specs=[lat(d), cspec(d)],
        out_shape=[jax.ShapeDtypeStruct((nb, e, cap, d), jnp.bfloat16),
                   jax.ShapeDtypeStruct((1, e, cap, d), jnp.bfloat16)],
        compiler_params=_params("parallel", "arbitrary"), name="expert_ffn_ctx",
    )(xe, wgt, xe_c, wgt_c, w_g, w_u, w_d)


COMBINE_WIN = 128
SLOT_ALIGN = 16


def _combine_kernel(starts_ref, slot_ref, ye_ref, x_ref, gate_ref, o_ref, *, slots_per_sample):
    f32 = jnp.float32
    bi, ti = pl.program_id(0), pl.program_id(1)
    n_exp, n_slots, d = ye_ref.shape[1:]
    assert n_exp % 2 == 0
    win = min(COMBINE_WIN, n_slots)
    base = bi * slots_per_sample
    slot_t = slot_ref[0]
    tm = slot_t.shape[0]
    slot_t = jnp.where(slot_t < 0.0, -1.0, slot_t + base.astype(f32))
    expert_lane = lax.broadcasted_iota(jnp.int32, slot_t.shape, 1)
    lane = lax.broadcasted_iota(jnp.int32, (tm, win), 1).astype(f32)

    def column(e):
        col = jnp.sum(jnp.where(expert_lane == e, slot_t, 0.0), axis=1, keepdims=True)
        return jnp.broadcast_to(col, (tm, win))

    def first_window(e):
        lo = base + starts_ref[bi, e, ti]
        return jnp.minimum(lo // SLOT_ALIGN * SLOT_ALIGN, n_slots - win)

    def onehot(slot_b, nominal, start):
        hit = (slot_b - start.astype(f32) == lane) & (slot_b >= nominal.astype(f32))
        return _bf16(jnp.where(hit, 1.0, 0.0))

    def rows(e, start):
        return ye_ref[0, e, pl.ds(pl.multiple_of(start, SLOT_ALIGN), win), :]

    for e in range(0, n_exp, 2):
        ws = [first_window(e), first_window(e + 1)]
        picks = jnp.concatenate([onehot(column(e + j), ws[j], ws[j]) for j in range(2)], axis=1)
        term = _dot(picks, jnp.concatenate([rows(e + j, ws[j]) for j in range(2)], axis=0))
        if e == 0:
            o_ref[0] = term
        else:
            o_ref[0] += term

    def expert(e, carry):
        ws = first_window(e)
        hi = base + starts_ref[bi, e, ti + 1]

        def window(k, c2):
            nominal = ws + k * win
            start = jnp.minimum(nominal, n_slots - win)
            o_ref[0] += _dot(onehot(column(e), nominal, start), rows(e, start))
            return c2

        return lax.fori_loop(1, (hi - ws + win - 1) // win, window, carry)

    lax.fori_loop(0, n_exp, expert, 0)
    o_ref[0] = x_ref[0] + gate_ref[0, 0] * o_ref[0]


def combine(starts, slot_t, ye, x, mod, tm):
    b, l, d = x.shape
    e = slot_t.shape[2]
    pooled = ye.shape[0] == 1 and b > 1
    bm = (lambda bi: bi) if mod.shape[0] == b else (lambda bi: 0)
    tok = lambda w: pl.BlockSpec((1, tm, w), lambda bi, i, st: (bi, i, 0))
    ye_spec = pl.BlockSpec((1,) + ye.shape[1:], lambda bi, i, st: (0 if pooled else bi, 0, 0, 0),
                           pipeline_mode=pl.Buffered(1))
    return pl.pallas_call(
        functools.partial(_combine_kernel, slots_per_sample=ye.shape[2] // b if pooled else 0),
        grid_spec=pltpu.PrefetchScalarGridSpec(
            num_scalar_prefetch=1,
            grid=(b, l // tm),
            in_specs=[tok(e), ye_spec, tok(d),
                      pl.BlockSpec((1, 1, 1, d), lambda bi, i, st: (bm(bi), N_MOD - 1, 0, 0))],
            out_specs=tok(d)),
        out_shape=jax.ShapeDtypeStruct((b, l, d), jnp.float32),
        compiler_params=_params("parallel", "arbitrary"),
        name="combine",
    )(starts, slot_t, ye, x, mod)


TM_IN = 2048
TM_MERGE = 256
TM_COMBINE = 512
HG_ROWS = 512


def _route_and_gather(hp, lg, xe_shape, xe_index, cap, tile):
    slot, idx, wgt, starts = route(lg, cap, tile)
    xe = gather(idx, hp, xe_shape, xe_index)
    return (starts, slot.transpose(0, 2, 1)), xe, wgt


def kernel(x, c, ctx, c_ctx, w_mod, b_mod, norm1, w_in, conv_w, hg_lb_logits, hg_norm, na_q_norm, na_k_norm, na_rpb,
           w_br_a, w_br_b, w_br_c, w_out, norm2, w_router, w_e_gate, w_e_up, w_e_down):
    b, s, d = x.shape
    l = ctx.shape[1]
    depth = w_mod.shape[0]
    e = w_router.shape[-1]
    cap = CAP_FACTOR * s // e
    cap_c = CAP_FACTOR * l // e
    assert b * cap_c == cap, "context rows of all samples fill one expert tile"

    lb_sm = jax.nn.softmax(hg_lb_logits.astype(jnp.float32), axis=0)
    lb_all = jnp.cumsum(lb_sm, axis=0) - lb_sm[0]
    rows = -(-(b + 1) // 8) * 8
    cc = jnp.zeros((rows, d), jnp.float32).at[:b].set(c).at[b].set(c_ctx)
    mod_all = modulation(cc, w_mod, b_mod).reshape(depth, rows, N_MOD, 1, d)

    w_in_r = _bf16(w_in)
    w_a, w_b, w_c, w_o = _bf16(w_br_a), _bf16(w_br_b), _bf16(w_br_c), _bf16(w_out)
    w_rt = _bf16(jnp.swapaxes(w_router, 1, 2))
    w_g, w_u, w_d = _bf16(w_e_gate), _bf16(w_e_up), _bf16(w_e_down)
    bias = jax.vmap(_natten_bias)(na_rpb)
    s0 = jnp.zeros((b, 2, HG_HEADS, HG_D, HG_D), jnp.float32)
    hg_rows = min(HG_ROWS, s)
    tm_in = min(TM_IN, s)
    tm_cmb = min(TM_COMBINE, s)

    xc = ctx
    for li in range(depth):
        last = li == depth - 1
        mod = mod_all[li, :b]
        mod_c = mod_all[li, b:b + 1]
        z = input_projection(x, mod, norm1[li], w_in_r, li, na_q_norm[li], na_k_norm[li], tm_in)
        zc = input_projection(xc.reshape(1, b * l, d), mod_c, norm1[li], w_in_r, li, na_q_norm[li], na_k_norm[li],
                              min(TM_IN, b * l)).reshape(b, l, -1)
        oc_fw, oc_bw, s_ctx = hgrn_scan(zc, lb_all[li], s0, l)
        o_fw, o_bw, _ = hgrn_scan(z, lb_all[li], s_ctx, hg_rows)
        y_na, yc_na = natten(z, zc, bias, li)

        mw = (conv_w[li], hg_norm[li], w_a, w_b, w_c, w_o, norm2[li], w_rt, li)
        x, hp, lg = merge(z, o_fw, o_bw, y_na, x, mod, *mw, TM_MERGE)
        plan, xe, wgt = _route_and_gather(hp, lg, (b, e, cap, d // 2), lambda bi, ei: (bi, ei, 0, 0), cap, tm_cmb)
        if last:
            ye, _ = expert_ffn(xe, wgt, None, None, w_g, w_u, w_d, li)
        else:
            xc, hpc, lgc = merge(zc, oc_fw, oc_bw, yc_na, xc, mod_c, *mw, l)
            plan_c, xe_c, wgt_c = _route_and_gather(hpc, lgc, (1, e, cap, d // 2), lambda bi, ei: (0, ei, bi, 0),
                                                    cap_c, l)
            ye, ye_c = expert_ffn(xe, wgt, xe_c, wgt_c.reshape(1, cap, LANES), w_g, w_u, w_d, li)
            xc = combine(*plan_c, ye_c, xc, mod_c, l)
        x = combine(*plan, ye, x, mod, tm_cmb)
    return x
```

```python
import functools

import jax
import jax.numpy as jnp
from jax import lax
from jax.experimental import pallas as pl
from jax.experimental.pallas import tpu as pltpu

N_MOD = 6
EPS = 1e-6
MASK_VALUE = -1e30
GRID_W = 64
GROUP_W = 512
HG_HEADS = 4
HG_D = 128
HG_CHUNK = 64
HG_SUB = 8
LOG2_E = 1.4426950408889634
NA_HEADS = 8
NA_HD = 64
WIN_R = 8
WIN_C = 16
CAP_FACTOR = 2
T_CONV_B, T_CONV_C, T_CONV_U, T_HG_Q, T_HG_I, T_HG_FF, T_HG_FB, T_HG_G, T_NA_Q, T_NA_K, T_NA_V, T_GATES = range(12)
N_GATES = 3

VMEM_LIMIT_BYTES = 48 * 1024 * 1024


def _params(*semantics):
    return pltpu.CompilerParams(dimension_semantics=semantics, vmem_limit_bytes=VMEM_LIMIT_BYTES)


def _silu(v):
    return v * jax.nn.sigmoid(v)


def _bf16(v):
    return v.astype(jnp.bfloat16)


def _dot(a, b):
    return jnp.dot(a, b, preferred_element_type=jnp.float32)


def _dot_nt(a, b):
    return lax.dot_general(a, b, (((1,), (1,)), ((), ())), preferred_element_type=jnp.float32)


def _mod_kernel(c_ref, w_ref, b_ref, o_ref):
    cond = _bf16(_silu(c_ref[...]))
    o_ref[0] = _dot(cond, _bf16(w_ref[0])) + b_ref[0]


def modulation(cc, w_mod, b_mod):
    depth, d, nm = w_mod.shape
    r = cc.shape[0]
    tn = d
    return pl.pallas_call(
        _mod_kernel,
        grid=(depth, nm // tn),
        in_specs=[pl.BlockSpec((r, d), lambda l, j: (0, 0)),
                  pl.BlockSpec((1, d, tn), lambda l, j: (l, 0, j)),
                  pl.BlockSpec((1, 1, tn), lambda l, j: (l, 0, j))],
        out_specs=pl.BlockSpec((1, r, tn), lambda l, j: (l, 0, j)),
        out_shape=jax.ShapeDtypeStruct((depth, r, nm), jnp.float32),
        compiler_params=_params("parallel", "parallel"),
        name="modulation",
    )(cc, w_mod, b_mod.reshape(depth, 1, nm))


def _modnorm(x, w, shift, scale):
    ms = jnp.mean(x * x, axis=-1, keepdims=True)
    return (x * lax.rsqrt(ms + EPS) * w) * (1.0 + scale) + shift


def _group_rms(acc, gmat, w_tiled):
    ms = _dot(_bf16(acc * acc), gmat)
    return acc * lax.rsqrt(ms + EPS) * w_tiled


def _inproj_kernel(x_ref, shift_ref, scale_ref, nw_ref, w_ref, g64_ref, qn_ref, kn_ref, o_ref, h_ref):
    j = pl.program_id(2)

    @pl.when(j == 0)
    def _():
        h_ref[...] = _bf16(_modnorm(x_ref[0], nw_ref[...], shift_ref[0, 0], scale_ref[0, 0]))

    acc = lambda: _dot(h_ref[...], w_ref[0])
    plain = (j != T_HG_Q) & (j != T_NA_Q) & (j != T_NA_K) & (j < T_GATES)

    @pl.when(plain)
    def _():
        o_ref[0] = acc().astype(o_ref.dtype)

    @pl.when(j >= T_GATES)
    def _():
        o_ref[0] = _sigmoid(acc()).astype(o_ref.dtype)

    @pl.when(j == T_HG_Q)
    def _():
        o_ref[0] = (_silu(acc()) * (HG_D ** -0.5)).astype(o_ref.dtype)

    @pl.when(j == T_NA_Q)
    def _():
        o_ref[0] = (_group_rms(acc(), g64_ref[...], qn_ref[...]) * (NA_HD ** -0.5)).astype(o_ref.dtype)

    @pl.when(j == T_NA_K)
    def _():
        o_ref[0] = _group_rms(acc(), g64_ref[...], kn_ref[...]).astype(o_ref.dtype)


def _block_diag_mean(n, group):
    idx = jnp.arange(n) // group
    return _bf16(jnp.where(idx[:, None] == idx[None, :], 1.0 / group, 0.0))


def input_projection(x, mod, norm_w, w_in, li, qn, kn, tm):
    bx, s, d = x.shape
    n = w_in.shape[2]
    tn = GROUP_W
    g64 = _block_diag_mean(tn, NA_HD)
    tile = lambda v: jnp.tile(v.reshape(1, -1), (1, tn // v.shape[-1]))
    return pl.pallas_call(
        _inproj_kernel,
        grid=(bx, s // tm, n // tn),
        in_specs=[pl.BlockSpec((1, tm, d), lambda b, i, j: (b, i, 0)),
                  pl.BlockSpec((1, 1, 1, d), lambda b, i, j: (b, 0, 0, 0)),
                  pl.BlockSpec((1, 1, 1, d), lambda b, i, j: (b, 1, 0, 0)),
                  pl.BlockSpec((1, d), lambda b, i, j: (0, 0)),
                  pl.BlockSpec((1, d, tn), lambda b, i, j: (li, 0, j)),
                  pl.BlockSpec((tn, tn), lambda b, i, j: (0, 0)),
                  pl.BlockSpec((1, tn), lambda b, i, j: (0, 0)),
                  pl.BlockSpec((1, tn), lambda b, i, j: (0, 0))],
        out_specs=pl.BlockSpec((1, tm, tn), lambda b, i, j: (b, i, j)),
        out_shape=jax.ShapeDtypeStruct((bx, s, n), jnp.bfloat16),
        scratch_shapes=[pltpu.VMEM((tm, d), jnp.bfloat16)],
        compiler_params=_params("parallel", "parallel", "arbitrary"),
        name="input_projection",
    )(x, mod, mod, norm_w.reshape(1, d), w_in, g64, tile(qn), tile(kn))


def _split3(v):
    h1 = _bf16(v)
    r1 = v - h1.astype(jnp.float32)
    h2 = _bf16(r1)
    h3 = _bf16(r1 - h2.astype(jnp.float32))
    return h1, h2, h3


def _hgrn_gates(f, lb, tri):
    sig = jax.nn.sigmoid(f)
    lg = jnp.log(lb + (1.0 - lb) * sig) * LOG2_E
    k = (1.0 - lb) * (1.0 - sig)
    h1, h2, h3 = _split3(lg)
    return k, _dot(tri, h1) + _dot(tri, h2) + _dot(tri, h3)


HG_GRP = 8
HG_UNROLL = 2


def _hgrn_head_products(q, v, k, bc, st, ones, rev, k_row, bc_row):
    c = HG_CHUNK
    end = 0 if rev else c - 1
    kd = k * jnp.exp2(bc_row(end, c) - bc)
    v_t = _bf16(v.astype(jnp.float32).T)
    st_new = st * jnp.exp2(bc_row(end, st.shape[0])) + _dot(v_t, _bf16(kd))

    row = lax.broadcasted_iota(jnp.int32, (HG_GRP, HG_D), 0)
    a_off, ps, where = [], [], []
    for i in range(c // HG_SUB):
        lo, hi = i * HG_SUB, (i + 1) * HG_SUB
        qi, bi = q[lo:hi], bc[lo:hi]
        prev = None
        if not rev and i > 0:
            ref, prev = lo - 1, slice(0, lo)
        if rev and hi < c:
            ref, prev = hi, slice(hi, c)
        if prev is None:
            a_off.append(None)
        else:
            qt = _bf16(qi * jnp.exp2(bi - bc_row(ref, HG_SUB)))
            kt = k[prev] * jnp.exp2(bc_row(ref, prev.stop - prev.start) - bc[prev])
            pad = jnp.zeros((c - kt.shape[0], HG_D), kt.dtype)
            kt = _bf16(jnp.concatenate([pad, kt] if rev else [kt, pad], axis=0))
            a_off.append(_dot_nt(qt, kt))
        for s in range(lo, hi):
            for g in range(lo // HG_GRP, hi // HG_GRP):
                g_lo = g * HG_GRP
                sees = (g_lo <= s) if rev else (g_lo + HG_GRP > s)
                if not sees:
                    continue
                diff = bc[g_lo:g_lo + HG_GRP] - bc_row(s, HG_GRP)
                if g_lo <= s < g_lo + HG_GRP:
                    valid = (row <= s - g_lo) if rev else (row >= s - g_lo)
                    diff = jnp.where(valid, diff, MASK_VALUE)
                where.append((g, s))
                ps.append(q[g_lo:g_lo + HG_GRP] * k_row(s, HG_GRP) * jnp.exp2(diff))
    rsum = _dot(_bf16(jnp.concatenate(ps, axis=0)), ones)
    o_inter = _dot_nt(_bf16(q * jnp.exp2(bc)), _bf16(st))
    return o_inter, st_new, a_off, rsum, where


def _hgrn_head_output(v, o_inter, a_off, rsum, where):
    c = HG_CHUNK
    lane = lax.broadcasted_iota(jnp.int32, (HG_GRP, c), 1)
    groups = [None] * (c // HG_GRP)
    for i, a in enumerate(a_off):
        for g in range(i * HG_SUB // HG_GRP, (i + 1) * HG_SUB // HG_GRP):
            r = (g - i * HG_SUB // HG_GRP) * HG_GRP
            groups[g] = jnp.zeros((HG_GRP, c), jnp.float32) if a is None else a[r:r + HG_GRP]
    for n, (g, s) in enumerate(where):
        groups[g] = jnp.where(lane == s, rsum[n * HG_GRP:(n + 1) * HG_GRP, :c], groups[g])
    return o_inter + _dot(_bf16(jnp.concatenate(groups, axis=0)), v)


def _hgrn_kernel(qf_ref, vf_ref, ff_ref, qb_ref, vb_ref, fb_ref, lb_ref, s0_ref, tri_ref, ones_ref,
                 of_ref, ob_ref, sfin_ref, st_ref, kb_ref, bb_ref, *, n_chunks):
    cb = pl.program_id(1)

    @pl.when(cb == 0)
    def _():
        st_ref[...] = s0_ref[0]

    ones = ones_ref[...]
    heads = [slice(h * HG_D, (h + 1) * HG_D) for h in range(HG_HEADS)]
    streams = ((qf_ref, vf_ref, ff_ref, of_ref, False), (qb_ref, vb_ref, fb_ref, ob_ref, True))

    def repeat_row(ref, u, di, hs):
        return lambda s, n: jnp.broadcast_to(ref[u, di, s:s + 1, hs], (n, HG_D))

    def chunks(it, carry):
        state = [[st_ref[di, h] for h in range(HG_HEADS)] for di in range(2)]
        loaded = []
        for u in range(HG_UNROLL):
            ci = it * HG_UNROLL + u
            for di, (q_ref, v_ref, f_ref, o_ref, rev) in enumerate(streams):
                r0 = pl.multiple_of(((n_chunks - 1 - ci) if rev else ci) * HG_CHUNK, HG_CHUNK)
                q = q_ref[0, pl.ds(r0, HG_CHUNK), :].astype(jnp.float32)
                v = v_ref[0, pl.ds(r0, HG_CHUNK), :]
                f = f_ref[0, pl.ds(r0, HG_CHUNK), :].astype(jnp.float32)
                k, bc = _hgrn_gates(f, lb_ref[di:di + 1], tri_ref[di])
                kb_ref[u, di] = k
                bb_ref[u, di] = bc
                loaded.append((u, di, o_ref, rev, r0, q, v, k, bc))
        pending = []
        for u, di, o_ref, rev, r0, q, v, k, bc in loaded:
            stage = []
            for h, hs in enumerate(heads):
                res = _hgrn_head_products(q[:, hs], v[:, hs], k[:, hs], bc[:, hs], state[di][h], ones, rev,
                                          repeat_row(kb_ref, u, di, hs), repeat_row(bb_ref, u, di, hs))
                state[di][h] = res[1]
                stage.append(res)
            pending.append((o_ref, r0, v, stage))
        for o_ref, r0, v, stage in pending:
            outs = [_hgrn_head_output(v[:, hs], o_inter, a_off, rsum, where)
                    for hs, (o_inter, _, a_off, rsum, where) in zip(heads, stage)]
            o_ref[0, pl.ds(r0, HG_CHUNK), :] = jnp.concatenate(outs, axis=1)
        for di in range(2):
            for h in range(HG_HEADS):
                st_ref[di, h] = state[di][h]
        return carry

    lax.fori_loop(0, n_chunks // HG_UNROLL, chunks, 0)

    @pl.when(cb == pl.num_programs(1) - 1)
    def _():
        sfin_ref[0] = st_ref[...]


def hgrn_scan(z, lb, s0, tc):
    b, l, _ = z.shape
    nb = l // tc
    ci = jnp.arange(HG_CHUNK)
    tri = _bf16(jnp.stack([ci[None, :] <= ci[:, None], ci[None, :] >= ci[:, None]]))
    ones = jnp.ones((HG_D, HG_D), jnp.bfloat16)
    fwd = lambda t: pl.BlockSpec((1, tc, GROUP_W), lambda bi, c: (bi, c, t))
    bwd = lambda t: pl.BlockSpec((1, tc, GROUP_W), lambda bi, c: (bi, nb - 1 - c, t))
    st_spec = pl.BlockSpec((1, 2, HG_HEADS, HG_D, HG_D), lambda bi, c: (bi, 0, 0, 0, 0))
    return pl.pallas_call(
        functools.partial(_hgrn_kernel, n_chunks=tc // HG_CHUNK),
        grid=(b, nb),
        in_specs=[fwd(T_HG_Q), fwd(T_HG_I), fwd(T_HG_FF), bwd(T_HG_Q), bwd(T_HG_I), bwd(T_HG_FB),
                  pl.BlockSpec((2, GROUP_W), lambda bi, c: (0, 0)),
                  st_spec,
                  pl.BlockSpec((2, HG_CHUNK, HG_CHUNK), lambda bi, c: (0, 0, 0)),
                  pl.BlockSpec((HG_D, HG_D), lambda bi, c: (0, 0))],
        out_specs=[pl.BlockSpec((1, tc, GROUP_W), lambda bi, c: (bi, c, 0)),
                   pl.BlockSpec((1, tc, GROUP_W), lambda bi, c: (bi, nb - 1 - c, 0)), st_spec],
        out_shape=[jax.ShapeDtypeStruct((b, l, GROUP_W), jnp.float32),
                   jax.ShapeDtypeStruct((b, l, GROUP_W), jnp.float32),
                   jax.ShapeDtypeStruct(s0.shape, jnp.float32)],
        scratch_shapes=[pltpu.VMEM((2, HG_HEADS, HG_D, HG_D), jnp.float32),
                        pltpu.VMEM((HG_UNROLL, 2, HG_CHUNK, GROUP_W), jnp.float32),
                        pltpu.VMEM((HG_UNROLL, 2, HG_CHUNK, GROUP_W), jnp.float32)],
        compiler_params=_params("parallel", "arbitrary"),
        name="hgrn_scan",
    )(z, z, z, z, z, z, lb, s0, tri, ones)


NA_ROWS = 4


def _softmax_pv(scores, values):
    m = functools.reduce(jnp.maximum, [jnp.max(s, axis=-1, keepdims=True) for s in scores])
    ps = [jnp.exp(s - m) for s in scores]
    den = functools.reduce(jnp.add, [jnp.sum(p, axis=-1, keepdims=True) for p in ps])
    num = functools.reduce(jnp.add, [_dot(_bf16(p), v) for p, v in zip(ps, values)])
    return num / den


def _natten_kernel(q_ref, k_ref, v_ref, qc_ref, kc_ref, vc_ref, bias_ref, o_ref, oc_ref, *, rows):
    lane = lax.broadcasted_iota(jnp.int32, (1, 2 * NA_HD), 1)
    first = lane < NA_HD
    kc = kc_ref[0]
    vc = vc_ref[0]
    zero = jnp.zeros((), q_ref.dtype)

    def stack(q):
        return jnp.concatenate([jnp.where(first, q, zero), jnp.where(first, zero, q)], axis=0)

    def unstack(o):
        n = o.shape[0] // 2
        return jnp.where(first, o[:n], o[n:])

    oc_ref[0] = unstack(_softmax_pv([_dot_nt(stack(qc_ref[0]), kc)], [vc])).astype(oc_ref.dtype)

    def row_group(it, carry):
        work = []
        for j in range(NA_ROWS):
            r = it * NA_ROWS + j
            r0 = jnp.clip(r - WIN_R // 2, 0, rows - WIN_R)
            delta = r0 - r + WIN_R - 1
            qs = pl.multiple_of(r * GRID_W, GRID_W)
            ks = pl.multiple_of(r0 * GRID_W, GRID_W)
            q2 = stack(q_ref[0, pl.ds(qs, GRID_W), :])
            kl = k_ref[0, pl.ds(ks, WIN_R * GRID_W), :]
            vl = v_ref[0, pl.ds(ks, WIN_R * GRID_W), :]
            work.append((qs, vl, _dot_nt(q2, kl) + bias_ref[0, 0, delta], _dot_nt(q2, kc)))
        outs = [unstack(_softmax_pv([s_loc, s_ctx], [vl, vc])) for _, vl, s_loc, s_ctx in work]
        for (qs, _, _, _), o in zip(work, outs):
            o_ref[0, pl.ds(qs, GRID_W), :] = o.astype(o_ref.dtype)
        return carry

    lax.fori_loop(0, rows // NA_ROWS, row_group, 0)


def _natten_bias(rpb):
    qcol = jnp.arange(GRID_W)[:, None]
    kcol = jnp.arange(GRID_W)[None, :]
    wstart = jnp.clip(qcol - WIN_C // 2, 0, GRID_W - WIN_C)
    in_win = (kcol >= wstart) & (kcol < wstart + WIN_C)
    dc = jnp.clip(kcol - qcol + WIN_C - 1, 0, 2 * WIN_C - 2)
    dr = jnp.arange(WIN_R)[:, None] + jnp.arange(WIN_R)[None, :]
    t = rpb[:, :, dc][:, dr]
    t = jnp.where(in_win[None, None, None], t.astype(jnp.float32), MASK_VALUE)
    t = t.transpose(0, 1, 3, 2, 4).reshape(NA_HEADS // 2, 2, WIN_R, GRID_W, WIN_R * GRID_W)
    return t.transpose(0, 2, 1, 3, 4).reshape(NA_HEADS // 2, WIN_R, 2 * GRID_W, WIN_R * GRID_W)


def natten(z, zc, bias, li):
    b, s, _ = z.shape
    l = zc.shape[1]
    rows = s // GRID_W
    assert rows >= WIN_R
    pairs = NA_HEADS // 2
    per_tile = GROUP_W // (2 * NA_HD)
    spec = lambda n, t: pl.BlockSpec((1, n, 2 * NA_HD), lambda bi, p: (bi, 0, t * per_tile + p))
    return pl.pallas_call(
        functools.partial(_natten_kernel, rows=rows),
        grid=(b, pairs),
        in_specs=[spec(s, T_NA_Q), spec(s, T_NA_K), spec(s, T_NA_V),
                  spec(l, T_NA_Q), spec(l, T_NA_K), spec(l, T_NA_V),
                  pl.BlockSpec((1, 1, WIN_R, 2 * GRID_W, WIN_R * GRID_W), lambda bi, p: (li, p, 0, 0, 0))],
        out_specs=[pl.BlockSpec((1, s, 2 * NA_HD), lambda bi, p: (bi, 0, p)),
                   pl.BlockSpec((1, l, 2 * NA_HD), lambda bi, p: (bi, 0, p))],
        out_shape=[jax.ShapeDtypeStruct((b, s, GROUP_W), jnp.bfloat16),
                   jax.ShapeDtypeStruct((b, l, GROUP_W), jnp.bfloat16)],
        compiler_params=_params("parallel", "parallel"),
        name="natten",
    )(z, z, z, zc, zc, zc, bias)


HALO = 16
MERGE_PARTS = 2


def _pack_halves(h):
    half = h.shape[1] // 2
    lo = pltpu.bitcast(_bf16(h[:, :half]).astype(jnp.float32), jnp.uint32)
    hi = pltpu.bitcast(_bf16(h[:, half:]).astype(jnp.float32), jnp.uint32)
    return (lo >> 16) | (hi & jnp.uint32(0xFFFF0000))


def _unpack_halves(w):
    lo = pltpu.bitcast(w << 16, jnp.float32)
    hi = pltpu.bitcast(w & jnp.uint32(0xFFFF0000), jnp.float32)
    return _bf16(jnp.concatenate([lo, hi], axis=1))


def _sigmoid(v):
    return 0.5 * jnp.tanh(0.5 * v) + 0.5


def _merge_kernel(zb_ref, zc_ref, zu_ref, pc_ref, pu_ref, nc_ref, nu_ref, cw_ref,
                  ofw_ref, obw_ref, zg_ref, hgn_ref, g128_ref, yna_ref, *rest):
    n_gate = len(rest) - 13
    gate_refs, rest = rest[:n_gate], rest[n_gate:]
    (x_ref, gate_ref, wa_ref, wb_ref, wc_ref, wo_ref, n2_ref, shift_ref, scale_ref, wr_ref,
     xo_ref, hp_ref, lg_ref) = rest
    per_gate = n_gate // N_GATES
    i = pl.program_id(1)
    tm = zb_ref.shape[1]
    f32 = jnp.float32

    v = zc_ref[0].astype(f32) * zu_ref[0].astype(f32)
    vp = pc_ref[0, HALO - 1:HALO].astype(f32) * pu_ref[0, HALO - 1:HALO].astype(f32)
    vn = nc_ref[0, 0:1].astype(f32) * nu_ref[0, 0:1].astype(f32)
    vp = jnp.where(i == 0, 0.0, vp)
    vn = jnp.where(i == pl.num_programs(1) - 1, 0.0, vn)
    row = lax.broadcasted_iota(jnp.int32, v.shape, 0)
    v_prev = jnp.where(row == 0, vp, pltpu.roll(v, 1, 0))
    v_next = jnp.where(row == tm - 1, vn, pltpu.roll(v, tm - 1, 0))
    cw = cw_ref[...]
    y_cv = zb_ref[0].astype(f32) * (cw[0:1] * v_prev + cw[1:2] * v + cw[2:3] * v_next)

    parts = [slice(p * tm // MERGE_PARTS, (p + 1) * tm // MERGE_PARTS) for p in range(MERGE_PARTS)]
    o = [ofw_ref[0, rs, :] + obw_ref[0, rs, :] for rs in parts]
    ms = [_dot(_bf16(v * v), g128_ref[...]) for v in o]
    y_hg = [v * lax.rsqrt(m_ + EPS) * hgn_ref[...] * _silu(zg_ref[0, rs, :].astype(f32))
            for v, m_, rs in zip(o, ms, parts)]
    gates = [[jnp.concatenate([r[0, rs, :] for r in gate_refs[k * per_gate:(k + 1) * per_gate]], axis=1).astype(f32)
              for k in range(N_GATES)] for rs in parts]
    m = [g[0] * _dot(_bf16(y_cv[rs]), wa_ref[0]) + g[1] * _dot(_bf16(yh), wb_ref[0])
         + g[2] * _dot(yna_ref[0, rs, :], wc_ref[0]) for g, yh, rs in zip(gates, y_hg, parts)]
    x_new = [x_ref[0, rs, :] + gate_ref[0, 0] * _dot(_bf16(v), wo_ref[0]) for v, rs in zip(m, parts)]
    h2 = [_modnorm(v, n2_ref[...], shift_ref[0, 0], scale_ref[0, 0]) for v in x_new]
    logits = [_dot_nt(wr_ref[0], _bf16(v)) for v in h2]
    for rs, xn, h in zip(parts, x_new, h2):
        xo_ref[0, rs, :] = xn
        hp_ref[0, rs, :] = _pack_halves(h)
    lg_ref[0] = jnp.concatenate(logits, axis=1)


def merge(z, o_fw, o_bw, y_na, x, mod, conv_w, hg_norm, w_a, w_b, w_c, w_o, norm2, w_rt, li, tm):
    b, l, d = x.shape
    e = w_rt.shape[1]
    layer = lambda a: pl.BlockSpec((1,) + a.shape[1:], lambda bi, i: (li,) + (0,) * (a.ndim - 1))
    nt = l // tm
    per = tm // HALO
    n_halo = l // HALO
    assert d % GROUP_W == 0
    n_gate = N_GATES * d // GROUP_W
    bm = (lambda bi: bi) if mod.shape[0] == b else (lambda bi: 0)
    zt = lambda t: pl.BlockSpec((1, tm, GROUP_W), lambda bi, i: (bi, i, t))
    zprev = lambda t: pl.BlockSpec((1, HALO, GROUP_W), lambda bi, i: (bi, jnp.maximum(i * per - 1, 0), t))
    znext = lambda t: pl.BlockSpec((1, HALO, GROUP_W),
                                   lambda bi, i: (bi, jnp.minimum((i + 1) * per, n_halo - 1), t))
    act = lambda w: pl.BlockSpec((1, tm, w), lambda bi, i: (bi, i, 0))
    modrow = lambda k: pl.BlockSpec((1, 1, 1, d), lambda bi, i: (bm(bi), k, 0, 0))
    full = lambda a: pl.BlockSpec(a.shape, lambda bi, i: (0,) * a.ndim)
    g128 = _block_diag_mean(GROUP_W, HG_D)
    hgn = jnp.tile(hg_norm.reshape(1, -1), (1, GROUP_W // HG_D))
    n2 = norm2.reshape(1, d)
    return pl.pallas_call(
        _merge_kernel,
        grid=(b, nt),
        in_specs=[zt(T_CONV_B), zt(T_CONV_C), zt(T_CONV_U),
                  zprev(T_CONV_C), zprev(T_CONV_U), znext(T_CONV_C), znext(T_CONV_U), full(conv_w),
                  act(GROUP_W), act(GROUP_W), zt(T_HG_G), full(hgn), full(g128), act(GROUP_W)]
                 + [zt(T_GATES + k) for k in range(n_gate)]
                 + [act(d), modrow(2), layer(w_a), layer(w_b), layer(w_c), layer(w_o),
                  full(n2), modrow(3), modrow(4), layer(w_rt)],
        out_specs=[act(d), act(d // 2), pl.BlockSpec((1, e, tm), lambda bi, i: (bi, 0, i))],
        out_shape=[jax.ShapeDtypeStruct((b, l, d), jnp.float32),
                   jax.ShapeDtypeStruct((b, l, d // 2), jnp.uint32),
                   jax.ShapeDtypeStruct((b, e, l), jnp.float32)],
        compiler_params=_params("parallel", "arbitrary"),
        name="merge",
    )(z, z, z, z, z, z, z, conv_w, o_fw, o_bw, z, hgn, g128, y_na, *([z] * n_gate),
      x, mod, w_a, w_b, w_c, w_o, n2, mod, mod, w_rt)


LANES = 128
ROUTE_K_CHUNK = 1024


def _count(mask):
    return jnp.sum(jnp.where(mask, 1.0, 0.0), axis=1, keepdims=True)


def _route_kernel(lg_ref, tmat_ref, excl_ref, slot_ref, idx_ref, wgt_ref, starts_ref, table_ref, *, cap, tile):
    f32 = jnp.float32
    lg = lg_ref[0]
    e, s = lg.shape
    ex = jnp.exp(lg - jnp.max(lg, axis=0, keepdims=True))
    aff = ex / jnp.sum(ex, axis=0, keepdims=True)
    bits = pltpu.bitcast(aff, jnp.int32)

    def thr_bit(it, thr):
        cand = thr | (jnp.int32(1) << (30 - it))
        return jnp.where(_count(bits >= cand) >= cap, cand, thr)

    thr = lax.fori_loop(0, 31, thr_bit, jnp.zeros((e, 1), jnp.int32))
    gt = bits > thr
    eq = bits == thr
    need = cap - _count(gt)
    tok = lax.broadcasted_iota(jnp.int32, (e, s), 1)
    nbits = s.bit_length()

    def end_bit(it, end):
        cand = end + (jnp.int32(1) << (nbits - 1 - it))
        ok = (cand <= s) & (_count(eq & (tok < cand)) <= need)
        return jnp.where(ok, cand, end)

    end = lax.fori_loop(0, nbits, end_bit, jnp.zeros((e, 1), jnp.int32))
    sel = gt | (eq & (tok < end))

    self = jnp.where(sel, 1.0, 0.0)
    offs = jnp.zeros((e, 1), f32)
    pieces = []
    tile_lane = lax.broadcasted_iota(jnp.int32, (e, LANES), 1)
    starts = jnp.zeros((e, LANES), f32)
    for c in range(s // LANES):
        if (c * LANES) % tile == 0:
            starts = jnp.where(tile_lane == c * LANES // tile, offs, starts)
        blk = self[:, c * LANES:(c + 1) * LANES]
        pieces.append(_dot(_bf16(blk), excl_ref[...]) + offs)
        offs = offs + jnp.sum(blk, axis=1, keepdims=True)
    starts_ref[0] = jnp.where(tile_lane == s // tile, offs, starts).astype(jnp.int32)
    slot_ref[0] = jnp.where(sel, jnp.concatenate(pieces, axis=1), -1.0)

    a_t = aff.T
    plane = lax.broadcasted_iota(jnp.int32, (e, LANES), 1) - lax.broadcasted_iota(jnp.int32, (e, LANES), 0)
    table = tmat_ref[...].astype(f32)
    for piece, part in enumerate(_split3(a_t)):
        table = table + _dot(part, _bf16(jnp.where(plane == 2 + piece * e, 1.0, 0.0)))
    table_ref[...] = _bf16(table)

    piota = lax.broadcasted_iota(jnp.int32, (cap, 1), 0).astype(f32)
    lane = lax.broadcasted_iota(jnp.int32, (cap, LANES), 1)
    kc = min(ROUTE_K_CHUNK, s)
    idx_ref[0] = jnp.zeros((cap, LANES), jnp.int32)
    wgt_ref[0] = jnp.zeros((cap, LANES), f32)

    def expert(ee, carry):
        res = jnp.zeros((cap, LANES), f32)
        for c in range(s // kc):
            row = slot_ref[0, pl.ds(ee, 1), c * kc:(c + 1) * kc]
            onehot = _bf16(jnp.where(row == piota, 1.0, 0.0))
            res = res + _dot(onehot, table_ref[c * kc:(c + 1) * kc, :])
        tok_idx = (res[:, 0:1] * 64.0 + res[:, 1:2]).astype(jnp.int32)
        idx_ref[0] = jnp.where(lane == ee, tok_idx, idx_ref[0])
        wgt = sum(pltpu.roll(res, LANES - 2 - piece * e, 1) for piece in range(3))
        wgt_ref[0] = jnp.where(lane == ee, wgt, wgt_ref[0])
        return carry

    lax.fori_loop(0, e, expert, 0)


def route(lg, cap, tile):
    b, e, s = lg.shape
    assert tile % LANES == 0 and s % tile == 0 and s // tile < LANES
    t = jnp.arange(s)
    tmat = jnp.zeros((s, LANES), jnp.bfloat16).at[:, 0].set(_bf16(t // 64)).at[:, 1].set(_bf16(t % 64))
    a = jnp.arange(LANES)
    excl = _bf16(a[:, None] < a[None, :])
    spec = pl.BlockSpec((1, e, s), lambda bi: (bi, 0, 0))
    per_slot = pl.BlockSpec((1, cap, LANES), lambda bi: (bi, 0, 0))
    slot, idx, wgt, starts = pl.pallas_call(
        functools.partial(_route_kernel, cap=cap, tile=tile),
        grid=(b,),
        in_specs=[spec, pl.BlockSpec((s, LANES), lambda bi: (0, 0)), pl.BlockSpec((LANES, LANES), lambda bi: (0, 0))],
        out_specs=[spec, per_slot, per_slot, pl.BlockSpec((1, e, LANES), lambda bi: (bi, 0, 0))],
        out_shape=[jax.ShapeDtypeStruct((b, e, s), jnp.float32), jax.ShapeDtypeStruct((b, cap, LANES), jnp.int32),
                   jax.ShapeDtypeStruct((b, cap, LANES), jnp.float32), jax.ShapeDtypeStruct((b, e, LANES), jnp.int32)],
        scratch_shapes=[pltpu.VMEM((s, LANES), jnp.bfloat16)],
        compiler_params=_params("parallel"),
        name="route",
    )(lg, tmat, excl)
    return slot, idx[:, :, :e].transpose(0, 2, 1), wgt, starts[:, :, :s // tile + 1]


GATHER_GROUP = 8


def _gather_kernel(idx_ref, h_ref, o_ref, *, cap):
    def body(g, carry):
        base = pl.multiple_of(g * GATHER_GROUP, GATHER_GROUP)
        for u in range(GATHER_GROUP):
            t = idx_ref[0, 0, base + u]
            o_ref[0, 0, pl.ds(base + u, 1), :] = h_ref[0, pl.ds(t, 1), :]
        return carry

    lax.fori_loop(0, cap // GATHER_GROUP, body, 0)


def gather(idx, hp, out_shape, out_index):
    b, e, cap = idx.shape
    _, s, w = hp.shape
    return pl.pallas_call(
        functools.partial(_gather_kernel, cap=cap),
        grid=(b, e),
        in_specs=[pl.BlockSpec((1, 1, cap), lambda bi, ei: (bi * e + ei, 0, 0), memory_space=pltpu.SMEM),
                  pl.BlockSpec((1, s, w), lambda bi, ei: (bi, 0, 0))],
        out_specs=pl.BlockSpec((1, 1, cap, w), out_index),
        out_shape=jax.ShapeDtypeStruct(out_shape, jnp.uint32),
        compiler_params=_params("parallel", "arbitrary"),
        name="gather",
    )(idx.reshape(b * e, 1, cap), hp)


FFN_CHUNK = 512


def _ffn_body(xw, wgt, wg_ref, wu_ref, wd_ref):
    x = _unpack_halves(xw)
    f = wg_ref.shape[3]
    fc = min(FFN_CHUNK, f)
    acc = jnp.zeros((x.shape[0], wd_ref.shape[3]), jnp.float32)
    for c in range(f // fc):
        a = _dot(x, wg_ref[0, 0, :, c * fc:(c + 1) * fc])
        u = _dot(x, wu_ref[0, 0, :, c * fc:(c + 1) * fc])
        acc = acc + _dot(_bf16(_silu(a) * u), wd_ref[0, 0, c * fc:(c + 1) * fc, :])
    mine = lax.broadcasted_iota(jnp.int32, wgt.shape, 1) == pl.program_id(0)
    return _bf16(acc * jnp.sum(jnp.where(mine, wgt, 0.0), axis=1, keepdims=True))


def _ffn_kernel(xl_ref, al_ref, wg_ref, wu_ref, wd_ref, yl_ref):
    yl_ref[0, 0] = _ffn_body(xl_ref[0, 0], al_ref[0], wg_ref, wu_ref, wd_ref)


def _ffn_ctx_kernel(xl_ref, al_ref, xc_ref, ac_ref, wg_ref, wu_ref, wd_ref, yl_ref, yc_ref, *, nb):
    is_ctx = pl.program_id(1) == nb
    y = _ffn_body(jnp.where(is_ctx, xc_ref[0, 0], xl_ref[0, 0]), jnp.where(is_ctx, ac_ref[0], al_ref[0]),
                  wg_ref, wu_ref, wd_ref)

    @pl.when(jnp.logical_not(is_ctx))
    def _():
        yl_ref[0, 0] = y

    @pl.when(is_ctx)
    def _():
        yc_ref[0, 0] = y


def expert_ffn(xe, wgt, xe_c, wgt_c, w_g, w_u, w_d, li):
    nb, e, cap, w = xe.shape
    d, f = w_g.shape[2:]
    wspec = lambda shape: pl.BlockSpec((1, 1) + shape, lambda ei, bi: (li, ei, 0, 0))
    weights = [wspec((d, f)), wspec((d, f)), wspec((f, d))]
    lat = lambda width: pl.BlockSpec((1, 1, cap, width), lambda ei, bi: (jnp.minimum(bi, nb - 1), ei, 0, 0))
    lat_w = pl.BlockSpec((1, cap, LANES), lambda ei, bi: (jnp.minimum(bi, nb - 1), 0, 0))
    cspec = lambda width: pl.BlockSpec((1, 1, cap, width), lambda ei, bi: (0, ei, 0, 0))
    ctx_w = pl.BlockSpec((1, cap, LANES), lambda ei, bi: (0, 0, 0))
    if xe_c is None:
        return pl.pallas_call(
            _ffn_kernel, grid=(e, nb),
            in_specs=[lat(w), lat_w] + weights, out_specs=lat(d),
            out_shape=jax.ShapeDtypeStruct((nb, e, cap, d), jnp.bfloat16),
            compiler_params=_params("parallel", "arbitrary"), name="expert_ffn",
        )(xe, wgt, w_g, w_u, w_d), None
    assert xe_c.shape == (1, e, cap, w) and wgt_c.shape == (1, cap, LANES)
    return pl.pallas_call(
        functools.partial(_ffn_ctx_kernel, nb=nb), grid=(e, nb + 1),
        in_specs=[lat(w), lat_w, cspec(w), ctx_w] + weights, out_specs=[lat(d), cspec(d)],
        out_shape=[jax.ShapeDtypeStruct((nb, e, cap, d), jnp.bfloat16),
                   jax.ShapeDtypeStruct((1, e, cap, d), jnp.bfloat16)],
        compiler_params=_params("parallel", "arbitrary"), name="expert_ffn_ctx",
    )(xe, wgt, xe_c, wgt_c, w_g, w_u, w_d)


COMBINE_WIN = 128
SLOT_ALIGN = 16


def _combine_kernel(starts_ref, slot_ref, ye_ref, x_ref, gate_ref, o_ref, *, slots_per_sample):
    f32 = jnp.float32
    bi, ti = pl.program_id(0), pl.program_id(1)
    n_exp, n_slots, d = ye_ref.shape[1:]
    assert n_exp % 2 == 0
    win = min(COMBINE_WIN, n_slots)
    base = bi * slots_per_sample
    slot_t = slot_ref[0]
    tm = slot_t.shape[0]
    slot_t = jnp.where(slot_t < 0.0, -1.0, slot_t + base.astype(f32))
    expert_lane = lax.broadcasted_iota(jnp.int32, slot_t.shape, 1)
    lane = lax.broadcasted_iota(jnp.int32, (tm, win), 1).astype(f32)

    def column(e):
        col = jnp.sum(jnp.where(expert_lane == e, slot_t, 0.0), axis=1, keepdims=True)
        return jnp.broadcast_to(col, (tm, win))

    def first_window(e):
        lo = base + starts_ref[bi, e, ti]
        return jnp.minimum(lo // SLOT_ALIGN * SLOT_ALIGN, n_slots - win)

    def onehot(slot_b, nominal, start):
        hit = (slot_b - start.astype(f32) == lane) & (slot_b >= nominal.astype(f32))
        return _bf16(jnp.where(hit, 1.0, 0.0))

    def rows(e, start):
        return ye_ref[0, e, pl.ds(pl.multiple_of(start, SLOT_ALIGN), win), :]

    for e in range(0, n_exp, 2):
        ws = [first_window(e), first_window(e + 1)]
        picks = jnp.concatenate([onehot(column(e + j), ws[j], ws[j]) for j in range(2)], axis=1)
        term = _dot(picks, jnp.concatenate([rows(e + j, ws[j]) for j in range(2)], axis=0))
        if e == 0:
            o_ref[0] = term
        else:
            o_ref[0] += term

    def expert(e, carry):
        ws = first_window(e)
        hi = base + starts_ref[bi, e, ti + 1]

        def window(k, c2):
            nominal = ws + k * win
            start = jnp.minimum(nominal, n_slots - win)
            o_ref[0] += _dot(onehot(column(e), nominal, start), rows(e, start))
            return c2

        return lax.fori_loop(1, (hi - ws + win - 1) // win, window, carry)

    lax.fori_loop(0, n_exp, expert, 0)
    o_ref[0] = x_ref[0] + gate_ref[0, 0] * o_ref[0]


def combine(starts, slot_t, ye, x, mod, tm):
    b, l, d = x.shape
    e = slot_t.shape[2]
    pooled = ye.shape[0] == 1 and b > 1
    bm = (lambda bi: bi) if mod.shape[0] == b else (lambda bi: 0)
    tok = lambda w: pl.BlockSpec((1, tm, w), lambda bi, i, st: (bi, i, 0))
    ye_spec = pl.BlockSpec((1,) + ye.shape[1:], lambda bi, i, st: (0 if pooled else bi, 0, 0, 0),
                           pipeline_mode=pl.Buffered(1))
    return pl.pallas_call(
        functools.partial(_combine_kernel, slots_per_sample=ye.shape[2] // b if pooled else 0),
        grid_spec=pltpu.PrefetchScalarGridSpec(
            num_scalar_prefetch=1,
            grid=(b, l // tm),
            in_specs=[tok(e), ye_spec, tok(d),
                      pl.BlockSpec((1, 1, 1, d), lambda bi, i, st: (bm(bi), N_MOD - 1, 0, 0))],
            out_specs=tok(d)),
        out_shape=jax.ShapeDtypeStruct((b, l, d), jnp.float32),
        compiler_params=_params("parallel", "arbitrary"),
        name="combine",
    )(starts, slot_t, ye, x, mod)


TM_IN = 2048
TM_MERGE = 512
TM_COMBINE = 512
HG_ROWS = 512


def _route_and_gather(hp, lg, xe_shape, xe_index, cap, tile):
    slot, idx, wgt, starts = route(lg, cap, tile)
    xe = gather(idx, hp, xe_shape, xe_index)
    return (starts, slot.transpose(0, 2, 1)), xe, wgt


def kernel(x, c, ctx, c_ctx, w_mod, b_mod, norm1, w_in, conv_w, hg_lb_logits, hg_norm, na_q_norm, na_k_norm, na_rpb,
           w_br_a, w_br_b, w_br_c, w_out, norm2, w_router, w_e_gate, w_e_up, w_e_down):
    b, s, d = x.shape
    l = ctx.shape[1]
    depth = w_mod.shape[0]
    e = w_router.shape[-1]
    cap = CAP_FACTOR * s // e
    cap_c = CAP_FACTOR * l // e
    assert b * cap_c == cap, "context rows of all samples fill one expert tile"

    lb_sm = jax.nn.softmax(hg_lb_logits.astype(jnp.float32), axis=0)
    lb_all = jnp.cumsum(lb_sm, axis=0) - lb_sm[0]
    rows = -(-(b + 1) // 8) * 8
    cc = jnp.zeros((rows, d), jnp.float32).at[:b].set(c).at[b].set(c_ctx)
    mod_all = modulation(cc, w_mod, b_mod).reshape(depth, rows, N_MOD, 1, d)

    w_in_r = _bf16(w_in)
    w_a, w_b, w_c, w_o = _bf16(w_br_a), _bf16(w_br_b), _bf16(w_br_c), _bf16(w_out)
    w_rt = _bf16(jnp.swapaxes(w_router, 1, 2))
    w_g, w_u, w_d = _bf16(w_e_gate), _bf16(w_e_up), _bf16(w_e_down)
    bias = jax.vmap(_natten_bias)(na_rpb)
    s0 = jnp.zeros((b, 2, HG_HEADS, HG_D, HG_D), jnp.float32)
    hg_rows = min(HG_ROWS, s)
    tm_in = min(TM_IN, s)
    tm_cmb = min(TM_COMBINE, s)

    xc = ctx
    for li in range(depth):
        last = li == depth - 1
        mod = mod_all[li, :b]
        mod_c = mod_all[li, b:b + 1]
        z = input_projection(x, mod, norm1[li], w_in_r, li, na_q_norm[li], na_k_norm[li], tm_in)
        zc = input_projection(xc.reshape(1, b * l, d), mod_c, norm1[li], w_in_r, li, na_q_norm[li], na_k_norm[li],
                              min(TM_IN, b * l)).reshape(b, l, -1)
        oc_fw, oc_bw, s_ctx = hgrn_scan(zc, lb_all[li], s0, l)
        o_fw, o_bw, _ = hgrn_scan(z, lb_all[li], s_ctx, hg_rows)
        y_na, yc_na = natten(z, zc, bias, li)

        mw = (conv_w[li], hg_norm[li], w_a, w_b, w_c, w_o, norm2[li], w_rt, li)
        x, hp, lg = merge(z, o_fw, o_bw, y_na, x, mod, *mw, TM_MERGE)
        plan, xe, wgt = _route_and_gather(hp, lg, (b, e, cap, d // 2), lambda bi, ei: (bi, ei, 0, 0), cap, tm_cmb)
        if last:
            ye, _ = expert_ffn(xe, wgt, None, None, w_g, w_u, w_d, li)
        else:
            xc, hpc, lgc = merge(zc, oc_fw, oc_bw, yc_na, xc, mod_c, *mw, l)
            plan_c, xe_c, wgt_c = _route_and_gather(hpc, lgc, (1, e, cap, d // 2), lambda bi, ei: (0, ei, bi, 0),
                                                    cap_c, l)
            ye, ye_c = expert_ffn(xe, wgt, xe_c, wgt_c.reshape(1, cap, LANES), w_g, w_u, w_d, li)
            xc = combine(*plan_c, ye_c, xc, mod_c, l)
        x = combine(*plan, ye, x, mod, tm_cmb)
    return x
```

```python
import functools

import jax
import jax.numpy as jnp
from jax import lax
from jax.experimental import pallas as pl
from jax.experimental.pallas import tpu as pltpu

N_MOD = 6
EPS = 1e-6
MASK_VALUE = -1e30
GRID_W = 64
GROUP_W = 512
HG_HEADS = 4
HG_D = 128
HG_CHUNK = 64
HG_SUB = 8
LOG2_E = 1.4426950408889634
NA_HEADS = 8
NA_HD = 64
WIN_R = 8
WIN_C = 16
CAP_FACTOR = 2
T_CONV_B, T_CONV_C, T_CONV_U, T_HG_Q, T_HG_I, T_HG_FF, T_HG_FB, T_HG_G, T_NA_Q, T_NA_K, T_NA_V, T_GATES = range(12)
N_GATES = 3

VMEM_LIMIT_BYTES = 48 * 1024 * 1024


def _params(*semantics):
    return pltpu.CompilerParams(dimension_semantics=semantics, vmem_limit_bytes=VMEM_LIMIT_BYTES)


def _silu(v):
    return v * jax.nn.sigmoid(v)


def _bf16(v):
    return v.astype(jnp.bfloat16)


def _dot(a, b):
    return jnp.dot(a, b, preferred_element_type=jnp.float32)


def _dot_nt(a, b):
    return lax.dot_general(a, b, (((1,), (1,)), ((), ())), preferred_element_type=jnp.float32)


def _mod_kernel(c_ref, w_ref, b_ref, o_ref):
    cond = _bf16(_silu(c_ref[...]))
    o_ref[0] = _dot(cond, _bf16(w_ref[0])) + b_ref[0]


def modulation(cc, w_mod, b_mod):
    depth, d, nm = w_mod.shape
    r = cc.shape[0]
    tn = d
    return pl.pallas_call(
        _mod_kernel,
        grid=(depth, nm // tn),
        in_specs=[pl.BlockSpec((r, d), lambda l, j: (0, 0)),
                  pl.BlockSpec((1, d, tn), lambda l, j: (l, 0, j)),
                  pl.BlockSpec((1, 1, tn), lambda l, j: (l, 0, j))],
        out_specs=pl.BlockSpec((1, r, tn), lambda l, j: (l, 0, j)),
        out_shape=jax.ShapeDtypeStruct((depth, r, nm), jnp.float32),
        compiler_params=_params("parallel", "parallel"),
        name="modulation",
    )(cc, w_mod, b_mod.reshape(depth, 1, nm))


def _modnorm(x, w, shift, scale):
    ms = jnp.mean(x * x, axis=-1, keepdims=True)
    return (x * lax.rsqrt(ms + EPS) * w) * (1.0 + scale) + shift


def _group_rms(acc, gmat, w_tiled):
    ms = _dot(_bf16(acc * acc), gmat)
    return acc * lax.rsqrt(ms + EPS) * w_tiled


def _inproj_kernel(x_ref, shift_ref, scale_ref, nw_ref, w_ref, g64_ref, qn_ref, kn_ref, o_ref, h_ref):
    j = pl.program_id(2)

    @pl.when(j == 0)
    def _():
        h_ref[...] = _bf16(_modnorm(x_ref[0], nw_ref[...], shift_ref[0, 0], scale_ref[0, 0]))

    acc = lambda: _dot(h_ref[...], w_ref[0])
    plain = (j != T_HG_Q) & (j != T_NA_Q) & (j != T_NA_K) & (j < T_GATES)

    @pl.when(plain)
    def _():
        o_ref[0] = acc().astype(o_ref.dtype)

    @pl.when(j >= T_GATES)
    def _():
        o_ref[0] = _sigmoid(acc()).astype(o_ref.dtype)

    @pl.when(j == T_HG_Q)
    def _():
        o_ref[0] = (_silu(acc()) * (HG_D ** -0.5)).astype(o_ref.dtype)

    @pl.when(j == T_NA_Q)
    def _():
        o_ref[0] = (_group_rms(acc(), g64_ref[...], qn_ref[...]) * (NA_HD ** -0.5)).astype(o_ref.dtype)

    @pl.when(j == T_NA_K)
    def _():
        o_ref[0] = _group_rms(acc(), g64_ref[...], kn_ref[...]).astype(o_ref.dtype)


def _block_diag_mean(n, group):
    idx = jnp.arange(n) // group
    return _bf16(jnp.where(idx[:, None] == idx[None, :], 1.0 / group, 0.0))


def input_projection(x, mod, norm_w, w_in, li, qn, kn, tm):
    bx, s, d = x.shape
    n = w_in.shape[2]
    tn = GROUP_W
    g64 = _block_diag_mean(tn, NA_HD)
    tile = lambda v: jnp.tile(v.reshape(1, -1), (1, tn // v.shape[-1]))
    return pl.pallas_call(
        _inproj_kernel,
        grid=(bx, s // tm, n // tn),
        in_specs=[pl.BlockSpec((1, tm, d), lambda b, i, j: (b, i, 0)),
                  pl.BlockSpec((1, 1, 1, d), lambda b, i, j: (b, 0, 0, 0)),
                  pl.BlockSpec((1, 1, 1, d), lambda b, i, j: (b, 1, 0, 0)),
                  pl.BlockSpec((1, d), lambda b, i, j: (0, 0)),
                  pl.BlockSpec((1, d, tn), lambda b, i, j: (li, 0, j)),
                  pl.BlockSpec((tn, tn), lambda b, i, j: (0, 0)),
                  pl.BlockSpec((1, tn), lambda b, i, j: (0, 0)),
                  pl.BlockSpec((1, tn), lambda b, i, j: (0, 0))],
        out_specs=pl.BlockSpec((1, tm, tn), lambda b, i, j: (b, i, j)),
        out_shape=jax.ShapeDtypeStruct((bx, s, n), jnp.bfloat16),
        scratch_shapes=[pltpu.VMEM((tm, d), jnp.bfloat16)],
        compiler_params=_params("parallel", "parallel", "arbitrary"),
        name="input_projection",
    )(x, mod, mod, norm_w.reshape(1, d), w_in, g64, tile(qn), tile(kn))


def _split3(v):
    h1 = _bf16(v)
    r1 = v - h1.astype(jnp.float32)
    h2 = _bf16(r1)
    h3 = _bf16(r1 - h2.astype(jnp.float32))
    return h1, h2, h3


def _hgrn_gates(f, lb, tri):
    sig = jax.nn.sigmoid(f)
    lg = jnp.log(lb + (1.0 - lb) * sig) * LOG2_E
    k = (1.0 - lb) * (1.0 - sig)
    h1, h2, h3 = _split3(lg)
    return k, _dot(tri, h1) + _dot(tri, h2) + _dot(tri, h3)


HG_GRP = 8
HG_UNROLL = 2


def _hgrn_head_products(q, v, k, bc, st, ones, rev, k_row, bc_row):
    c = HG_CHUNK
    end = 0 if rev else c - 1
    kd = k * jnp.exp2(bc_row(end, c) - bc)
    v_t = _bf16(v.astype(jnp.float32).T)
    st_new = st * jnp.exp2(bc_row(end, st.shape[0])) + _dot(v_t, _bf16(kd))

    row = lax.broadcasted_iota(jnp.int32, (HG_GRP, HG_D), 0)
    a_off, ps, where = [], [], []
    for i in range(c // HG_SUB):
        lo, hi = i * HG_SUB, (i + 1) * HG_SUB
        qi, bi = q[lo:hi], bc[lo:hi]
        prev = None
        if not rev and i > 0:
            ref, prev = lo - 1, slice(0, lo)
        if rev and hi < c:
            ref, prev = hi, slice(hi, c)
        if prev is None:
            a_off.append(None)
        else:
            qt = _bf16(qi * jnp.exp2(bi - bc_row(ref, HG_SUB)))
            kt = k[prev] * jnp.exp2(bc_row(ref, prev.stop - prev.start) - bc[prev])
            pad = jnp.zeros((c - kt.shape[0], HG_D), kt.dtype)
            kt = _bf16(jnp.concatenate([pad, kt] if rev else [kt, pad], axis=0))
            a_off.append(_dot_nt(qt, kt))
        for s in range(lo, hi):
            for g in range(lo // HG_GRP, hi // HG_GRP):
                g_lo = g * HG_GRP
                sees = (g_lo <= s) if rev else (g_lo + HG_GRP > s)
                if not sees:
                    continue
                diff = bc[g_lo:g_lo + HG_GRP] - bc_row(s, HG_GRP)
                if g_lo <= s < g_lo + HG_GRP:
                    valid = (row <= s - g_lo) if rev else (row >= s - g_lo)
                    diff = jnp.where(valid, diff, MASK_VALUE)
                where.append((g, s))
                ps.append(q[g_lo:g_lo + HG_GRP] * k_row(s, HG_GRP) * jnp.exp2(diff))
    rsum = _dot(_bf16(jnp.concatenate(ps, axis=0)), ones)
    o_inter = _dot_nt(_bf16(q * jnp.exp2(bc)), _bf16(st))
    return o_inter, st_new, a_off, rsum, where


def _hgrn_head_output(v, o_inter, a_off, rsum, where):
    c = HG_CHUNK
    lane = lax.broadcasted_iota(jnp.int32, (HG_GRP, c), 1)
    groups = [None] * (c // HG_GRP)
    for i, a in enumerate(a_off):
        for g in range(i * HG_SUB // HG_GRP, (i + 1) * HG_SUB // HG_GRP):
            r = (g - i * HG_SUB // HG_GRP) * HG_GRP
            groups[g] = jnp.zeros((HG_GRP, c), jnp.float32) if a is None else a[r:r + HG_GRP]
    for n, (g, s) in enumerate(where):
        groups[g] = jnp.where(lane == s, rsum[n * HG_GRP:(n + 1) * HG_GRP, :c], groups[g])
    return o_inter + _dot(_bf16(jnp.concatenate(groups, axis=0)), v)


def _hgrn_kernel(qf_ref, vf_ref, ff_ref, qb_ref, vb_ref, fb_ref, lb_ref, s0_ref, tri_ref, ones_ref,
                 of_ref, ob_ref, sfin_ref, st_ref, kb_ref, bb_ref, *, n_chunks):
    cb = pl.program_id(1)

    @pl.when(cb == 0)
    def _():
        st_ref[...] = s0_ref[0]

    ones = ones_ref[...]
    heads = [slice(h * HG_D, (h + 1) * HG_D) for h in range(HG_HEADS)]
    streams = ((qf_ref, vf_ref, ff_ref, of_ref, False), (qb_ref, vb_ref, fb_ref, ob_ref, True))

    def repeat_row(ref, u, di, hs):
        return lambda s, n: jnp.broadcast_to(ref[u, di, s:s + 1, hs], (n, HG_D))

    def chunks(it, carry):
        state = [[st_ref[di, h] for h in range(HG_HEADS)] for di in range(2)]
        loaded = []
        for u in range(HG_UNROLL):
            ci = it * HG_UNROLL + u
            for di, (q_ref, v_ref, f_ref, o_ref, rev) in enumerate(streams):
                r0 = pl.multiple_of(((n_chunks - 1 - ci) if rev else ci) * HG_CHUNK, HG_CHUNK)
                q = q_ref[0, pl.ds(r0, HG_CHUNK), :].astype(jnp.float32)
                v = v_ref[0, pl.ds(r0, HG_CHUNK), :]
                f = f_ref[0, pl.ds(r0, HG_CHUNK), :].astype(jnp.float32)
                k, bc = _hgrn_gates(f, lb_ref[di:di + 1], tri_ref[di])
                kb_ref[u, di] = k
                bb_ref[u, di] = bc
                loaded.append((u, di, o_ref, rev, r0, q, v, k, bc))
        pending = []
        for u, di, o_ref, rev, r0, q, v, k, bc in loaded:
            stage = []
            for h, hs in enumerate(heads):
                res = _hgrn_head_products(q[:, hs], v[:, hs], k[:, hs], bc[:, hs], state[di][h], ones, rev,
                                          repeat_row(kb_ref, u, di, hs), repeat_row(bb_ref, u, di, hs))
                state[di][h] = res[1]
                stage.append(res)
            pending.append((o_ref, r0, v, stage))
        for o_ref, r0, v, stage in pending:
            outs = [_hgrn_head_output(v[:, hs], o_inter, a_off, rsum, where)
                    for hs, (o_inter, _, a_off, rsum, where) in zip(heads, stage)]
            o_ref[0, pl.ds(r0, HG_CHUNK), :] = jnp.concatenate(outs, axis=1)
        for di in range(2):
            for h in range(HG_HEADS):
                st_ref[di, h] = state[di][h]
        return carry

    lax.fori_loop(0, n_chunks // HG_UNROLL, chunks, 0)

    @pl.when(cb == pl.num_programs(1) - 1)
    def _():
        sfin_ref[0] = st_ref[...]


def hgrn_scan(z, lb, s0, tc):
    b, l, _ = z.shape
    nb = l // tc
    ci = jnp.arange(HG_CHUNK)
    tri = _bf16(jnp.stack([ci[None, :] <= ci[:, None], ci[None, :] >= ci[:, None]]))
    ones = jnp.ones((HG_D, HG_D), jnp.bfloat16)
    fwd = lambda t: pl.BlockSpec((1, tc, GROUP_W), lambda bi, c: (bi, c, t))
    bwd = lambda t: pl.BlockSpec((1, tc, GROUP_W), lambda bi, c: (bi, nb - 1 - c, t))
    st_spec = pl.BlockSpec((1, 2, HG_HEADS, HG_D, HG_D), lambda bi, c: (bi, 0, 0, 0, 0))
    return pl.pallas_call(
        functools.partial(_hgrn_kernel, n_chunks=tc // HG_CHUNK),
        grid=(b, nb),
        in_specs=[fwd(T_HG_Q), fwd(T_HG_I), fwd(T_HG_FF), bwd(T_HG_Q), bwd(T_HG_I), bwd(T_HG_FB),
                  pl.BlockSpec((2, GROUP_W), lambda bi, c: (0, 0)),
                  st_spec,
                  pl.BlockSpec((2, HG_CHUNK, HG_CHUNK), lambda bi, c: (0, 0, 0)),
                  pl.BlockSpec((HG_D, HG_D), lambda bi, c: (0, 0))],
        out_specs=[pl.BlockSpec((1, tc, GROUP_W), lambda bi, c: (bi, c, 0)),
                   pl.BlockSpec((1, tc, GROUP_W), lambda bi, c: (bi, nb - 1 - c, 0)), st_spec],
        out_shape=[jax.ShapeDtypeStruct((b, l, GROUP_W), jnp.float32),
                   jax.ShapeDtypeStruct((b, l, GROUP_W), jnp.float32),
                   jax.ShapeDtypeStruct(s0.shape, jnp.float32)],
        scratch_shapes=[pltpu.VMEM((2, HG_HEADS, HG_D, HG_D), jnp.float32),
                        pltpu.VMEM((HG_UNROLL, 2, HG_CHUNK, GROUP_W), jnp.float32),
                        pltpu.VMEM((HG_UNROLL, 2, HG_CHUNK, GROUP_W), jnp.float32)],
        compiler_params=_params("parallel", "arbitrary"),
        name="hgrn_scan",
    )(z, z, z, z, z, z, lb, s0, tri, ones)


NA_ROWS = 8


def _softmax_pv(scores, values):
    m = functools.reduce(jnp.maximum, [jnp.max(s, axis=-1, keepdims=True) for s in scores])
    ps = [jnp.exp(s - m) for s in scores]
    den = functools.reduce(jnp.add, [jnp.sum(p, axis=-1, keepdims=True) for p in ps])
    num = functools.reduce(jnp.add, [_dot(_bf16(p), v) for p, v in zip(ps, values)])
    return num / den


def _natten_kernel(q_ref, k_ref, v_ref, qc_ref, kc_ref, vc_ref, bias_ref, o_ref, oc_ref, *, rows):
    lane = lax.broadcasted_iota(jnp.int32, (1, 2 * NA_HD), 1)
    first = lane < NA_HD
    kc = kc_ref[0]
    vc = vc_ref[0]
    zero = jnp.zeros((), q_ref.dtype)

    def stack(q):
        return jnp.concatenate([jnp.where(first, q, zero), jnp.where(first, zero, q)], axis=0)

    def unstack(o):
        n = o.shape[0] // 2
        return jnp.where(first, o[:n], o[n:])

    oc_ref[0] = unstack(_softmax_pv([_dot_nt(stack(qc_ref[0]), kc)], [vc])).astype(oc_ref.dtype)

    def row_group(it, carry):
        work = []
        for j in range(NA_ROWS):
            r = it * NA_ROWS + j
            r0 = jnp.clip(r - WIN_R // 2, 0, rows - WIN_R)
            delta = r0 - r + WIN_R - 1
            qs = pl.multiple_of(r * GRID_W, GRID_W)
            ks = pl.multiple_of(r0 * GRID_W, GRID_W)
            q2 = stack(q_ref[0, pl.ds(qs, GRID_W), :])
            kl = k_ref[0, pl.ds(ks, WIN_R * GRID_W), :]
            vl = v_ref[0, pl.ds(ks, WIN_R * GRID_W), :]
            work.append((qs, vl, _dot_nt(q2, kl) + bias_ref[0, 0, delta], _dot_nt(q2, kc)))
        outs = [unstack(_softmax_pv([s_loc, s_ctx], [vl, vc])) for _, vl, s_loc, s_ctx in work]
        for (qs, _, _, _), o in zip(work, outs):
            o_ref[0, pl.ds(qs, GRID_W), :] = o.astype(o_ref.dtype)
        return carry

    lax.fori_loop(0, rows // NA_ROWS, row_group, 0)


def _natten_bias(rpb):
    qcol = jnp.arange(GRID_W)[:, None]
    kcol = jnp.arange(GRID_W)[None, :]
    wstart = jnp.clip(qcol - WIN_C // 2, 0, GRID_W - WIN_C)
    in_win = (kcol >= wstart) & (kcol < wstart + WIN_C)
    dc = jnp.clip(kcol - qcol + WIN_C - 1, 0, 2 * WIN_C - 2)
    dr = jnp.arange(WIN_R)[:, None] + jnp.arange(WIN_R)[None, :]
    t = rpb[:, :, dc][:, dr]
    t = jnp.where(in_win[None, None, None], t.astype(jnp.float32), MASK_VALUE)
    t = t.transpose(0, 1, 3, 2, 4).reshape(NA_HEADS // 2, 2, WIN_R, GRID_W, WIN_R * GRID_W)
    return t.transpose(0, 2, 1, 3, 4).reshape(NA_HEADS // 2, WIN_R, 2 * GRID_W, WIN_R * GRID_W)


def natten(z, zc, bias, li):
    b, s, _ = z.shape
    l = zc.shape[1]
    rows = s // GRID_W
    assert rows >= WIN_R
    pairs = NA_HEADS // 2
    per_tile = GROUP_W // (2 * NA_HD)
    spec = lambda n, t: pl.BlockSpec((1, n, 2 * NA_HD), lambda bi, p: (bi, 0, t * per_tile + p))
    return pl.pallas_call(
        functools.partial(_natten_kernel, rows=rows),
        grid=(b, pairs),
        in_specs=[spec(s, T_NA_Q), spec(s, T_NA_K), spec(s, T_NA_V),
                  spec(l, T_NA_Q), spec(l, T_NA_K), spec(l, T_NA_V),
                  pl.BlockSpec((1, 1, WIN_R, 2 * GRID_W, WIN_R * GRID_W), lambda bi, p: (li, p, 0, 0, 0))],
        out_specs=[pl.BlockSpec((1, s, 2 * NA_HD), lambda bi, p: (bi, 0, p)),
                   pl.BlockSpec((1, l, 2 * NA_HD), lambda bi, p: (bi, 0, p))],
        out_shape=[jax.ShapeDtypeStruct((b, s, GROUP_W), jnp.bfloat16),
                   jax.ShapeDtypeStruct((b, l, GROUP_W), jnp.bfloat16)],
        compiler_params=_params("parallel", "parallel"),
        name="natten",
    )(z, z, z, zc, zc, zc, bias)


HALO = 16
MERGE_PARTS = 2


def _pack_halves(h):
    half = h.shape[1] // 2
    lo = pltpu.bitcast(_bf16(h[:, :half]).astype(jnp.float32), jnp.uint32)
    hi = pltpu.bitcast(_bf16(h[:, half:]).astype(jnp.float32), jnp.uint32)
    return (lo >> 16) | (hi & jnp.uint32(0xFFFF0000))


def _unpack_halves(w):
    lo = pltpu.bitcast(w << 16, jnp.float32)
    hi = pltpu.bitcast(w & jnp.uint32(0xFFFF0000), jnp.float32)
    return _bf16(jnp.concatenate([lo, hi], axis=1))


def _sigmoid(v):
    return 0.5 * jnp.tanh(0.5 * v) + 0.5


def _merge_kernel(zb_ref, zc_ref, zu_ref, pc_ref, pu_ref, nc_ref, nu_ref, cw_ref,
                  ofw_ref, obw_ref, zg_ref, hgn_ref, g128_ref, yna_ref, *rest):
    n_gate = len(rest) - 13
    gate_refs, rest = rest[:n_gate], rest[n_gate:]
    (x_ref, gate_ref, wa_ref, wb_ref, wc_ref, wo_ref, n2_ref, shift_ref, scale_ref, wr_ref,
     xo_ref, hp_ref, lg_ref) = rest
    per_gate = n_gate // N_GATES
    i = pl.program_id(1)
    tm = zb_ref.shape[1]
    f32 = jnp.float32

    v = zc_ref[0].astype(f32) * zu_ref[0].astype(f32)
    vp = pc_ref[0, HALO - 1:HALO].astype(f32) * pu_ref[0, HALO - 1:HALO].astype(f32)
    vn = nc_ref[0, 0:1].astype(f32) * nu_ref[0, 0:1].astype(f32)
    vp = jnp.where(i == 0, 0.0, vp)
    vn = jnp.where(i == pl.num_programs(1) - 1, 0.0, vn)
    row = lax.broadcasted_iota(jnp.int32, v.shape, 0)
    v_prev = jnp.where(row == 0, vp, pltpu.roll(v, 1, 0))
    v_next = jnp.where(row == tm - 1, vn, pltpu.roll(v, tm - 1, 0))
    cw = cw_ref[...]
    y_cv = zb_ref[0].astype(f32) * (cw[0:1] * v_prev + cw[1:2] * v + cw[2:3] * v_next)

    parts = [slice(p * tm // MERGE_PARTS, (p + 1) * tm // MERGE_PARTS) for p in range(MERGE_PARTS)]
    o = [ofw_ref[0, rs, :] + obw_ref[0, rs, :] for rs in parts]
    ms = [_dot(_bf16(v * v), g128_ref[...]) for v in o]
    y_hg = [v * lax.rsqrt(m_ + EPS) * hgn_ref[...] * _silu(zg_ref[0, rs, :].astype(f32))
            for v, m_, rs in zip(o, ms, parts)]
    gates = [[jnp.concatenate([r[0, rs, :] for r in gate_refs[k * per_gate:(k + 1) * per_gate]], axis=1).astype(f32)
              for k in range(N_GATES)] for rs in parts]
    m = [g[0] * _dot(_bf16(y_cv[rs]), wa_ref[0]) + g[1] * _dot(_bf16(yh), wb_ref[0])
         + g[2] * _dot(yna_ref[0, rs, :], wc_ref[0]) for g, yh, rs in zip(gates, y_hg, parts)]
    x_new = [x_ref[0, rs, :] + gate_ref[0, 0] * _dot(_bf16(v), wo_ref[0]) for v, rs in zip(m, parts)]
    h2 = [_modnorm(v, n2_ref[...], shift_ref[0, 0], scale_ref[0, 0]) for v in x_new]
    logits = [_dot_nt(wr_ref[0], _bf16(v)) for v in h2]
    for rs, xn, h in zip(parts, x_new, h2):
        xo_ref[0, rs, :] = xn
        hp_ref[0, rs, :] = _pack_halves(h)
    lg_ref[0] = jnp.concatenate(logits, axis=1)


def merge(z, o_fw, o_bw, y_na, x, mod, conv_w, hg_norm, w_a, w_b, w_c, w_o, norm2, w_rt, li, tm):
    b, l, d = x.shape
    e = w_rt.shape[1]
    layer = lambda a: pl.BlockSpec((1,) + a.shape[1:], lambda bi, i: (li,) + (0,) * (a.ndim - 1))
    nt = l // tm
    per = tm // HALO
    n_halo = l // HALO
    assert d % GROUP_W == 0
    n_gate = N_GATES * d // GROUP_W
    bm = (lambda bi: bi) if mod.shape[0] == b else (lambda bi: 0)
    zt = lambda t: pl.BlockSpec((1, tm, GROUP_W), lambda bi, i: (bi, i, t))
    zprev = lambda t: pl.BlockSpec((1, HALO, GROUP_W), lambda bi, i: (bi, jnp.maximum(i * per - 1, 0), t))
    znext = lambda t: pl.BlockSpec((1, HALO, GROUP_W),
                                   lambda bi, i: (bi, jnp.minimum((i + 1) * per, n_halo - 1), t))
    act = lambda w: pl.BlockSpec((1, tm, w), lambda bi, i: (bi, i, 0))
    modrow = lambda k: pl.BlockSpec((1, 1, 1, d), lambda bi, i: (bm(bi), k, 0, 0))
    full = lambda a: pl.BlockSpec(a.shape, lambda bi, i: (0,) * a.ndim)
    g128 = _block_diag_mean(GROUP_W, HG_D)
    hgn = jnp.tile(hg_norm.reshape(1, -1), (1, GROUP_W // HG_D))
    n2 = norm2.reshape(1, d)
    return pl.pallas_call(
        _merge_kernel,
        grid=(b, nt),
        in_specs=[zt(T_CONV_B), zt(T_CONV_C), zt(T_CONV_U),
                  zprev(T_CONV_C), zprev(T_CONV_U), znext(T_CONV_C), znext(T_CONV_U), full(conv_w),
                  act(GROUP_W), act(GROUP_W), zt(T_HG_G), full(hgn), full(g128), act(GROUP_W)]
                 + [zt(T_GATES + k) for k in range(n_gate)]
                 + [act(d), modrow(2), layer(w_a), layer(w_b), layer(w_c), layer(w_o),
                  full(n2), modrow(3), modrow(4), layer(w_rt)],
        out_specs=[act(d), act(d // 2), pl.BlockSpec((1, e, tm), lambda bi, i: (bi, 0, i))],
        out_shape=[jax.ShapeDtypeStruct((b, l, d), jnp.float32),
                   jax.ShapeDtypeStruct((b, l, d // 2), jnp.uint32),
                   jax.ShapeDtypeStruct((b, e, l), jnp.float32)],
        compiler_params=_params("parallel", "arbitrary"),
        name="merge",
    )(z, z, z, z, z, z, z, conv_w, o_fw, o_bw, z, hgn, g128, y_na, *([z] * n_gate),
      x, mod, w_a, w_b, w_c, w_o, n2, mod, mod, w_rt)


LANES = 128
ROUTE_K_CHUNK = 1024


def _count(mask):
    return jnp.sum(jnp.where(mask, 1.0, 0.0), axis=1, keepdims=True)


def _route_kernel(lg_ref, tmat_ref, excl_ref, slot_ref, idx_ref, wgt_ref, starts_ref, table_ref, *, cap, tile):
    f32 = jnp.float32
    lg = lg_ref[0]
    e, s = lg.shape
    ex = jnp.exp(lg - jnp.max(lg, axis=0, keepdims=True))
    aff = ex / jnp.sum(ex, axis=0, keepdims=True)
    bits = pltpu.bitcast(aff, jnp.int32)

    def thr_bit(it, thr):
        cand = thr | (jnp.int32(1) << (30 - it))
        return jnp.where(_count(bits >= cand) >= cap, cand, thr)

    thr = lax.fori_loop(0, 31, thr_bit, jnp.zeros((e, 1), jnp.int32))
    gt = bits > thr
    eq = bits == thr
    need = cap - _count(gt)
    tok = lax.broadcasted_iota(jnp.int32, (e, s), 1)
    nbits = s.bit_length()

    def end_bit(it, end):
        cand = end + (jnp.int32(1) << (nbits - 1 - it))
        ok = (cand <= s) & (_count(eq & (tok < cand)) <= need)
        return jnp.where(ok, cand, end)

    end = lax.fori_loop(0, nbits, end_bit, jnp.zeros((e, 1), jnp.int32))
    sel = gt | (eq & (tok < end))

    self = jnp.where(sel, 1.0, 0.0)
    offs = jnp.zeros((e, 1), f32)
    pieces = []
    tile_lane = lax.broadcasted_iota(jnp.int32, (e, LANES), 1)
    starts = jnp.zeros((e, LANES), f32)
    for c in range(s // LANES):
        if (c * LANES) % tile == 0:
            starts = jnp.where(tile_lane == c * LANES // tile, offs, starts)
        blk = self[:, c * LANES:(c + 1) * LANES]
        pieces.append(_dot(_bf16(blk), excl_ref[...]) + offs)
        offs = offs + jnp.sum(blk, axis=1, keepdims=True)
    starts_ref[0] = jnp.where(tile_lane == s // tile, offs, starts).astype(jnp.int32)
    slot_ref[0] = jnp.where(sel, jnp.concatenate(pieces, axis=1), -1.0)

    a_t = aff.T
    plane = lax.broadcasted_iota(jnp.int32, (e, LANES), 1) - lax.broadcasted_iota(jnp.int32, (e, LANES), 0)
    table = tmat_ref[...].astype(f32)
    for piece, part in enumerate(_split3(a_t)):
        table = table + _dot(part, _bf16(jnp.where(plane == 2 + piece * e, 1.0, 0.0)))
    table_ref[...] = _bf16(table)

    piota = lax.broadcasted_iota(jnp.int32, (cap, 1), 0).astype(f32)
    lane = lax.broadcasted_iota(jnp.int32, (cap, LANES), 1)
    kc = min(ROUTE_K_CHUNK, s)
    idx_ref[0] = jnp.zeros((cap, LANES), jnp.int32)
    wgt_ref[0] = jnp.zeros((cap, LANES), f32)

    def expert(ee, carry):
        res = jnp.zeros((cap, LANES), f32)
        for c in range(s // kc):
            row = slot_ref[0, pl.ds(ee, 1), c * kc:(c + 1) * kc]
            onehot = _bf16(jnp.where(row == piota, 1.0, 0.0))
            res = res + _dot(onehot, table_ref[c * kc:(c + 1) * kc, :])
        tok_idx = (res[:, 0:1] * 64.0 + res[:, 1:2]).astype(jnp.int32)
        idx_ref[0] = jnp.where(lane == ee, tok_idx, idx_ref[0])
        wgt = sum(pltpu.roll(res, LANES - 2 - piece * e, 1) for piece in range(3))
        wgt_ref[0] = jnp.where(lane == ee, wgt, wgt_ref[0])
        return carry

    lax.fori_loop(0, e, expert, 0)


def route(lg, cap, tile):
    b, e, s = lg.shape
    assert tile % LANES == 0 and s % tile == 0 and s // tile < LANES
    t = jnp.arange(s)
    tmat = jnp.zeros((s, LANES), jnp.bfloat16).at[:, 0].set(_bf16(t // 64)).at[:, 1].set(_bf16(t % 64))
    a = jnp.arange(LANES)
    excl = _bf16(a[:, None] < a[None, :])
    spec = pl.BlockSpec((1, e, s), lambda bi: (bi, 0, 0))
    per_slot = pl.BlockSpec((1, cap, LANES), lambda bi: (bi, 0, 0))
    slot, idx, wgt, starts = pl.pallas_call(
        functools.partial(_route_kernel, cap=cap, tile=tile),
        grid=(b,),
        in_specs=[spec, pl.BlockSpec((s, LANES), lambda bi: (0, 0)), pl.BlockSpec((LANES, LANES), lambda bi: (0, 0))],
        out_specs=[spec, per_slot, per_slot, pl.BlockSpec((1, e, LANES), lambda bi: (bi, 0, 0))],
        out_shape=[jax.ShapeDtypeStruct((b, e, s), jnp.float32), jax.ShapeDtypeStruct((b, cap, LANES), jnp.int32),
                   jax.ShapeDtypeStruct((b, cap, LANES), jnp.float32), jax.ShapeDtypeStruct((b, e, LANES), jnp.int32)],
        scratch_shapes=[pltpu.VMEM((s, LANES), jnp.bfloat16)],
        compiler_params=_params("parallel"),
        name="route",
    )(lg, tmat, excl)
    return slot, idx[:, :, :e].transpose(0, 2, 1), wgt, starts[:, :, :s // tile + 1]


GATHER_GROUP = 8


def _gather_kernel(idx_ref, h_ref, o_ref, *, cap):
    def body(g, carry):
        base = pl.multiple_of(g * GATHER_GROUP, GATHER_GROUP)
        for u in range(GATHER_GROUP):
            t = idx_ref[0, 0, base + u]
            o_ref[0, 0, pl.ds(base + u, 1), :] = h_ref[0, pl.ds(t, 1), :]
        return carry

    lax.fori_loop(0, cap // GATHER_GROUP, body, 0)


def gather(idx, hp, out_shape, out_index):
    b, e, cap = idx.shape
    _, s, w = hp.shape
    return pl.pallas_call(
        functools.partial(_gather_kernel, cap=cap),
        grid=(b, e),
        in_specs=[pl.BlockSpec((1, 1, cap), lambda bi, ei: (bi * e + ei, 0, 0), memory_space=pltpu.SMEM),
                  pl.BlockSpec((1, s, w), lambda bi, ei: (bi, 0, 0))],
        out_specs=pl.BlockSpec((1, 1, cap, w), out_index),
        out_shape=jax.ShapeDtypeStruct(out_shape, jnp.uint32),
        compiler_params=_params("parallel", "arbitrary"),
        name="gather",
    )(idx.reshape(b * e, 1, cap), hp)


FFN_CHUNK = 512


def _ffn_body(xw, wgt, wg_ref, wu_ref, wd_ref):
    x = _unpack_halves(xw)
    f = wg_ref.shape[3]
    fc = min(FFN_CHUNK, f)
    acc = jnp.zeros((x.shape[0], wd_ref.shape[3]), jnp.float32)
    for c in range(f // fc):
        a = _dot(x, wg_ref[0, 0, :, c * fc:(c + 1) * fc])
        u = _dot(x, wu_ref[0, 0, :, c * fc:(c + 1) * fc])
        acc = acc + _dot(_bf16(_silu(a) * u), wd_ref[0, 0, c * fc:(c + 1) * fc, :])
    mine = lax.broadcasted_iota(jnp.int32, wgt.shape, 1) == pl.program_id(0)
    return _bf16(acc * jnp.sum(jnp.where(mine, wgt, 0.0), axis=1, keepdims=True))


def _ffn_kernel(xl_ref, al_ref, wg_ref, wu_ref, wd_ref, yl_ref):
    yl_ref[0, 0] = _ffn_body(xl_ref[0, 0], al_ref[0], wg_ref, wu_ref, wd_ref)


def _ffn_ctx_kernel(xl_ref, al_ref, xc_ref, ac_ref, wg_ref, wu_ref, wd_ref, yl_ref, yc_ref, *, nb):
    is_ctx = pl.program_id(1) == nb
    y = _ffn_body(jnp.where(is_ctx, xc_ref[0, 0], xl_ref[0, 0]), jnp.where(is_ctx, ac_ref[0], al_ref[0]),
                  wg_ref, wu_ref, wd_ref)

    @pl.when(jnp.logical_not(is_ctx))
    def _():
        yl_ref[0, 0] = y

    @pl.when(is_ctx)
    def _():
        yc_ref[0, 0] = y


def expert_ffn(xe, wgt, xe_c, wgt_c, w_g, w_u, w_d, li):
    nb, e, cap, w = xe.shape
    d, f = w_g.shape[2:]
    wspec = lambda shape: pl.BlockSpec((1, 1) + shape, lambda ei, bi: (li, ei, 0, 0))
    weights = [wspec((d, f)), wspec((d, f)), wspec((f, d))]
    lat = lambda width: pl.BlockSpec((1, 1, cap, width), lambda ei, bi: (jnp.minimum(bi, nb - 1), ei, 0, 0))
    lat_w = pl.BlockSpec((1, cap, LANES), lambda ei, bi: (jnp.minimum(bi, nb - 1), 0, 0))
    cspec = lambda width: pl.BlockSpec((1, 1, cap, width), lambda ei, bi: (0, ei, 0, 0))
    ctx_w = pl.BlockSpec((1, cap, LANES), lambda ei, bi: (0, 0, 0))
    if xe_c is None:
        return pl.pallas_call(
            _ffn_kernel, grid=(e, nb),
            in_specs=[lat(w), lat_w] + weights, out_specs=lat(d),
            out_shape=jax.ShapeDtypeStruct((nb, e, cap, d), jnp.bfloat16),
            compiler_params=_params("parallel", "arbitrary"), name="expert_ffn",
        )(xe, wgt, w_g, w_u, w_d), None
    assert xe_c.shape == (1, e, cap, w) and wgt_c.shape == (1, cap, LANES)
    return pl.pallas_call(
        functools.partial(_ffn_ctx_kernel, nb=nb), grid=(e, nb + 1),
        in_specs=[lat(w), lat_w, cspec(w), ctx_w] + weights, out_specs=[lat(d), cspec(d)],
        out_shape=[jax.ShapeDtypeStruct((nb, e, cap, d), jnp.bfloat16),
                   jax.ShapeDtypeStruct((1, e, cap, d), jnp.bfloat16)],
        compiler_params=_params("parallel", "arbitrary"), name="expert_ffn_ctx",
    )(xe, wgt, xe_c, wgt_c, w_g, w_u, w_d)


COMBINE_WIN = 128
SLOT_ALIGN = 16


def _combine_kernel(starts_ref, slot_ref, ye_ref, x_ref, gate_ref, o_ref, *, slots_per_sample):
    f32 = jnp.float32
    bi, ti = pl.program_id(0), pl.program_id(1)
    n_exp, n_slots, d = ye_ref.shape[1:]
    assert n_exp % 2 == 0
    win = min(COMBINE_WIN, n_slots)
    base = bi * slots_per_sample
    slot_t = slot_ref[0]
    tm = slot_t.shape[0]
    slot_t = jnp.where(slot_t < 0.0, -1.0, slot_t + base.astype(f32))
    expert_lane = lax.broadcasted_iota(jnp.int32, slot_t.shape, 1)
    lane = lax.broadcasted_iota(jnp.int32, (tm, win), 1).astype(f32)

    def column(e):
        col = jnp.sum(jnp.where(expert_lane == e, slot_t, 0.0), axis=1, keepdims=True)
        return jnp.broadcast_to(col, (tm, win))

    def first_window(e):
        lo = base + starts_ref[bi, e, ti]
        return jnp.minimum(lo // SLOT_ALIGN * SLOT_ALIGN, n_slots - win)

    def onehot(slot_b, nominal, start):
        hit = (slot_b - start.astype(f32) == lane) & (slot_b >= nominal.astype(f32))
        return _bf16(jnp.where(hit, 1.0, 0.0))

    def rows(e, start):
        return ye_ref[0, e, pl.ds(pl.multiple_of(start, SLOT_ALIGN), win), :]

    for e in range(0, n_exp, 2):
        ws = [first_window(e), first_window(e + 1)]
        picks = jnp.concatenate([onehot(column(e + j), ws[j], ws[j]) for j in range(2)], axis=1)
        term = _dot(picks, jnp.concatenate([rows(e + j, ws[j]) for j in range(2)], axis=0))
        if e == 0:
            o_ref[0] = term
        else:
            o_ref[0] += term

    def expert(e, carry):
        ws = first_window(e)
        hi = base + starts_ref[bi, e, ti + 1]

        def window(k, c2):
            nominal = ws + k * win
            start = jnp.minimum(nominal, n_slots - win)
            o_ref[0] += _dot(onehot(column(e), nominal, start), rows(e, start))
            return c2

        return lax.fori_loop(1, (hi - ws + win - 1) // win, window, carry)

    lax.fori_loop(0, n_exp, expert, 0)
    o_ref[0] = x_ref[0] + gate_ref[0, 0] * o_ref[0]


def combine(starts, slot_t, ye, x, mod, tm):
    b, l, d = x.shape
    e = slot_t.shape[2]
    pooled = ye.shape[0] == 1 and b > 1
    bm = (lambda bi: bi) if mod.shape[0] == b else (lambda bi: 0)
    tok = lambda w: pl.BlockSpec((1, tm, w), lambda bi, i, st: (bi, i, 0))
    ye_spec = pl.BlockSpec((1,) + ye.shape[1:], lambda bi, i, st: (0 if pooled else bi, 0, 0, 0),
                           pipeline_mode=pl.Buffered(1))
    return pl.pallas_call(
        functools.partial(_combine_kernel, slots_per_sample=ye.shape[2] // b if pooled else 0),
        grid_spec=pltpu.PrefetchScalarGridSpec(
            num_scalar_prefetch=1,
            grid=(b, l // tm),
            in_specs=[tok(e), ye_spec, tok(d),
                      pl.BlockSpec((1, 1, 1, d), lambda bi, i, st: (bm(bi), N_MOD - 1, 0, 0))],
            out_specs=tok(d)),
        out_shape=jax.ShapeDtypeStruct((b, l, d), jnp.float32),
        compiler_params=_params("parallel", "arbitrary"),
        name="combine",
    )(starts, slot_t, ye, x, mod)


TM_IN = 2048
TM_MERGE = 512
TM_COMBINE = 512
HG_ROWS = 512


def _route_and_gather(hp, lg, xe_shape, xe_index, cap, tile):
    slot, idx, wgt, starts = route(lg, cap, tile)
    xe = gather(idx, hp, xe_shape, xe_index)
    return (starts, slot.transpose(0, 2, 1)), xe, wgt


def kernel(x, c, ctx, c_ctx, w_mod, b_mod, norm1, w_in, conv_w, hg_lb_logits, hg_norm, na_q_norm, na_k_norm, na_rpb,
           w_br_a, w_br_b, w_br_c, w_out, norm2, w_router, w_e_gate, w_e_up, w_e_down):
    b, s, d = x.shape
    l = ctx.shape[1]
    depth = w_mod.shape[0]
    e = w_router.shape[-1]
    cap = CAP_FACTOR * s // e
    cap_c = CAP_FACTOR * l // e
    assert b * cap_c == cap, "context rows of all samples fill one expert tile"

    lb_sm = jax.nn.softmax(hg_lb_logits.astype(jnp.float32), axis=0)
    lb_all = jnp.cumsum(lb_sm, axis=0) - lb_sm[0]
    rows = -(-(b + 1) // 8) * 8
    cc = jnp.zeros((rows, d), jnp.float32).at[:b].set(c).at[b].set(c_ctx)
    mod_all = modulation(cc, w_mod, b_mod).reshape(depth, rows, N_MOD, 1, d)

    w_in_r = _bf16(w_in)
    w_a, w_b, w_c, w_o = _bf16(w_br_a), _bf16(w_br_b), _bf16(w_br_c), _bf16(w_out)
    w_rt = _bf16(jnp.swapaxes(w_router, 1, 2))
    w_g, w_u, w_d = _bf16(w_e_gate), _bf16(w_e_up), _bf16(w_e_down)
    bias = jax.vmap(_natten_bias)(na_rpb)
    s0 = jnp.zeros((b, 2, HG_HEADS, HG_D, HG_D), jnp.float32)
    hg_rows = min(HG_ROWS, s)
    tm_in = min(TM_IN, s)
    tm_cmb = min(TM_COMBINE, s)

    xc = ctx
    for li in range(depth):
        last = li == depth - 1
        mod = mod_all[li, :b]
        mod_c = mod_all[li, b:b + 1]
        z = input_projection(x, mod, norm1[li], w_in_r, li, na_q_norm[li], na_k_norm[li], tm_in)
        zc = input_projection(xc.reshape(1, b * l, d), mod_c, norm1[li], w_in_r, li, na_q_norm[li], na_k_norm[li],
                              min(TM_IN, b * l)).reshape(b, l, -1)
        oc_fw, oc_bw, s_ctx = hgrn_scan(zc, lb_all[li], s0, l)
        o_fw, o_bw, _ = hgrn_scan(z, lb_all[li], s_ctx, hg_rows)
        y_na, yc_na = natten(z, zc, bias, li)

        mw = (conv_w[li], hg_norm[li], w_a, w_b, w_c, w_o, norm2[li], w_rt, li)
        x, hp, lg = merge(z, o_fw, o_bw, y_na, x, mod, *mw, TM_MERGE)
        plan, xe, wgt = _route_and_gather(hp, lg, (b, e, cap, d // 2), lambda bi, ei: (bi, ei, 0, 0), cap, tm_cmb)
        if last:
            ye, _ = expert_ffn(xe, wgt, None, None, w_g, w_u, w_d, li)
        else:
            xc, hpc, lgc = merge(zc, oc_fw, oc_bw, yc_na, xc, mod_c, *mw, l)
            plan_c, xe_c, wgt_c = _route_and_gather(hpc, lgc, (1, e, cap, d // 2), lambda bi, ei: (0, ei, bi, 0),
                                                    cap_c, l)
            ye, ye_c = expert_ffn(xe, wgt, xe_c, wgt_c.reshape(1, cap, LANES), w_g, w_u, w_d, li)
            xc = combine(*plan_c, ye_c, xc, mod_c, l)
        x = combine(*plan, ye, x, mod, tm_cmb)
    return x
```

```python
import functools

import jax
import jax.numpy as jnp
from jax import lax
from jax.experimental import pallas as pl
from jax.experimental.pallas import tpu as pltpu

N_MOD = 6
EPS = 1e-6
MASK_VALUE = -1e30
GRID_W = 64
GROUP_W = 512
HG_HEADS = 4
HG_D = 128
HG_CHUNK = 64
HG_SUB = 8
LOG2_E = 1.4426950408889634
NA_HEADS = 8
NA_HD = 64
WIN_R = 8
WIN_C = 16
CAP_FACTOR = 2
T_CONV_B, T_CONV_C, T_CONV_U, T_HG_Q, T_HG_I, T_HG_FF, T_HG_FB, T_HG_G, T_NA_Q, T_NA_K, T_NA_V, T_GATES = range(12)
N_GATES = 3

VMEM_LIMIT_BYTES = 48 * 1024 * 1024


def _params(*semantics):
    return pltpu.CompilerParams(dimension_semantics=semantics, vmem_limit_bytes=VMEM_LIMIT_BYTES)


def _silu(v):
    return v * jax.nn.sigmoid(v)


def _bf16(v):
    return v.astype(jnp.bfloat16)


def _dot(a, b):
    return jnp.dot(a, b, preferred_element_type=jnp.float32)


def _dot_nt(a, b):
    return lax.dot_general(a, b, (((1,), (1,)), ((), ())), preferred_element_type=jnp.float32)


def _mod_kernel(c_ref, w_ref, b_ref, o_ref):
    cond = _bf16(_silu(c_ref[...]))
    o_ref[0] = _dot(cond, _bf16(w_ref[0])) + b_ref[0]


def modulation(cc, w_mod, b_mod):
    depth, d, nm = w_mod.shape
    r = cc.shape[0]
    tn = d
    return pl.pallas_call(
        _mod_kernel,
        grid=(depth, nm // tn),
        in_specs=[pl.BlockSpec((r, d), lambda l, j: (0, 0)),
                  pl.BlockSpec((1, d, tn), lambda l, j: (l, 0, j)),
                  pl.BlockSpec((1, 1, tn), lambda l, j: (l, 0, j))],
        out_specs=pl.BlockSpec((1, r, tn), lambda l, j: (l, 0, j)),
        out_shape=jax.ShapeDtypeStruct((depth, r, nm), jnp.float32),
        compiler_params=_params("parallel", "parallel"),
        name="modulation",
    )(cc, w_mod, b_mod.reshape(depth, 1, nm))


def _modnorm(x, w, shift, scale):
    ms = jnp.mean(x * x, axis=-1, keepdims=True)
    return (x * lax.rsqrt(ms + EPS) * w) * (1.0 + scale) + shift


def _group_rms(acc, gmat, w_tiled):
    ms = _dot(_bf16(acc * acc), gmat)
    return acc * lax.rsqrt(ms + EPS) * w_tiled


def _inproj_kernel(x_ref, shift_ref, scale_ref, nw_ref, w_ref, g64_ref, qn_ref, kn_ref, o_ref, h_ref):
    j = pl.program_id(2)

    @pl.when(j == 0)
    def _():
        h_ref[...] = _bf16(_modnorm(x_ref[0], nw_ref[...], shift_ref[0, 0], scale_ref[0, 0]))

    acc = lambda: _dot(h_ref[...], w_ref[0])
    plain = (j != T_HG_Q) & (j != T_NA_Q) & (j != T_NA_K) & (j < T_GATES)

    @pl.when(plain)
    def _():
        o_ref[0] = acc().astype(o_ref.dtype)

    @pl.when(j >= T_GATES)
    def _():
        o_ref[0] = _sigmoid(acc()).astype(o_ref.dtype)

    @pl.when(j == T_HG_Q)
    def _():
        o_ref[0] = (_silu(acc()) * (HG_D ** -0.5)).astype(o_ref.dtype)

    @pl.when(j == T_NA_Q)
    def _():
        o_ref[0] = (_group_rms(acc(), g64_ref[...], qn_ref[...]) * (NA_HD ** -0.5)).astype(o_ref.dtype)

    @pl.when(j == T_NA_K)
    def _():
        o_ref[0] = _group_rms(acc(), g64_ref[...], kn_ref[...]).astype(o_ref.dtype)


def _block_diag_mean(n, group):
    idx = jnp.arange(n) // group
    return _bf16(jnp.where(idx[:, None] == idx[None, :], 1.0 / group, 0.0))


def input_projection(x, mod, norm_w, w_in, li, qn, kn, tm):
    bx, s, d = x.shape
    n = w_in.shape[2]
    tn = GROUP_W
    g64 = _block_diag_mean(tn, NA_HD)
    tile = lambda v: jnp.tile(v.reshape(1, -1), (1, tn // v.shape[-1]))
    return pl.pallas_call(
        _inproj_kernel,
        grid=(bx, s // tm, n // tn),
        in_specs=[pl.BlockSpec((1, tm, d), lambda b, i, j: (b, i, 0)),
                  pl.BlockSpec((1, 1, 1, d), lambda b, i, j: (b, 0, 0, 0)),
                  pl.BlockSpec((1, 1, 1, d), lambda b, i, j: (b, 1, 0, 0)),
                  pl.BlockSpec((1, d), lambda b, i, j: (0, 0)),
                  pl.BlockSpec((1, d, tn), lambda b, i, j: (li, 0, j)),
                  pl.BlockSpec((tn, tn), lambda b, i, j: (0, 0)),
                  pl.BlockSpec((1, tn), lambda b, i, j: (0, 0)),
                  pl.BlockSpec((1, tn), lambda b, i, j: (0, 0))],
        out_specs=pl.BlockSpec((1, tm, tn), lambda b, i, j: (b, i, j)),
        out_shape=jax.ShapeDtypeStruct((bx, s, n), jnp.bfloat16),
        scratch_shapes=[pltpu.VMEM((tm, d), jnp.bfloat16)],
        compiler_params=_params("parallel", "parallel", "arbitrary"),
        name="input_projection",
    )(x, mod, mod, norm_w.reshape(1, d), w_in, g64, tile(qn), tile(kn))


def _split3(v):
    h1 = _bf16(v)
    r1 = v - h1.astype(jnp.float32)
    h2 = _bf16(r1)
    h3 = _bf16(r1 - h2.astype(jnp.float32))
    return h1, h2, h3


def _hgrn_gates(f, lb, tri):
    sig = jax.nn.sigmoid(f)
    lg = jnp.log(lb + (1.0 - lb) * sig) * LOG2_E
    k = (1.0 - lb) * (1.0 - sig)
    h1, h2, h3 = _split3(lg)
    return k, _dot(tri, h1) + _dot(tri, h2) + _dot(tri, h3)


HG_GRP = 8
HG_UNROLL = 2


def _hgrn_head_products(q, v, k, bc, st, ones, rev, k_row, bc_row):
    c = HG_CHUNK
    end = 0 if rev else c - 1
    kd = k * jnp.exp2(bc_row(end, c) - bc)
    v_t = _bf16(v.astype(jnp.float32).T)
    st_new = st * jnp.exp2(bc_row(end, st.shape[0])) + _dot(v_t, _bf16(kd))

    row = lax.broadcasted_iota(jnp.int32, (HG_GRP, HG_D), 0)
    a_off, ps, where = [], [], []
    for i in range(c // HG_SUB):
        lo, hi = i * HG_SUB, (i + 1) * HG_SUB
        qi, bi = q[lo:hi], bc[lo:hi]
        prev = None
        if not rev and i > 0:
            ref, prev = lo - 1, slice(0, lo)
        if rev and hi < c:
            ref, prev = hi, slice(hi, c)
        if prev is None:
            a_off.append(None)
        else:
            qt = _bf16(qi * jnp.exp2(bi - bc_row(ref, HG_SUB)))
            kt = k[prev] * jnp.exp2(bc_row(ref, prev.stop - prev.start) - bc[prev])
            pad = jnp.zeros((c - kt.shape[0], HG_D), kt.dtype)
            kt = _bf16(jnp.concatenate([pad, kt] if rev else [kt, pad], axis=0))
            a_off.append(_dot_nt(qt, kt))
        for s in range(lo, hi):
            for g in range(lo // HG_GRP, hi // HG_GRP):
                g_lo = g * HG_GRP
                sees = (g_lo <= s) if rev else (g_lo + HG_GRP > s)
                if not sees:
                    continue
                diff = bc[g_lo:g_lo + HG_GRP] - bc_row(s, HG_GRP)
                if g_lo <= s < g_lo + HG_GRP:
                    valid = (row <= s - g_lo) if rev else (row >= s - g_lo)
                    diff = jnp.where(valid, diff, MASK_VALUE)
                where.append((g, s))
                ps.append(q[g_lo:g_lo + HG_GRP] * k_row(s, HG_GRP) * jnp.exp2(diff))
    rsum = _dot(_bf16(jnp.concatenate(ps, axis=0)), ones)
    o_inter = _dot_nt(_bf16(q * jnp.exp2(bc)), _bf16(st))
    return o_inter, st_new, a_off, rsum, where


def _hgrn_head_output(v, o_inter, a_off, rsum, where):
    c = HG_CHUNK
    lane = lax.broadcasted_iota(jnp.int32, (HG_GRP, c), 1)
    groups = [None] * (c // HG_GRP)
    for i, a in enumerate(a_off):
        for g in range(i * HG_SUB // HG_GRP, (i + 1) * HG_SUB // HG_GRP):
            r = (g - i * HG_SUB // HG_GRP) * HG_GRP
            groups[g] = jnp.zeros((HG_GRP, c), jnp.float32) if a is None else a[r:r + HG_GRP]
    for n, (g, s) in enumerate(where):
        groups[g] = jnp.where(lane == s, rsum[n * HG_GRP:(n + 1) * HG_GRP, :c], groups[g])
    return o_inter + _dot(_bf16(jnp.concatenate(groups, axis=0)), v)


def _hgrn_kernel(qf_ref, vf_ref, ff_ref, qb_ref, vb_ref, fb_ref, lb_ref, s0_ref, tri_ref, ones_ref,
                 of_ref, ob_ref, sfin_ref, st_ref, kb_ref, bb_ref, *, n_chunks):
    cb = pl.program_id(1)

    @pl.when(cb == 0)
    def _():
        st_ref[...] = s0_ref[0]

    ones = ones_ref[...]
    heads = [slice(h * HG_D, (h + 1) * HG_D) for h in range(HG_HEADS)]
    streams = ((qf_ref, vf_ref, ff_ref, of_ref, False), (qb_ref, vb_ref, fb_ref, ob_ref, True))

    def repeat_row(ref, u, di, hs):
        return lambda s, n: jnp.broadcast_to(ref[u, di, s:s + 1, hs], (n, HG_D))

    def chunks(it, carry):
        state = [[st_ref[di, h] for h in range(HG_HEADS)] for di in range(2)]
        loaded = []
        for u in range(HG_UNROLL):
            ci = it * HG_UNROLL + u
            for di, (q_ref, v_ref, f_ref, o_ref, rev) in enumerate(streams):
                r0 = pl.multiple_of(((n_chunks - 1 - ci) if rev else ci) * HG_CHUNK, HG_CHUNK)
                q = q_ref[0, pl.ds(r0, HG_CHUNK), :].astype(jnp.float32)
                v = v_ref[0, pl.ds(r0, HG_CHUNK), :]
                f = f_ref[0, pl.ds(r0, HG_CHUNK), :].astype(jnp.float32)
                k, bc = _hgrn_gates(f, lb_ref[di:di + 1], tri_ref[di])
                kb_ref[u, di] = k
                bb_ref[u, di] = bc
                loaded.append((u, di, o_ref, rev, r0, q, v, k, bc))
        pending = []
        for u, di, o_ref, rev, r0, q, v, k, bc in loaded:
            stage = []
            for h, hs in enumerate(heads):
                res = _hgrn_head_products(q[:, hs], v[:, hs], k[:, hs], bc[:, hs], state[di][h], ones, rev,
                                          repeat_row(kb_ref, u, di, hs), repeat_row(bb_ref, u, di, hs))
                state[di][h] = res[1]
                stage.append(res)
            pending.append((o_ref, r0, v, stage))
        for o_ref, r0, v, stage in pending:
            outs = [_hgrn_head_output(v[:, hs], o_inter, a_off, rsum, where)
                    for hs, (o_inter, _, a_off, rsum, where) in zip(heads, stage)]
            o_ref[0, pl.ds(r0, HG_CHUNK), :] = jnp.concatenate(outs, axis=1)
        for di in range(2):
            for h in range(HG_HEADS):
                st_ref[di, h] = state[di][h]
        return carry

    lax.fori_loop(0, n_chunks // HG_UNROLL, chunks, 0)

    @pl.when(cb == pl.num_programs(1) - 1)
    def _():
        sfin_ref[0] = st_ref[...]


def hgrn_scan(z, lb, s0, tc):
    b, l, _ = z.shape
    nb = l // tc
    ci = jnp.arange(HG_CHUNK)
    tri = _bf16(jnp.stack([ci[None, :] <= ci[:, None], ci[None, :] >= ci[:, None]]))
    ones = jnp.ones((HG_D, HG_D), jnp.bfloat16)
    fwd = lambda t: pl.BlockSpec((1, tc, GROUP_W), lambda bi, c: (bi, c, t))
    bwd = lambda t: pl.BlockSpec((1, tc, GROUP_W), lambda bi, c: (bi, nb - 1 - c, t))
    st_spec = pl.BlockSpec((1, 2, HG_HEADS, HG_D, HG_D), lambda bi, c: (bi, 0, 0, 0, 0))
    return pl.pallas_call(
        functools.partial(_hgrn_kernel, n_chunks=tc // HG_CHUNK),
        grid=(b, nb),
        in_specs=[fwd(T_HG_Q), fwd(T_HG_I), fwd(T_HG_FF), bwd(T_HG_Q), bwd(T_HG_I), bwd(T_HG_FB),
                  pl.BlockSpec((2, GROUP_W), lambda bi, c: (0, 0)),
                  st_spec,
                  pl.BlockSpec((2, HG_CHUNK, HG_CHUNK), lambda bi, c: (0, 0, 0)),
                  pl.BlockSpec((HG_D, HG_D), lambda bi, c: (0, 0))],
        out_specs=[pl.BlockSpec((1, tc, GROUP_W), lambda bi, c: (bi, c, 0)),
                   pl.BlockSpec((1, tc, GROUP_W), lambda bi, c: (bi, nb - 1 - c, 0)), st_spec],
        out_shape=[jax.ShapeDtypeStruct((b, l, GROUP_W), jnp.float32),
                   jax.ShapeDtypeStruct((b, l, GROUP_W), jnp.float32),
                   jax.ShapeDtypeStruct(s0.shape, jnp.float32)],
        scratch_shapes=[pltpu.VMEM((2, HG_HEADS, HG_D, HG_D), jnp.float32),
                        pltpu.VMEM((HG_UNROLL, 2, HG_CHUNK, GROUP_W), jnp.float32),
                        pltpu.VMEM((HG_UNROLL, 2, HG_CHUNK, GROUP_W), jnp.float32)],
        compiler_params=_params("parallel", "arbitrary"),
        name="hgrn_scan",
    )(z, z, z, z, z, z, lb, s0, tri, ones)


NA_ROWS = 16


def _softmax_pv(scores, values):
    m = functools.reduce(jnp.maximum, [jnp.max(s, axis=-1, keepdims=True) for s in scores])
    ps = [jnp.exp(s - m) for s in scores]
    den = functools.reduce(jnp.add, [jnp.sum(p, axis=-1, keepdims=True) for p in ps])
    num = functools.reduce(jnp.add, [_dot(_bf16(p), v) for p, v in zip(ps, values)])
    return num / den


def _natten_kernel(q_ref, k_ref, v_ref, qc_ref, kc_ref, vc_ref, bias_ref, o_ref, oc_ref, *, rows):
    lane = lax.broadcasted_iota(jnp.int32, (1, 2 * NA_HD), 1)
    first = lane < NA_HD
    kc = kc_ref[0]
    vc = vc_ref[0]
    zero = jnp.zeros((), q_ref.dtype)

    def stack(q):
        return jnp.concatenate([jnp.where(first, q, zero), jnp.where(first, zero, q)], axis=0)

    def unstack(o):
        n = o.shape[0] // 2
        return jnp.where(first, o[:n], o[n:])

    oc_ref[0] = unstack(_softmax_pv([_dot_nt(stack(qc_ref[0]), kc)], [vc])).astype(oc_ref.dtype)

    def row_group(it, carry):
        work = []
        for j in range(NA_ROWS):
            r = it * NA_ROWS + j
            r0 = jnp.clip(r - WIN_R // 2, 0, rows - WIN_R)
            delta = r0 - r + WIN_R - 1
            qs = pl.multiple_of(r * GRID_W, GRID_W)
            ks = pl.multiple_of(r0 * GRID_W, GRID_W)
            q2 = stack(q_ref[0, pl.ds(qs, GRID_W), :])
            kl = k_ref[0, pl.ds(ks, WIN_R * GRID_W), :]
            vl = v_ref[0, pl.ds(ks, WIN_R * GRID_W), :]
            work.append((qs, vl, _dot_nt(q2, kl) + bias_ref[0, 0, delta], _dot_nt(q2, kc)))
        outs = [unstack(_softmax_pv([s_loc, s_ctx], [vl, vc])) for _, vl, s_loc, s_ctx in work]
        for (qs, _, _, _), o in zip(work, outs):
            o_ref[0, pl.ds(qs, GRID_W), :] = o.astype(o_ref.dtype)
        return carry

    lax.fori_loop(0, rows // NA_ROWS, row_group, 0)


def _natten_bias(rpb):
    qcol = jnp.arange(GRID_W)[:, None]
    kcol = jnp.arange(GRID_W)[None, :]
    wstart = jnp.clip(qcol - WIN_C // 2, 0, GRID_W - WIN_C)
    in_win = (kcol >= wstart) & (kcol < wstart + WIN_C)
    dc = jnp.clip(kcol - qcol + WIN_C - 1, 0, 2 * WIN_C - 2)
    dr = jnp.arange(WIN_R)[:, None] + jnp.arange(WIN_R)[None, :]
    t = rpb[:, :, dc][:, dr]
    t = jnp.where(in_win[None, None, None], t.astype(jnp.float32), MASK_VALUE)
    t = t.transpose(0, 1, 3, 2, 4).reshape(NA_HEADS // 2, 2, WIN_R, GRID_W, WIN_R * GRID_W)
    return t.transpose(0, 2, 1, 3, 4).reshape(NA_HEADS // 2, WIN_R, 2 * GRID_W, WIN_R * GRID_W)


def natten(z, zc, bias, li):
    b, s, _ = z.shape
    l = zc.shape[1]
    rows = s // GRID_W
    assert rows >= WIN_R
    pairs = NA_HEADS // 2
    per_tile = GROUP_W // (2 * NA_HD)
    spec = lambda n, t: pl.BlockSpec((1, n, 2 * NA_HD), lambda bi, p: (bi, 0, t * per_tile + p))
    return pl.pallas_call(
        functools.partial(_natten_kernel, rows=rows),
        grid=(b, pairs),
        in_specs=[spec(s, T_NA_Q), spec(s, T_NA_K), spec(s, T_NA_V),
                  spec(l, T_NA_Q), spec(l, T_NA_K), spec(l, T_NA_V),
                  pl.BlockSpec((1, 1, WIN_R, 2 * GRID_W, WIN_R * GRID_W), lambda bi, p: (li, p, 0, 0, 0))],
        out_specs=[pl.BlockSpec((1, s, 2 * NA_HD), lambda bi, p: (bi, 0, p)),
                   pl.BlockSpec((1, l, 2 * NA_HD), lambda bi, p: (bi, 0, p))],
        out_shape=[jax.ShapeDtypeStruct((b, s, GROUP_W), jnp.bfloat16),
                   jax.ShapeDtypeStruct((b, l, GROUP_W), jnp.bfloat16)],
        compiler_params=_params("parallel", "parallel"),
        name="natten",
    )(z, z, z, zc, zc, zc, bias)


HALO = 16
MERGE_PARTS = 2


def _pack_halves(h):
    half = h.shape[1] // 2
    lo = pltpu.bitcast(_bf16(h[:, :half]).astype(jnp.float32), jnp.uint32)
    hi = pltpu.bitcast(_bf16(h[:, half:]).astype(jnp.float32), jnp.uint32)
    return (lo >> 16) | (hi & jnp.uint32(0xFFFF0000))


def _unpack_halves(w):
    lo = pltpu.bitcast(w << 16, jnp.float32)
    hi = pltpu.bitcast(w & jnp.uint32(0xFFFF0000), jnp.float32)
    return _bf16(jnp.concatenate([lo, hi], axis=1))


def _sigmoid(v):
    return 0.5 * jnp.tanh(0.5 * v) + 0.5


def _merge_kernel(zb_ref, zc_ref, zu_ref, pc_ref, pu_ref, nc_ref, nu_ref, cw_ref,
                  ofw_ref, obw_ref, zg_ref, hgn_ref, g128_ref, yna_ref, *rest):
    n_gate = len(rest) - 13
    gate_refs, rest = rest[:n_gate], rest[n_gate:]
    (x_ref, gate_ref, wa_ref, wb_ref, wc_ref, wo_ref, n2_ref, shift_ref, scale_ref, wr_ref,
     xo_ref, hp_ref, lg_ref) = rest
    per_gate = n_gate // N_GATES
    i = pl.program_id(1)
    tm = zb_ref.shape[1]
    f32 = jnp.float32

    v = zc_ref[0].astype(f32) * zu_ref[0].astype(f32)
    vp = pc_ref[0, HALO - 1:HALO].astype(f32) * pu_ref[0, HALO - 1:HALO].astype(f32)
    vn = nc_ref[0, 0:1].astype(f32) * nu_ref[0, 0:1].astype(f32)
    vp = jnp.where(i == 0, 0.0, vp)
    vn = jnp.where(i == pl.num_programs(1) - 1, 0.0, vn)
    row = lax.broadcasted_iota(jnp.int32, v.shape, 0)
    v_prev = jnp.where(row == 0, vp, pltpu.roll(v, 1, 0))
    v_next = jnp.where(row == tm - 1, vn, pltpu.roll(v, tm - 1, 0))
    cw = cw_ref[...]
    y_cv = zb_ref[0].astype(f32) * (cw[0:1] * v_prev + cw[1:2] * v + cw[2:3] * v_next)

    parts = [slice(p * tm // MERGE_PARTS, (p + 1) * tm // MERGE_PARTS) for p in range(MERGE_PARTS)]
    o = [ofw_ref[0, rs, :] + obw_ref[0, rs, :] for rs in parts]
    ms = [_dot(_bf16(v * v), g128_ref[...]) for v in o]
    y_hg = [v * lax.rsqrt(m_ + EPS) * hgn_ref[...] * _silu(zg_ref[0, rs, :].astype(f32))
            for v, m_, rs in zip(o, ms, parts)]
    gates = [[jnp.concatenate([r[0, rs, :] for r in gate_refs[k * per_gate:(k + 1) * per_gate]], axis=1).astype(f32)
              for k in range(N_GATES)] for rs in parts]
    m = [g[0] * _dot(_bf16(y_cv[rs]), wa_ref[0]) + g[1] * _dot(_bf16(yh), wb_ref[0])
         + g[2] * _dot(yna_ref[0, rs, :], wc_ref[0]) for g, yh, rs in zip(gates, y_hg, parts)]
    x_new = [x_ref[0, rs, :] + gate_ref[0, 0] * _dot(_bf16(v), wo_ref[0]) for v, rs in zip(m, parts)]
    h2 = [_modnorm(v, n2_ref[...], shift_ref[0, 0], scale_ref[0, 0]) for v in x_new]
    logits = [_dot_nt(wr_ref[0], _bf16(v)) for v in h2]
    for rs, xn, h in zip(parts, x_new, h2):
        xo_ref[0, rs, :] = xn
        hp_ref[0, rs, :] = _pack_halves(h)
    lg_ref[0] = jnp.concatenate(logits, axis=1)


def merge(z, o_fw, o_bw, y_na, x, mod, conv_w, hg_norm, w_a, w_b, w_c, w_o, norm2, w_rt, li, tm):
    b, l, d = x.shape
    e = w_rt.shape[1]
    layer = lambda a: pl.BlockSpec((1,) + a.shape[1:], lambda bi, i: (li,) + (0,) * (a.ndim - 1))
    nt = l // tm
    per = tm // HALO
    n_halo = l // HALO
    assert d % GROUP_W == 0
    n_gate = N_GATES * d // GROUP_W
    bm = (lambda bi: bi) if mod.shape[0] == b else (lambda bi: 0)
    zt = lambda t: pl.BlockSpec((1, tm, GROUP_W), lambda bi, i: (bi, i, t))
    zprev = lambda t: pl.BlockSpec((1, HALO, GROUP_W), lambda bi, i: (bi, jnp.maximum(i * per - 1, 0), t))
    znext = lambda t: pl.BlockSpec((1, HALO, GROUP_W),
                                   lambda bi, i: (bi, jnp.minimum((i + 1) * per, n_halo - 1), t))
    act = lambda w: pl.BlockSpec((1, tm, w), lambda bi, i: (bi, i, 0))
    modrow = lambda k: pl.BlockSpec((1, 1, 1, d), lambda bi, i: (bm(bi), k, 0, 0))
    full = lambda a: pl.BlockSpec(a.shape, lambda bi, i: (0,) * a.ndim)
    g128 = _block_diag_mean(GROUP_W, HG_D)
    hgn = jnp.tile(hg_norm.reshape(1, -1), (1, GROUP_W // HG_D))
    n2 = norm2.reshape(1, d)
    return pl.pallas_call(
        _merge_kernel,
        grid=(b, nt),
        in_specs=[zt(T_CONV_B), zt(T_CONV_C), zt(T_CONV_U),
                  zprev(T_CONV_C), zprev(T_CONV_U), znext(T_CONV_C), znext(T_CONV_U), full(conv_w),
                  act(GROUP_W), act(GROUP_W), zt(T_HG_G), full(hgn), full(g128), act(GROUP_W)]
                 + [zt(T_GATES + k) for k in range(n_gate)]
                 + [act(d), modrow(2), layer(w_a), layer(w_b), layer(w_c), layer(w_o),
                  full(n2), modrow(3), modrow(4), layer(w_rt)],
        out_specs=[act(d), act(d // 2), pl.BlockSpec((1, e, tm), lambda bi, i: (bi, 0, i))],
        out_shape=[jax.ShapeDtypeStruct((b, l, d), jnp.float32),
                   jax.ShapeDtypeStruct((b, l, d // 2), jnp.uint32),
                   jax.ShapeDtypeStruct((b, e, l), jnp.float32)],
        compiler_params=_params("parallel", "arbitrary"),
        name="merge",
    )(z, z, z, z, z, z, z, conv_w, o_fw, o_bw, z, hgn, g128, y_na, *([z] * n_gate),
      x, mod, w_a, w_b, w_c, w_o, n2, mod, mod, w_rt)


LANES = 128
ROUTE_K_CHUNK = 1024


def _count(mask):
    return jnp.sum(jnp.where(mask, 1.0, 0.0), axis=1, keepdims=True)


def _route_kernel(lg_ref, tmat_ref, excl_ref, slot_ref, idx_ref, wgt_ref, starts_ref, table_ref, *, cap, tile):
    f32 = jnp.float32
    lg = lg_ref[0]
    e, s = lg.shape
    ex = jnp.exp(lg - jnp.max(lg, axis=0, keepdims=True))
    aff = ex / jnp.sum(ex, axis=0, keepdims=True)
    bits = pltpu.bitcast(aff, jnp.int32)

    def thr_bit(it, thr):
        cand = thr | (jnp.int32(1) << (30 - it))
        return jnp.where(_count(bits >= cand) >= cap, cand, thr)

    thr = lax.fori_loop(0, 31, thr_bit, jnp.zeros((e, 1), jnp.int32))
    gt = bits > thr
    eq = bits == thr
    need = cap - _count(gt)
    tok = lax.broadcasted_iota(jnp.int32, (e, s), 1)
    nbits = s.bit_length()

    def end_bit(it, end):
        cand = end + (jnp.int32(1) << (nbits - 1 - it))
        ok = (cand <= s) & (_count(eq & (tok < cand)) <= need)
        return jnp.where(ok, cand, end)

    end = lax.fori_loop(0, nbits, end_bit, jnp.zeros((e, 1), jnp.int32))
    sel = gt | (eq & (tok < end))

    self = jnp.where(sel, 1.0, 0.0)
    offs = jnp.zeros((e, 1), f32)
    pieces = []
    tile_lane = lax.broadcasted_iota(jnp.int32, (e, LANES), 1)
    starts = jnp.zeros((e, LANES), f32)
    for c in range(s // LANES):
        if (c * LANES) % tile == 0:
            starts = jnp.where(tile_lane == c * LANES // tile, offs, starts)
        blk = self[:, c * LANES:(c + 1) * LANES]
        pieces.append(_dot(_bf16(blk), excl_ref[...]) + offs)
        offs = offs + jnp.sum(blk, axis=1, keepdims=True)
    starts_ref[0] = jnp.where(tile_lane == s // tile, offs, starts).astype(jnp.int32)
    slot_ref[0] = jnp.where(sel, jnp.concatenate(pieces, axis=1), -1.0)

    a_t = aff.T
    plane = lax.broadcasted_iota(jnp.int32, (e, LANES), 1) - lax.broadcasted_iota(jnp.int32, (e, LANES), 0)
    table = tmat_ref[...].astype(f32)
    for piece, part in enumerate(_split3(a_t)):
        table = table + _dot(part, _bf16(jnp.where(plane == 2 + piece * e, 1.0, 0.0)))
    table_ref[...] = _bf16(table)

    piota = lax.broadcasted_iota(jnp.int32, (cap, 1), 0).astype(f32)
    lane = lax.broadcasted_iota(jnp.int32, (cap, LANES), 1)
    kc = min(ROUTE_K_CHUNK, s)
    idx_ref[0] = jnp.zeros((cap, LANES), jnp.int32)
    wgt_ref[0] = jnp.zeros((cap, LANES), f32)

    def expert(ee, carry):
        res = jnp.zeros((cap, LANES), f32)
        for c in range(s // kc):
            row = slot_ref[0, pl.ds(ee, 1), c * kc:(c + 1) * kc]
            onehot = _bf16(jnp.where(row == piota, 1.0, 0.0))
            res = res + _dot(onehot, table_ref[c * kc:(c + 1) * kc, :])
        tok_idx = (res[:, 0:1] * 64.0 + res[:, 1:2]).astype(jnp.int32)
        idx_ref[0] = jnp.where(lane == ee, tok_idx, idx_ref[0])
        wgt = sum(pltpu.roll(res, LANES - 2 - piece * e, 1) for piece in range(3))
        wgt_ref[0] = jnp.where(lane == ee, wgt, wgt_ref[0])
        return carry

    lax.fori_loop(0, e, expert, 0)


def route(lg, cap, tile):
    b, e, s = lg.shape
    assert tile % LANES == 0 and s % tile == 0 and s // tile < LANES
    t = jnp.arange(s)
    tmat = jnp.zeros((s, LANES), jnp.bfloat16).at[:, 0].set(_bf16(t // 64)).at[:, 1].set(_bf16(t % 64))
    a = jnp.arange(LANES)
    excl = _bf16(a[:, None] < a[None, :])
    spec = pl.BlockSpec((1, e, s), lambda bi: (bi, 0, 0))
    per_slot = pl.BlockSpec((1, cap, LANES), lambda bi: (bi, 0, 0))
    slot, idx, wgt, starts = pl.pallas_call(
        functools.partial(_route_kernel, cap=cap, tile=tile),
        grid=(b,),
        in_specs=[spec, pl.BlockSpec((s, LANES), lambda bi: (0, 0)), pl.BlockSpec((LANES, LANES), lambda bi: (0, 0))],
        out_specs=[spec, per_slot, per_slot, pl.BlockSpec((1, e, LANES), lambda bi: (bi, 0, 0))],
        out_shape=[jax.ShapeDtypeStruct((b, e, s), jnp.float32), jax.ShapeDtypeStruct((b, cap, LANES), jnp.int32),
                   jax.ShapeDtypeStruct((b, cap, LANES), jnp.float32), jax.ShapeDtypeStruct((b, e, LANES), jnp.int32)],
        scratch_shapes=[pltpu.VMEM((s, LANES), jnp.bfloat16)],
        compiler_params=_params("parallel"),
        name="route",
    )(lg, tmat, excl)
    return slot, idx[:, :, :e].transpose(0, 2, 1), wgt, starts[:, :, :s // tile + 1]


GATHER_GROUP = 8


def _gather_kernel(idx_ref, h_ref, o_ref, *, cap):
    def body(g, carry):
        base = pl.multiple_of(g * GATHER_GROUP, GATHER_GROUP)
        for u in range(GATHER_GROUP):
            t = idx_ref[0, 0, base + u]
            o_ref[0, 0, pl.ds(base + u, 1), :] = h_ref[0, pl.ds(t, 1), :]
        return carry

    lax.fori_loop(0, cap // GATHER_GROUP, body, 0)


def gather(idx, hp, out_shape, out_index):
    b, e, cap = idx.shape
    _, s, w = hp.shape
    return pl.pallas_call(
        functools.partial(_gather_kernel, cap=cap),
        grid=(b, e),
        in_specs=[pl.BlockSpec((1, 1, cap), lambda bi, ei: (bi * e + ei, 0, 0), memory_space=pltpu.SMEM),
                  pl.BlockSpec((1, s, w), lambda bi, ei: (bi, 0, 0))],
        out_specs=pl.BlockSpec((1, 1, cap, w), out_index),
        out_shape=jax.ShapeDtypeStruct(out_shape, jnp.uint32),
        compiler_params=_params("parallel", "arbitrary"),
        name="gather",
    )(idx.reshape(b * e, 1, cap), hp)


FFN_CHUNK = 512


def _ffn_body(xw, wgt, wg_ref, wu_ref, wd_ref):
    x = _unpack_halves(xw)
    f = wg_ref.shape[3]
    fc = min(FFN_CHUNK, f)
    acc = jnp.zeros((x.shape[0], wd_ref.shape[3]), jnp.float32)
    for c in range(f // fc):
        a = _dot(x, wg_ref[0, 0, :, c * fc:(c + 1) * fc])
        u = _dot(x, wu_ref[0, 0, :, c * fc:(c + 1) * fc])
        acc = acc + _dot(_bf16(_silu(a) * u), wd_ref[0, 0, c * fc:(c + 1) * fc, :])
    mine = lax.broadcasted_iota(jnp.int32, wgt.shape, 1) == pl.program_id(0)
    return _bf16(acc * jnp.sum(jnp.where(mine, wgt, 0.0), axis=1, keepdims=True))


def _ffn_kernel(xl_ref, al_ref, wg_ref, wu_ref, wd_ref, yl_ref):
    yl_ref[0, 0] = _ffn_body(xl_ref[0, 0], al_ref[0], wg_ref, wu_ref, wd_ref)


def _ffn_ctx_kernel(xl_ref, al_ref, xc_ref, ac_ref, wg_ref, wu_ref, wd_ref, yl_ref, yc_ref, *, nb):
    is_ctx = pl.program_id(1) == nb
    y = _ffn_body(jnp.where(is_ctx, xc_ref[0, 0], xl_ref[0, 0]), jnp.where(is_ctx, ac_ref[0], al_ref[0]),
                  wg_ref, wu_ref, wd_ref)

    @pl.when(jnp.logical_not(is_ctx))
    def _():
        yl_ref[0, 0] = y

    @pl.when(is_ctx)
    def _():
        yc_ref[0, 0] = y


def expert_ffn(xe, wgt, xe_c, wgt_c, w_g, w_u, w_d, li):
    nb, e, cap, w = xe.shape
    d, f = w_g.shape[2:]
    wspec = lambda shape: pl.BlockSpec((1, 1) + shape, lambda ei, bi: (li, ei, 0, 0))
    weights = [wspec((d, f)), wspec((d, f)), wspec((f, d))]
    lat = lambda width: pl.BlockSpec((1, 1, cap, width), lambda ei, bi: (jnp.minimum(bi, nb - 1), ei, 0, 0))
    lat_w = pl.BlockSpec((1, cap, LANES), lambda ei, bi: (jnp.minimum(bi, nb - 1), 0, 0))
    cspec = lambda width: pl.BlockSpec((1, 1, cap, width), lambda ei, bi: (0, ei, 0, 0))
    ctx_w = pl.BlockSpec((1, cap, LANES), lambda ei, bi: (0, 0, 0))
    if xe_c is None:
        return pl.pallas_call(
            _ffn_kernel, grid=(e, nb),
            in_specs=[lat(w), lat_w] + weights, out_specs=lat(d),
            out_shape=jax.ShapeDtypeStruct((nb, e, cap, d), jnp.bfloat16),
            compiler_params=_params("parallel", "arbitrary"), name="expert_ffn",
        )(xe, wgt, w_g, w_u, w_d), None
    assert xe_c.shape == (1, e, cap, w) and wgt_c.shape == (1, cap, LANES)
    return pl.pallas_call(
        functools.partial(_ffn_ctx_kernel, nb=nb), grid=(e, nb + 1),
        in_specs=[lat(w), lat_w, cspec(w), ctx_w] + weights, out_specs=[lat(d), cspec(d)],
        out_shape=[jax.ShapeDtypeStruct((nb, e, cap, d), jnp.bfloat16),
                   jax.ShapeDtypeStruct((1, e, cap, d), jnp.bfloat16)],
        compiler_params=_params("parallel", "arbitrary"), name="expert_ffn_ctx",
    )(xe, wgt, xe_c, wgt_c, w_g, w_u, w_d)


COMBINE_WIN = 128
SLOT_ALIGN = 16


def _combine_kernel(starts_ref, slot_ref, ye_ref, x_ref, gate_ref, o_ref, *, slots_per_sample):
    f32 = jnp.float32
    bi, ti = pl.program_id(0), pl.program_id(1)
    n_exp, n_slots, d = ye_ref.shape[1:]
    assert n_exp % 2 == 0
    win = min(COMBINE_WIN, n_slots)
    base = bi * slots_per_sample
    slot_t = slot_ref[0]
    tm = slot_t.shape[0]
    slot_t = jnp.where(slot_t < 0.0, -1.0, slot_t + base.astype(f32))
    expert_lane = lax.broadcasted_iota(jnp.int32, slot_t.shape, 1)
    lane = lax.broadcasted_iota(jnp.int32, (tm, win), 1).astype(f32)

    def column(e):
        col = jnp.sum(jnp.where(expert_lane == e, slot_t, 0.0), axis=1, keepdims=True)
        return jnp.broadcast_to(col, (tm, win))

    def first_window(e):
        lo = base + starts_ref[bi, e, ti]
        return jnp.minimum(lo // SLOT_ALIGN * SLOT_ALIGN, n_slots - win)

    def onehot(slot_b, nominal, start):
        hit = (slot_b - start.astype(f32) == lane) & (slot_b >= nominal.astype(f32))
        return _bf16(jnp.where(hit, 1.0, 0.0))

    def rows(e, start):
        return ye_ref[0, e, pl.ds(pl.multiple_of(start, SLOT_ALIGN), win), :]

    for e in range(0, n_exp, 2):
        ws = [first_window(e), first_window(e + 1)]
        picks = jnp.concatenate([onehot(column(e + j), ws[j], ws[j]) for j in range(2)], axis=1)
        term = _dot(picks, jnp.concatenate([rows(e + j, ws[j]) for j in range(2)], axis=0))
        if e == 0:
            o_ref[0] = term
        else:
            o_ref[0] += term

    def expert(e, carry):
        ws = first_window(e)
        hi = base + starts_ref[bi, e, ti + 1]

        def window(k, c2):
            nominal = ws + k * win
            start = jnp.minimum(nominal, n_slots - win)
            o_ref[0] += _dot(onehot(column(e), nominal, start), rows(e, start))
            return c2

        return lax.fori_loop(1, (hi - ws + win - 1) // win, window, carry)

    lax.fori_loop(0, n_exp, expert, 0)
    o_ref[0] = x_ref[0] + gate_ref[0, 0] * o_ref[0]


def combine(starts, slot_t, ye, x, mod, tm):
    b, l, d = x.shape
    e = slot_t.shape[2]
    pooled = ye.shape[0] == 1 and b > 1
    bm = (lambda bi: bi) if mod.shape[0] == b else (lambda bi: 0)
    tok = lambda w: pl.BlockSpec((1, tm, w), lambda bi, i, st: (bi, i, 0))
    ye_spec = pl.BlockSpec((1,) + ye.shape[1:], lambda bi, i, st: (0 if pooled else bi, 0, 0, 0),
                           pipeline_mode=pl.Buffered(1))
    return pl.pallas_call(
        functools.partial(_combine_kernel, slots_per_sample=ye.shape[2] // b if pooled else 0),
        grid_spec=pltpu.PrefetchScalarGridSpec(
            num_scalar_prefetch=1,
            grid=(b, l // tm),
            in_specs=[tok(e), ye_spec, tok(d),
                      pl.BlockSpec((1, 1, 1, d), lambda bi, i, st: (bm(bi), N_MOD - 1, 0, 0))],
            out_specs=tok(d)),
        out_shape=jax.ShapeDtypeStruct((b, l, d), jnp.float32),
        compiler_params=_params("parallel", "arbitrary"),
        name="combine",
    )(starts, slot_t, ye, x, mod)


TM_IN = 2048
TM_MERGE = 512
TM_COMBINE = 512
HG_ROWS = 1024


def _route_and_gather(hp, lg, xe_shape, xe_index, cap, tile):
    slot, idx, wgt, starts = route(lg, cap, tile)
    xe = gather(idx, hp, xe_shape, xe_index)
    return (starts, slot.transpose(0, 2, 1)), xe, wgt


def kernel(x, c, ctx, c_ctx, w_mod, b_mod, norm1, w_in, conv_w, hg_lb_logits, hg_norm, na_q_norm, na_k_norm, na_rpb,
           w_br_a, w_br_b, w_br_c, w_out, norm2, w_router, w_e_gate, w_e_up, w_e_down):
    b, s, d = x.shape
    l = ctx.shape[1]
    depth = w_mod.shape[0]
    e = w_router.shape[-1]
    cap = CAP_FACTOR * s // e
    cap_c = CAP_FACTOR * l // e
    assert b * cap_c == cap, "context rows of all samples fill one expert tile"

    lb_sm = jax.nn.softmax(hg_lb_logits.astype(jnp.float32), axis=0)
    lb_all = jnp.cumsum(lb_sm, axis=0) - lb_sm[0]
    rows = -(-(b + 1) // 8) * 8
    cc = jnp.zeros((rows, d), jnp.float32).at[:b].set(c).at[b].set(c_ctx)
    mod_all = modulation(cc, w_mod, b_mod).reshape(depth, rows, N_MOD, 1, d)

    w_in_r = _bf16(w_in)
    w_a, w_b, w_c, w_o = _bf16(w_br_a), _bf16(w_br_b), _bf16(w_br_c), _bf16(w_out)
    w_rt = _bf16(jnp.swapaxes(w_router, 1, 2))
    w_g, w_u, w_d = _bf16(w_e_gate), _bf16(w_e_up), _bf16(w_e_down)
    bias = jax.vmap(_natten_bias)(na_rpb)
    s0 = jnp.zeros((b, 2, HG_HEADS, HG_D, HG_D), jnp.float32)
    hg_rows = min(HG_ROWS, s)
    tm_in = min(TM_IN, s)
    tm_cmb = min(TM_COMBINE, s)

    xc = ctx
    for li in range(depth):
        last = li == depth - 1
        mod = mod_all[li, :b]
        mod_c = mod_all[li, b:b + 1]
        z = input_projection(x, mod, norm1[li], w_in_r, li, na_q_norm[li], na_k_norm[li], tm_in)
        zc = input_projection(xc.reshape(1, b * l, d), mod_c, norm1[li], w_in_r, li, na_q_norm[li], na_k_norm[li],
                              min(TM_IN, b * l)).reshape(b, l, -1)
        oc_fw, oc_bw, s_ctx = hgrn_scan(zc, lb_all[li], s0, l)
        o_fw, o_bw, _ = hgrn_scan(z, lb_all[li], s_ctx, hg_rows)
        y_na, yc_na = natten(z, zc, bias, li)

        mw = (conv_w[li], hg_norm[li], w_a, w_b, w_c, w_o, norm2[li], w_rt, li)
        x, hp, lg = merge(z, o_fw, o_bw, y_na, x, mod, *mw, TM_MERGE)
        plan, xe, wgt = _route_and_gather(hp, lg, (b, e, cap, d // 2), lambda bi, ei: (bi, ei, 0, 0), cap, tm_cmb)
        if last:
            ye, _ = expert_ffn(xe, wgt, None, None, w_g, w_u, w_d, li)
        else:
            xc, hpc, lgc = merge(zc, oc_fw, oc_bw, yc_na, xc, mod_c, *mw, l)
            plan_c, xe_c, wgt_c = _route_and_gather(hpc, lgc, (1, e, cap, d // 2), lambda bi, ei: (0, ei, bi, 0),
                                                    cap_c, l)
            ye, ye_c = expert_ffn(xe, wgt, xe_c, wgt_c.reshape(1, cap, LANES), w_g, w_u, w_d, li)
            xc = combine(*plan_c, ye_c, xc, mod_c, l)
        x = combine(*plan, ye, x, mod, tm_cmb)
    return x
```
